```python
import math
import jax
import jax.numpy as jnp
from jax import lax
import numpy as np

D_MODEL = 1024
BATCH = 32
SEQ = 256
DEPTH = 2
DEC_BATCH = 2
DEC_SEQ = 4096
PAST_LEN = 256

GRID_W = 64
POS_BASE = 10000.0
N_DIR = 2
CONV_W = 4
LN_EPS = 1e-5
RMS_EPS = 1e-6
H_A = 4
DK_A = 128
DV_A = 128
QK_A = H_A * DK_A
W_A = H_A * DV_A
QKV_A = 2 * QK_A + W_A
CHUNK_A = 64
W_B = 512
NB_B = 4
BW_B = W_B // NB_B
LRU_C = 8.0
H_C = 4
DK_C = 128
DV_C = 128
QK_C = H_C * DK_C
W_C = H_C * DV_C
CHUNK_C = 64
N_GROUPS = 4
E_PER_GROUP = 8
N_EXPERTS = N_GROUPS * E_PER_GROUP
TOP_K = 2
D_EXPERT = 256
DN_ALPHA = (2 * DEPTH) ** 0.25
DN_BETA = (8 * DEPTH) ** -0.25
IN_SIZES = (QKV_A, W_B, W_A, N_DIR * H_A, N_DIR * H_A, W_B, QK_C, QK_C, W_C, W_C, N_DIR * H_C, N_DIR * H_C, 3 * D_MODEL)
N_IN = sum(IN_SIZES)

kernel_name = 'hybrid_bidir_deltanet_rglru_mlstm_hmoe_step'


def _flip(a):
    return jnp.flip(a, axis=1)


def _split(a, sizes):
    parts, start = [], 0
    for s in sizes:
        parts.append(a[..., start:start + s])
        start += s
    return parts


def layer_norm(x, g=None, b=None):
    xf = x.astype(jnp.float32)
    mu = jnp.mean(xf, axis=-1, keepdims=True)
    var = jnp.mean(jnp.square(xf - mu), axis=-1, keepdims=True)
    y = (xf - mu) * lax.rsqrt(var + LN_EPS)
    if g is not None:
        y = y * g + b
    return y.astype(x.dtype)


def rms_norm(x, w):
    xf = x.astype(jnp.float32)
    y = xf * lax.rsqrt(jnp.mean(jnp.square(xf), axis=-1, keepdims=True) + RMS_EPS)
    return (y * w).astype(x.dtype)


def _l2norm(a):
    return a * lax.rsqrt(jnp.sum(jnp.square(a), axis=-1, keepdims=True) + RMS_EPS)


def dwconv(x, w):
    k_w, t = w.shape[0], x.shape[1]
    xp = jnp.pad(x, ((0, 0), ((k_w - 1) // 2, k_w // 2), (0, 0)))
    return sum(xp[:, j:j + t] * w[j] for j in range(k_w))


def grid_pos_embed(n_tokens):
    rows = n_tokens // GRID_W
    r, col = jnp.meshgrid(jnp.arange(rows, dtype=jnp.float32), jnp.arange(GRID_W, dtype=jnp.float32), indexing='ij')
    quarter = D_MODEL // 4
    freqs = jnp.exp(-math.log(POS_BASE) * jnp.arange(quarter, dtype=jnp.float32) / quarter)
    ar = r.reshape(-1, 1) * freqs
    ac = col.reshape(-1, 1) * freqs
    return jnp.concatenate([jnp.sin(ar), jnp.cos(ar), jnp.sin(ac), jnp.cos(ac)], axis=-1)


def _to_chunks(a, chunk):
    b, t = a.shape[:2]
    return jnp.moveaxis(a.reshape(b, t // chunk, chunk, *a.shape[2:]), 2, 3)


def _from_chunks(a):
    b, n, h, c, d = a.shape
    return jnp.moveaxis(a, 3, 2).reshape(b, n * c, h, d)


def gated_delta_chunked(q, k, v, g, beta, s0):
    c = CHUNK_A
    qc, kc, vc, gc, bc = (_to_chunks(a, c) for a in (q, k, v, g, beta))
    G = jnp.cumsum(gc, axis=-1)
    causal = jnp.tril(jnp.ones((c, c), dtype=bool))
    strict = jnp.tril(jnp.ones((c, c), dtype=bool), -1)
    decay = jnp.exp(jnp.where(causal, G[..., :, None] - G[..., None, :], -jnp.inf))
    kk = jnp.einsum('bnhid,bnhjd->bnhij', kc, kc)
    strict_l = jnp.where(strict, bc[..., :, None] * kk * decay, 0.0)
    eye = jnp.eye(c, dtype=strict_l.dtype)
    rhs = jnp.concatenate([vc * bc[..., None], kc * (bc * jnp.exp(G))[..., None]], axis=-1)
    sol = lax.linalg.triangular_solve(eye + strict_l, rhs, left_side=True, lower=True)
    dv = v.shape[-1]
    u_v, u_k = sol[..., :dv], sol[..., dv:]
    qk = jnp.where(causal, jnp.einsum('bnhid,bnhjd->bnhij', qc, kc) * decay, 0.0)
    q_dec = qc * jnp.exp(G)[..., None]
    k_dec = kc * jnp.exp(G[..., -1:] - G)[..., None]
    g_end = jnp.exp(G[..., -1])

    def step(s, xs):
        uv, uk, qk_n, qd, kd, ge = xs
        u = uv - jnp.einsum('bhik,bhkv->bhiv', uk, s)
        o = jnp.einsum('bhik,bhkv->bhiv', qd, s) + jnp.einsum('bhij,bhjv->bhiv', qk_n, u)
        s = s * ge[..., None, None] + jnp.einsum('bhik,bhiv->bhkv', kd, u)
        return s, o

    xs = tuple(jnp.moveaxis(a, 1, 0) for a in (u_v, u_k, qk, q_dec, k_dec, g_end))
    s_end, o = lax.scan(step, s0, xs)
    return _from_chunks(jnp.moveaxis(o, 0, 1)), s_end


def delta_mixer(qkv, z, beta_pre, alpha_pre, a_log, dt_bias, norm_w, s0):
    f32 = jnp.float32
    bsz, t, _ = qkv.shape
    qkv = qkv.astype(f32)
    q = _l2norm(qkv[..., :QK_A].reshape(bsz, t, H_A, DK_A)) * DK_A ** -0.5
    k = _l2norm(qkv[..., QK_A:2 * QK_A].reshape(bsz, t, H_A, DK_A))
    v = qkv[..., 2 * QK_A:].reshape(bsz, t, H_A, DV_A)
    beta = jax.nn.sigmoid(beta_pre.astype(f32).reshape(bsz, t, N_DIR, H_A))
    g = -jnp.exp(a_log) * jax.nn.softplus(alpha_pre.astype(f32).reshape(bsz, t, N_DIR, H_A) + dt_bias)
    s0 = s0.astype(f32)
    o_f, s_f = gated_delta_chunked(q, k, v, g[:, :, 0], beta[:, :, 0], s0[:, 0])
    o_b, s_b = gated_delta_chunked(_flip(q), _flip(k), _flip(v), _flip(g[:, :, 1]), _flip(beta[:, :, 1]), s0[:, 1])
    o = o_f + _flip(o_b)
    o = rms_norm(o, norm_w) * jax.nn.silu(z.astype(f32).reshape(bsz, t, H_A, DV_A))
    return o.reshape(bsz, t, W_A).astype(z.dtype), jnp.stack([s_f, s_b], axis=1)


def _linear_combine(e1, e2):
    a1, b1 = e1
    a2, b2 = e2
    return a1 * a2, a2 * b1 + b2


def rglru(x, w_a, b_a, w_x, b_x, lam, h0):
    bsz, t, _ = x.shape
    xb = x.reshape(bsz, t, NB_B, BW_B)
    r = jax.nn.sigmoid(jnp.einsum('btni,nij->btnj', xb, w_a).reshape(bsz, t, W_B) + b_a)
    gi = jax.nn.sigmoid(jnp.einsum('btni,nij->btnj', xb, w_x).reshape(bsz, t, W_B) + b_x)
    log_a = -LRU_C * r * jax.nn.softplus(-lam)
    a = jnp.exp(log_a)
    b = jnp.sqrt(-jnp.expm1(2.0 * log_a)) * (gi * x)
    a_cum, h = lax.associative_scan(_linear_combine, (a, b), axis=1)
    h = h + a_cum * h0[:, None, :]
    return h, h[:, -1]


def lru_mixer(xc, gate_pre, p, h0):
    f32 = jnp.float32
    xc = xc.astype(f32)
    h0 = h0.astype(f32)
    h_f, s_f = rglru(xc, p['lru_wa'][0], p['lru_ba'][0], p['lru_wx'][0], p['lru_bx'][0], p['lru_lambda'][0], h0[:, 0])
    h_b, s_b = rglru(_flip(xc), p['lru_wa'][1], p['lru_ba'][1], p['lru_wx'][1], p['lru_bx'][1], p['lru_lambda'][1], h0[:, 1])
    y = (h_f + _flip(h_b)).astype(gate_pre.dtype) * jax.nn.gelu(gate_pre)
    return y, jnp.stack([s_f, s_b], axis=1)


def mlstm_chunked(q, k, v, i_pre, log_f, c0, n0, m0):
    c = CHUNK_C
    qc, kc, vc, ic, fc = (_to_chunks(a, c) for a in (q, k, v, i_pre, log_f))
    F = jnp.cumsum(fc, axis=-1)
    causal = jnp.tril(jnp.ones((c, c), dtype=bool))
    dmat = jnp.where(causal, F[..., :, None] - F[..., None, :] + ic[..., None, :], -jnp.inf)
    m_intra = jnp.max(dmat, axis=-1)
    qk = jnp.einsum('bnhid,bnhjd->bnhij', qc, kc)
    d_end = F[..., -1:] - F + ic
    m_end = jnp.max(d_end, axis=-1)

    def step(carry, xs):
        cs, ns, ms = carry
        f_n, d_n, mi, qk_n, q_n, k_n, v_n, de, me = xs
        inter = f_n + ms[..., None]
        mt = jnp.maximum(inter, mi)
        w_int = jnp.exp(inter - mt)
        pm = jnp.exp(d_n - mt[..., None]) * qk_n
        num = w_int[..., None] * jnp.einsum('bhik,bhkv->bhiv', q_n, cs) + jnp.einsum('bhij,bhjv->bhiv', pm, v_n)
        den = w_int * jnp.einsum('bhik,bhk->bhi', q_n, ns) + jnp.sum(pm, axis=-1)
        h = num / jnp.maximum(jnp.abs(den), jnp.exp(-mt))[..., None]
        inter_end = f_n[..., -1] + ms
        m_new = jnp.maximum(inter_end, me)
        w_tok = jnp.exp(de - m_new[..., None])
        s_int = jnp.exp(inter_end - m_new)
        c_new = s_int[..., None, None] * cs + jnp.einsum('bhi,bhik,bhiv->bhkv', w_tok, k_n, v_n)
        n_new = s_int[..., None] * ns + jnp.einsum('bhi,bhik->bhk', w_tok, k_n)
        return (c_new, n_new, m_new), h

    xs = tuple(jnp.moveaxis(a, 1, 0) for a in (F, dmat, m_intra, qk, qc, kc, vc, d_end, m_end))
    (c_e, n_e, m_e), h = lax.scan(step, (c0, n0, m0), xs)
    return _from_chunks(jnp.moveaxis(h, 0, 1)), (c_e, n_e, m_e)


def mlstm_mixer(q, k, v, o_pre, i_pre, f_pre, b_i, b_f, norm_w, c0, n0, m0):
    f32 = jnp.float32
    bsz, t, _ = q.shape
    q = q.astype(f32).reshape(bsz, t, H_C, DK_C)
    k = k.astype(f32).reshape(bsz, t, H_C, DK_C) * DK_C ** -0.5
    v = v.astype(f32).reshape(bsz, t, H_C, DV_C)
    i_t = i_pre.astype(f32).reshape(bsz, t, N_DIR, H_C) + b_i
    lf = jax.nn.log_sigmoid(f_pre.astype(f32).reshape(bsz, t, N_DIR, H_C) + b_f)
    c0, n0, m0 = c0.astype(f32), n0.astype(f32), m0.astype(f32)
    h_f, st_f = mlstm_chunked(q, k, v, i_t[:, :, 0], lf[:, :, 0], c0[:, 0], n0[:, 0], m0[:, 0])
    h_b, st_b = mlstm_chunked(_flip(q), _flip(k), _flip(v), _flip(i_t[:, :, 1]), _flip(lf[:, :, 1]), c0[:, 1], n0[:, 1], m0[:, 1])
    h = layer_norm(h_f + _flip(h_b)) * norm_w.reshape(H_C, DV_C)
    y = jax.nn.sigmoid(o_pre) * h.reshape(bsz, t, W_C).astype(o_pre.dtype)
    states = tuple(jnp.stack([a, b], axis=1) for a, b in zip(st_f, st_b))
    return y, states


def token_mixer(h, p, init_state):
    s_delta, s_lru, s_c, s_n, s_m = init_state
    proj = h @ p['w_in']
    qkv_a, x_b, z_a, beta_a, alpha_a, g_b, q_c, k_c, v_c, o_c, i_c, f_c, gates = _split(proj, IN_SIZES)
    y_a, st_a = delta_mixer(jax.nn.silu(dwconv(qkv_a, p['conv_a'])), z_a, beta_a, alpha_a,
                            p['delta_a_log'], p['delta_dt_bias'], p['delta_norm'], s_delta)
    y_b, st_b = lru_mixer(dwconv(x_b, p['conv_b_w']) + p['conv_b_b'], g_b, p, s_lru)
    y_c, st_c = mlstm_mixer(q_c, k_c, v_c, o_c, i_c, f_c, p['mlstm_bi'], p['mlstm_bf'], p['mlstm_norm'], s_c, s_n, s_m)
    gate_a, gate_b, gate_c = jnp.split(jax.nn.sigmoid(gates), 3, axis=-1)
    merged = gate_a * (y_a @ p['w_pa']) + gate_b * (y_b @ p['w_pb']) + gate_c * (y_c @ p['w_pc'])
    return merged @ p['w_out'], (st_a, st_b, *st_c)


def hier_moe(h, p):
    f32 = jnp.float32
    bsz, t, _ = h.shape
    lg = (h @ p['w_rg'] + p['b_rg']).astype(f32)
    p_grp = jnp.max(jax.nn.softmax(lg, axis=-1), axis=-1, keepdims=True)
    g_sel = jnp.argmax(lg, axis=-1)
    le = (h @ p['w_re'] + p['b_re']).astype(f32).reshape(bsz, t, N_GROUPS, E_PER_GROUP)
    le_sel = jnp.einsum('btge,btg->bte', le, jax.nn.one_hot(g_sel, N_GROUPS, dtype=f32))
    top_v, top_i = lax.top_k(le_sel, TOP_K)
    w_top = jax.nn.softmax(top_v, axis=-1) * p_grp
    e_idx = g_sel[..., None] * E_PER_GROUP + top_i
    gate = jnp.einsum('btk,btke->bte', w_top, jax.nn.one_hot(e_idx, N_EXPERTS, dtype=f32)).astype(h.dtype)
    hid = jax.nn.silu(jnp.einsum('btd,edf->btef', h, p['w_e1'])) * jnp.einsum('btd,edf->btef', h, p['w_e3'])
    return jnp.einsum('btef,efd->btd', hid * gate[..., None], p['w_e2'])


def trunk_layer(x, mod, p, init_state):
    shift1, scale1, gate1, shift2, scale2, gate2 = (mod[:, i, None, :] for i in range(6))
    h = layer_norm(x) * (1.0 + scale1) + shift1
    mixed, final_state = token_mixer(h, p, init_state)
    x = layer_norm(DN_ALPHA * x + gate1 * mixed, p['ln1_g'], p['ln1_b'])
    h = layer_norm(x) * (1.0 + scale2) + shift2
    x = layer_norm(DN_ALPHA * x + gate2 * hier_moe(h, p), p['ln2_g'], p['ln2_b'])
    return x, final_state


def zero_states(nb):
    f32 = jnp.float32
    return (jnp.zeros((nb, N_DIR, H_A, DK_A, DV_A), f32), jnp.zeros((nb, N_DIR, W_B), f32),
            jnp.zeros((nb, N_DIR, H_C, DK_C, DV_C), f32), jnp.zeros((nb, N_DIR, H_C, DK_C), f32),
            jnp.zeros((nb, N_DIR, H_C), f32))


def setup_inputs(seed: int = 0) -> dict:
    key = jax.random.key(seed)
    k = jax.random.split(key, 48)
    f32 = jnp.float32

    def nrm(i, shape, scale=1.0):
        return scale * jax.random.normal(k[i], shape, f32)

    def uni(i, shape, lo, hi):
        return jax.random.uniform(k[i], shape, f32, lo, hi)

    dt = jnp.exp(uni(14, (DEPTH, N_DIR, H_A), math.log(1e-3), math.log(1e-1)))
    a_lru = uni(22, (DEPTH, N_DIR, W_B), 0.9, 0.999) ** (1.0 / LRU_C)
    return {
        'x_prompt': nrm(0, (BATCH, SEQ, D_MODEL)),
        'x_sample': nrm(1, (DEC_BATCH, DEC_SEQ, D_MODEL)),
        'state_delta': nrm(2, (DEC_BATCH, DEPTH, N_DIR, H_A, DK_A, DV_A), 0.05),
        'state_lru': nrm(3, (DEC_BATCH, DEPTH, N_DIR, W_B), 0.5),
        'state_mlstm_C': nrm(4, (DEC_BATCH, DEPTH, N_DIR, H_C, DK_C, DV_C), 0.05),
        'state_mlstm_n': nrm(5, (DEC_BATCH, DEPTH, N_DIR, H_C, DK_C), 0.1),
        'state_mlstm_m': 1.0 + nrm(6, (DEC_BATCH, DEPTH, N_DIR, H_C), 0.5),
        'c': nrm(7, (DEC_BATCH, D_MODEL)),
        'c_ctx': nrm(8, (D_MODEL,)),
        'w_mod': nrm(9, (DEPTH, D_MODEL, 6 * D_MODEL), 0.5 * D_MODEL ** -0.5),
        'b_mod': nrm(10, (DEPTH, 6 * D_MODEL), 0.01),
        'w_in': nrm(11, (DEPTH, D_MODEL, N_IN), D_MODEL ** -0.5),
        'conv_a': nrm(12, (DEPTH, CONV_W, QKV_A), CONV_W ** -0.5),
        'delta_a_log': jnp.log(uni(13, (DEPTH, N_DIR, H_A), 1.0, 16.0)),
        'delta_dt_bias': dt + jnp.log(-jnp.expm1(-dt)),
        'delta_norm': 1.0 + nrm(15, (DEPTH, DV_A), 0.02),
        'conv_b_w': nrm(16, (DEPTH, CONV_W, W_B), CONV_W ** -0.5),
        'conv_b_b': nrm(17, (DEPTH, W_B), 0.01),
        'lru_wa': nrm(18, (DEPTH, N_DIR, NB_B, BW_B, BW_B), BW_B ** -0.5),
        'lru_ba': nrm(19, (DEPTH, N_DIR, W_B), 0.01),
        'lru_wx': nrm(20, (DEPTH, N_DIR, NB_B, BW_B, BW_B), BW_B ** -0.5),
        'lru_bx': nrm(21, (DEPTH, N_DIR, W_B), 0.01),
        'lru_lambda': jnp.log(a_lru) - jnp.log1p(-a_lru),
        'mlstm_bi': nrm(23, (DEPTH, N_DIR, H_C), 0.1),
        'mlstm_bf': uni(24, (DEPTH, N_DIR, H_C), 3.0, 6.0),
        'mlstm_norm': 1.0 + nrm(25, (DEPTH, W_C), 0.02),
        'w_pa': nrm(26, (DEPTH, W_A, D_MODEL), W_A ** -0.5),
        'w_pb': nrm(27, (DEPTH, W_B, D_MODEL), W_B ** -0.5),
        'w_pc': nrm(28, (DEPTH, W_C, D_MODEL), W_C ** -0.5),
        'w_out': nrm(29, (DEPTH, D_MODEL, D_MODEL), DN_BETA * D_MODEL ** -0.5),
        'ln1_g': 1.0 + nrm(30, (DEPTH, D_MODEL), 0.02),
        'ln1_b': nrm(31, (DEPTH, D_MODEL), 0.02),
        'ln2_g': 1.0 + nrm(32, (DEPTH, D_MODEL), 0.02),
        'ln2_b': nrm(33, (DEPTH, D_MODEL), 0.02),
        'w_rg': nrm(34, (DEPTH, D_MODEL, N_GROUPS), D_MODEL ** -0.5),
        'b_rg': nrm(35, (DEPTH, N_GROUPS), 0.01),
        'w_re': nrm(36, (DEPTH, D_MODEL, N_EXPERTS), D_MODEL ** -0.5),
        'b_re': nrm(37, (DEPTH, N_EXPERTS), 0.01),
        'w_e1': nrm(38, (DEPTH, N_EXPERTS, D_MODEL, D_EXPERT), D_MODEL ** -0.5),
        'w_e3': nrm(39, (DEPTH, N_EXPERTS, D_MODEL, D_EXPERT), D_MODEL ** -0.5),
        'w_e2': nrm(40, (DEPTH, N_EXPERTS, D_EXPERT, D_MODEL), DN_BETA * D_EXPERT ** -0.5),
    }


def reference(x_prompt, x_sample, state_delta, state_lru, state_mlstm_C, state_mlstm_n, state_mlstm_m, c, c_ctx,
              w_mod, b_mod, w_in, conv_a, delta_a_log, delta_dt_bias, delta_norm, conv_b_w, conv_b_b,
              lru_wa, lru_ba, lru_wx, lru_bx, lru_lambda, mlstm_bi, mlstm_bf, mlstm_norm,
              w_pa, w_pb, w_pc, w_out, ln1_g, ln1_b, ln2_g, ln2_b, w_rg, b_rg, w_re, b_re, w_e1, w_e3, w_e2):
    layers = [dict(w_in=w_in[l], conv_a=conv_a[l], delta_a_log=delta_a_log[l], delta_dt_bias=delta_dt_bias[l],
                   delta_norm=delta_norm[l], conv_b_w=conv_b_w[l], conv_b_b=conv_b_b[l], lru_wa=lru_wa[l],
                   lru_ba=lru_ba[l], lru_wx=lru_wx[l], lru_bx=lru_bx[l], lru_lambda=lru_lambda[l],
                   mlstm_bi=mlstm_bi[l], mlstm_bf=mlstm_bf[l], mlstm_norm=mlstm_norm[l], w_pa=w_pa[l],
                   w_pb=w_pb[l], w_pc=w_pc[l], w_out=w_out[l], ln1_g=ln1_g[l], ln1_b=ln1_b[l], ln2_g=ln2_g[l],
                   ln2_b=ln2_b[l], w_rg=w_rg[l], b_rg=b_rg[l], w_re=w_re[l], b_re=b_re[l], w_e1=w_e1[l],
                   w_e3=w_e3[l], w_e2=w_e2[l]) for l in range(DEPTH)]

    def modulation(cond, l):
        return (jax.nn.silu(cond) @ w_mod[l] + b_mod[l]).reshape(-1, 6, D_MODEL)

    xp = x_prompt
    zero = zero_states(x_prompt.shape[0])
    ctx_states = []
    for l in range(DEPTH):
        xp, st = trunk_layer(xp, modulation(c_ctx[None, :], l), layers[l], zero)
        ctx_states.append(st)
    new_delta, new_lru, new_mc, new_mn, new_mm = (jnp.stack([st[i] for st in ctx_states], axis=1) for i in range(5))

    xs = x_sample + grid_pos_embed(x_sample.shape[1]).astype(x_sample.dtype)
    for l in range(DEPTH):
        cached = (state_delta[:, l], state_lru[:, l], state_mlstm_C[:, l], state_mlstm_n[:, l], state_mlstm_m[:, l])
        xs, _ = trunk_layer(xs, modulation(c, l), layers[l], cached)

    return (xp, xs, new_delta, new_lru, new_mc, new_mn, new_mm)
```

```python
import functools
import math
from typing import NamedTuple

import jax
import jax.numpy as jnp
from jax import lax
from jax.experimental import pallas as pl
from jax.experimental.pallas import tpu as pltpu

F32 = jnp.float32
BF16 = jnp.bfloat16

D_MODEL = 1024
DEPTH = 2
GRID_W = 64
POS_BASE = 10000.0
CONV_W = 4
LN_EPS = 1e-5
RMS_EPS = 1e-6
N_HEADS = 4
HEAD_DIM = 128
W_MIX = N_HEADS * HEAD_DIM
CHUNK = 64
W_B = 512
NB_B = 4
BW_B = W_B // NB_B
LRU_C = 8.0
N_GROUPS = 4
E_PER_GROUP = 8
N_EXPERTS = N_GROUPS * E_PER_GROUP
D_EXPERT = 256
DN_ALPHA = (2 * DEPTH) ** 0.25

LANES = 128
VMEM_LIMIT = 56 * 1024 * 1024

OFF_GATES, OFF_QKV, OFF_QC, OFF_KC, OFF_VC, OFF_Z, OFF_GB, OFF_OC = 0, 3072, 4608, 5120, 5632, 6144, 6656, 7168
N_BIG = OFF_OC + W_MIX
N_F32 = W_B + LANES
SM_BETA, SM_ALPHA, SM_I, SM_F = 0, 8, 16, 24


class Rows(NamedTuple):
    n_ctx: int
    ctx_len: int
    n_lat: int
    lat_len: int

    @property
    def ctx_rows(self):
        return self.n_ctx * self.ctx_len

    @property
    def total(self):
        return self.ctx_rows + self.n_lat * self.lat_len


def _cparams(sem):
    return pltpu.CompilerParams(dimension_semantics=sem, vmem_limit_bytes=VMEM_LIMIT)


def _sigmoid(x):
    return 1.0 / (1.0 + jnp.exp(-x))


def _silu(x):
    return x * _sigmoid(x)


def _softplus(x):
    return jnp.maximum(x, 0.0) + jnp.log1p(jnp.exp(-jnp.abs(x)))


def _ln(x):
    mu = jnp.mean(x, axis=-1, keepdims=True)
    xc = x - mu
    var = jnp.mean(xc * xc, axis=-1, keepdims=True)
    return xc * lax.rsqrt(var + LN_EPS)


def _bdot(a, b):
    return jnp.dot(a.astype(BF16), b.astype(BF16), preferred_element_type=F32)


def _bdot_nt(a, b):
    return lax.dot_general(a.astype(BF16), b.astype(BF16), (((1,), (1,)), ((), ())), preferred_element_type=F32)


def _bdot_tn(a, b):
    return lax.dot_general(a.astype(BF16), b.astype(BF16), (((0,), (0,)), ((), ())), preferred_element_type=F32)


def _split2(a):
    hi = a.astype(BF16)
    lo = (a - hi.astype(F32)).astype(BF16)
    return hi, lo


def _split3(a):
    hi = a.astype(BF16)
    r = a - hi.astype(F32)
    mid = r.astype(BF16)
    lo = (r - mid.astype(F32)).astype(BF16)
    return hi, mid, lo


def _dot3(a, b):
    ah, al = _split2(a)
    bh, bl = _split2(b)
    d = functools.partial(jnp.dot, preferred_element_type=F32)
    return d(ah, bh) + (d(ah, bl) + d(al, bh))


def _dot_exact_lhs(m_bf16, x):
    xh, xm, xl = _split3(x)
    d = functools.partial(jnp.dot, preferred_element_type=F32)
    return d(m_bf16, xh) + (d(m_bf16, xm) + d(m_bf16, xl))


def _cond_index(row0, rows: Rows):
    return jnp.maximum(0, (row0 - rows.ctx_rows + rows.lat_len) // rows.lat_len)


def _prep_kernel(xp_ref, xs_ref, pos_ref, o_ref, *, n_ctx_tiles):
    i = pl.program_id(0)

    @pl.when(i < n_ctx_tiles)
    def _():
        o_ref[...] = xp_ref[...]

    @pl.when(i >= n_ctx_tiles)
    def _():
        o_ref[...] = xs_ref[...] + pos_ref[...]


def _prep(xp2, xs2, pos, rows: Rows, tm):
    n_ctx_tiles = rows.ctx_rows // tm
    n_tiles = rows.total // tm
    pos_tiles = rows.lat_len // tm
    return pl.pallas_call(
        functools.partial(_prep_kernel, n_ctx_tiles=n_ctx_tiles),
        grid=(n_tiles,),
        in_specs=[
            pl.BlockSpec((tm, D_MODEL), lambda i: (jnp.minimum(i, n_ctx_tiles - 1), 0)),
            pl.BlockSpec((tm, D_MODEL), lambda i: (jnp.maximum(i - n_ctx_tiles, 0), 0)),
            pl.BlockSpec((tm, D_MODEL), lambda i: (jnp.maximum(i - n_ctx_tiles, 0) % pos_tiles, 0)),
        ],
        out_specs=pl.BlockSpec((tm, D_MODEL), lambda i: (i, 0)),
        out_shape=jax.ShapeDtypeStruct((rows.total, D_MODEL), F32),
        compiler_params=_cparams(("arbitrary",)),
        name="prep",
    )(xp2, xs2, pos)


def _mod_kernel(c_ref, w_ref, b_ref, o_ref):
    o_ref[0] = _bdot(_silu(c_ref[...]), w_ref[0]) + b_ref[0]


def _modulation(cond8, w_mod, b_mod):
    tn = 1024
    n6 = 6 * D_MODEL
    return pl.pallas_call(
        _mod_kernel,
        grid=(DEPTH, n6 // tn),
        in_specs=[
            pl.BlockSpec((8, D_MODEL), lambda l, n: (0, 0)),
            pl.BlockSpec((1, D_MODEL, tn), lambda l, n: (l, 0, n)),
            pl.BlockSpec((1, 1, tn), lambda l, n: (l, 0, n)),
        ],
        out_specs=pl.BlockSpec((1, 8, tn), lambda l, n: (l, 0, n)),
        out_shape=jax.ShapeDtypeStruct((DEPTH, 8, n6), F32),
        compiler_params=_cparams(("arbitrary", "arbitrary")),
        name="modulation",
    )(cond8, w_mod, b_mod.reshape(DEPTH, 1, n6))


def _proj_kernel(x_ref, mod_ref, wb_ref, wf_ref, ob_ref, of_ref, h_scr):
    n = pl.program_id(1)

    @pl.when(n == 0)
    def _():
        shift1 = mod_ref[0, 0:1, :]
        scale1 = mod_ref[0, 1:2, :]
        h = (_ln(x_ref[...]) * (1.0 + scale1) + shift1).astype(BF16)
        h_scr[...] = h
        of_ref[...] = jnp.dot(h, wf_ref[...], preferred_element_type=F32)

    ob_ref[...] = jnp.dot(h_scr[...], wb_ref[...], preferred_element_type=F32).astype(BF16)


def _projection(x, mod_l, w_big, w_f32, rows: Rows, tm, tn):
    nt = rows.total
    return pl.pallas_call(
        _proj_kernel,
        grid=(nt // tm, N_BIG // tn),
        in_specs=[
            pl.BlockSpec((tm, D_MODEL), lambda i, n: (i, 0)),
            pl.BlockSpec((1, 6, D_MODEL), lambda i, n: (_cond_index(i * tm, rows), 0, 0)),
            pl.BlockSpec((D_MODEL, tn), lambda i, n: (0, n)),
            pl.BlockSpec((D_MODEL, N_F32), lambda i, n: (0, 0)),
        ],
        out_specs=[
            pl.BlockSpec((tm, tn), lambda i, n: (i, n)),
            pl.BlockSpec((tm, N_F32), lambda i, n: (i, 0)),
        ],
        out_shape=[
            jax.ShapeDtypeStruct((nt, N_BIG), BF16),
            jax.ShapeDtypeStruct((nt, N_F32), F32),
        ],
        scratch_shapes=[pltpu.VMEM((tm, D_MODEL), BF16)],
        compiler_params=_cparams(("arbitrary", "arbitrary")),
        name="projection",
    )(x, mod_l, w_big, w_f32)


def _conv_taps(x, w_ref, seq_len):
    n = x.shape[0]
    pos = lax.broadcasted_iota(jnp.int32, x.shape, 0) & (seq_len - 1)
    xm1 = jnp.where(pos >= 1, pltpu.roll(x, 1, 0), 0.0)
    xp1 = jnp.where(pos <= seq_len - 2, pltpu.roll(x, n - 1, 0), 0.0)
    xp2 = jnp.where(pos <= seq_len - 3, pltpu.roll(x, n - 2, 0), 0.0)
    return xm1 * w_ref[0:1, :] + x * w_ref[1:2, :] + xp1 * w_ref[2:3, :] + xp2 * w_ref[3:4, :]


def _conv_a_kernel(x_ref, w_ref, o_ref, *, rows: Rows, br):
    b = pl.program_id(0)
    j = pl.program_id(1)
    seq_len = jnp.where(b * br < rows.ctx_rows, rows.ctx_len, rows.lat_len)
    y = _silu(_conv_taps(x_ref[...].astype(F32), w_ref[0], seq_len))
    nrm = lax.rsqrt(jnp.sum(y * y, axis=-1, keepdims=True) + RMS_EPS)
    fac = jnp.where(j < N_HEADS, nrm * HEAD_DIM ** -0.5, jnp.where(j < 2 * N_HEADS, nrm, 1.0))
    o_ref[...] = (y * fac).astype(BF16)


def _conv_a(proj_big, conv_w, l, rows: Rows, br):
    nt = rows.total
    ncol = 3 * N_HEADS
    return pl.pallas_call(
        functools.partial(_conv_a_kernel, rows=rows, br=br),
        grid=(nt // br, ncol),
        in_specs=[
            pl.BlockSpec((br, LANES), lambda b, j: (b, OFF_QKV // LANES + j)),
            pl.BlockSpec((1, CONV_W, LANES), lambda b, j: (l, 0, j)),
        ],
        out_specs=pl.BlockSpec((br, LANES), lambda b, j: (b, j)),
        out_shape=jax.ShapeDtypeStruct((nt, ncol * LANES), BF16),
        compiler_params=_cparams(("arbitrary", "arbitrary")),
        name="conv_a",
    )(proj_big, conv_w)


UNIT = 1024
NCH = UNIT // CHUNK


def _unit_layout(rows: Rows):
    n_ctx_units = rows.ctx_rows // UNIT
    lat_units = rows.lat_len // UNIT
    return n_ctx_units, lat_units, n_ctx_units + rows.n_lat * lat_units


def _unit_rowblock(u, d, rows: Rows):
    n_ctx_units, lat_units, _ = _unit_layout(rows)
    v = jnp.maximum(u - n_ctx_units, 0)
    b = v // lat_units
    j = v % lat_units
    jj = j + d * (lat_units - 1 - 2 * j)
    return jnp.where(u < n_ctx_units, u, n_ctx_units + b * lat_units + jj)


def _unit_lat_seq(u, rows: Rows):
    n_ctx_units, lat_units, _ = _unit_layout(rows)
    return jnp.maximum(u - n_ctx_units, 0) // lat_units


def _dir_masks(fwd):
    ii = lax.broadcasted_iota(jnp.int32, (CHUNK, CHUNK), 0)
    jj = lax.broadcasted_iota(jnp.int32, (CHUNK, CHUNK), 1)
    s = jnp.where(fwd, ii - jj, jj - ii)
    return s >= 0, s > 0, ii == jj


def _delta_kernel(qkv_ref, sm_ref, par_ref, s0_ref, o_ref, sf_ref,
                  s_scr, uv_scr, uk_scr, qk_scr, qd_scr, kd_scr, ge_scr, *, rows: Rows):
    d = pl.program_id(0)
    u = pl.program_id(1)
    n_ctx_units, lat_units, _ = _unit_layout(rows)
    seq_chunks = rows.ctx_len // CHUNK
    seq_per_unit = UNIT // rows.ctx_len
    fwd = d == 0
    is_ctx = u < n_ctx_units
    causal, strict, eye = _dir_masks(fwd)
    causal_bf = jnp.where(causal, 1.0, 0.0).astype(BF16)
    ones_bf = jnp.ones((CHUNK, CHUNK), BF16)
    eye_f = jnp.where(eye, 1.0, 0.0)
    ii = lax.broadcasted_iota(jnp.int32, (CHUNK, CHUNK), 0)
    jj = lax.broadcasted_iota(jnp.int32, (CHUNK, CHUNK), 1)
    pair_masks = [jnp.logical_and((ii >> (s + 1)) == (jj >> (s + 1)), (ii >> s) != (jj >> s))
                  for s in range(CHUNK.bit_length() - 1)]

    def pre(c, carry):
        r0 = pl.multiple_of(c * CHUNK, CHUNK)
        sm = sm_ref[pl.ds(r0, CHUNK), :]
        g_all = -jnp.exp(par_ref[0:1, :]) * _softplus(sm + par_ref[1:2, :])
        b_all = _sigmoid(sm)
        for h in range(N_HEADS):
            q = qkv_ref[pl.ds(r0, CHUNK), h * HEAD_DIM:(h + 1) * HEAD_DIM]
            k = qkv_ref[pl.ds(r0, CHUNK), W_MIX + h * HEAD_DIM:W_MIX + (h + 1) * HEAD_DIM]
            v = qkv_ref[pl.ds(r0, CHUNK), 2 * W_MIX + h * HEAD_DIM:2 * W_MIX + (h + 1) * HEAD_DIM]
            kf = k.astype(F32)
            beta = jnp.where(fwd, b_all[:, SM_BETA + h:SM_BETA + h + 1],
                             b_all[:, SM_BETA + N_HEADS + h:SM_BETA + N_HEADS + h + 1])
            g = jnp.where(fwd, g_all[:, SM_ALPHA + h:SM_ALPHA + h + 1],
                          g_all[:, SM_ALPHA + N_HEADS + h:SM_ALPHA + N_HEADS + h + 1])
            gi = _dot_exact_lhs(causal_bf, jnp.broadcast_to(g, (CHUNK, HEAD_DIM)))
            g64 = gi[:, :CHUNK]
            grow = _dot_exact_lhs(ones_bf, g64 * eye_f)
            decay = jnp.exp(jnp.where(causal, g64 - grow, -jnp.inf))
            kk = _bdot_nt(k, k)
            lmat = jnp.where(strict, beta * kk * decay, 0.0)
            eg = jnp.exp(gi)
            rhs_v = v.astype(F32) * beta
            rhs_k = kf * (beta * eg)
            t = eye_f - jnp.where(pair_masks[0], lmat, 0.0)
            for pm in pair_masks[1:]:
                t = t - _dot3(_dot3(t, jnp.where(pm, lmat, 0.0)), t)
            u_v = _dot3(t, rhs_v)
            u_k = _dot3(t, rhs_k)
            qk = jnp.where(causal, _bdot_nt(q, k) * decay, 0.0)
            g_end = jnp.where(fwd, gi[CHUNK - 1:CHUNK, :], gi[0:1, :])
            uv_scr[c, h] = u_v
            uk_scr[c, h] = u_k.astype(BF16)
            qk_scr[c, h] = qk.astype(BF16)
            qd_scr[c, h] = (q.astype(F32) * eg).astype(BF16)
            kd_scr[c, h] = (kf * jnp.exp(g_end - gi)).astype(BF16)
            ge_scr[c, h] = jnp.broadcast_to(jnp.exp(g_end), (8, HEAD_DIM))
        return carry

    lax.fori_loop(0, NCH, pre, 0)

    first_lat = jnp.logical_and(u >= n_ctx_units, (u - n_ctx_units) % lat_units == 0)

    @pl.when(first_lat)
    def _():
        s_scr[...] = s0_ref[0, 0, 0]

    @pl.when(u == n_ctx_units)
    def _():
        sf_ref[...] = jnp.zeros(sf_ref.shape, F32)

    def step(n, carry):
        c = jnp.where(fwd, n, NCH - 1 - n)
        r0 = pl.multiple_of(c * CHUNK, CHUNK)

        @pl.when(jnp.logical_and(is_ctx, n % seq_chunks == 0))
        def _():
            s_scr[...] = jnp.zeros(s_scr.shape, F32)

        for h in range(N_HEADS):
            s = s_scr[h]
            sb = s.astype(BF16)
            uu = uv_scr[c, h] - jnp.dot(uk_scr[c, h], sb, preferred_element_type=F32)
            ub = uu.astype(BF16)
            o = (jnp.dot(qd_scr[c, h], sb, preferred_element_type=F32)
                 + jnp.dot(qk_scr[c, h], ub, preferred_element_type=F32))
            s_new = s * ge_scr[c, h][0:1, :] + lax.dot_general(
                kd_scr[c, h], ub, (((0,), (0,)), ((), ())), preferred_element_type=F32)
            s_scr[h] = s_new
            o_ref[0, pl.ds(r0, CHUNK), h * HEAD_DIM:(h + 1) * HEAD_DIM] = o

        @pl.when(jnp.logical_and(is_ctx, n % seq_chunks == seq_chunks - 1))
        def _():
            sf_ref[c // seq_chunks, 0] = s_scr[...]

        return carry

    lax.fori_loop(0, NCH, step, 0)


def _delta(qkv_c, proj_f32, par, state, l, rows: Rows):
    n_ctx_units, lat_units, n_units = _unit_layout(rows)
    seq_per_unit = UNIT // rows.ctx_len
    nt = rows.total
    sm_col = W_B // LANES
    kern = functools.partial(_delta_kernel, rows=rows)
    hshape = (NCH, N_HEADS, CHUNK, HEAD_DIM)
    return pl.pallas_call(
        kern,
        grid=(2, n_units),
        in_specs=[
            pl.BlockSpec((UNIT, 3 * W_MIX), lambda d, u: (_unit_rowblock(u, d, rows), 0)),
            pl.BlockSpec((UNIT, LANES), lambda d, u: (_unit_rowblock(u, d, rows), sm_col)),
            pl.BlockSpec((8, LANES), lambda d, u: (0, 0)),
            pl.BlockSpec((1, 1, 1, N_HEADS, HEAD_DIM, HEAD_DIM),
                         lambda d, u: (_unit_lat_seq(u, rows), l, d, 0, 0, 0)),
        ],
        out_specs=[
            pl.BlockSpec((1, UNIT, W_MIX), lambda d, u: (d, _unit_rowblock(u, d, rows), 0)),
            pl.BlockSpec((seq_per_unit, 1, N_HEADS, HEAD_DIM, HEAD_DIM),
                         lambda d, u: (jnp.minimum(u, n_ctx_units), d, 0, 0, 0)),
        ],
        out_shape=[
            jax.ShapeDtypeStruct((2, nt, W_MIX), F32),
            jax.ShapeDtypeStruct((rows.n_ctx + seq_per_unit, 2, N_HEADS, HEAD_DIM, HEAD_DIM), F32),
        ],
        scratch_shapes=[
            pltpu.VMEM((N_HEADS, HEAD_DIM, HEAD_DIM), F32),
            pltpu.VMEM(hshape, F32),
            pltpu.VMEM(hshape, BF16),
            pltpu.VMEM((NCH, N_HEADS, CHUNK, CHUNK), BF16),
            pltpu.VMEM(hshape, BF16),
            pltpu.VMEM(hshape, BF16),
            pltpu.VMEM((NCH, N_HEADS, 8, HEAD_DIM), F32),
        ],
        compiler_params=_cparams(("arbitrary", "arbitrary")),
        name="delta",
    )(qkv_c, proj_f32, par, state)


def _mlstm_kernel(big_ref, sm_ref, par_ref, c0_ref, n0_ref, m0_ref, o_ref, cf_ref, nf_ref, mf_ref,
                  c_scr, n_scr, m_scr, dm_scr, qk_scr, fi_scr, mi_scr, de_scr, me_scr, *, rows: Rows):
    d = pl.program_id(0)
    u = pl.program_id(1)
    n_ctx_units, lat_units, _ = _unit_layout(rows)
    seq_chunks = rows.ctx_len // CHUNK
    fwd = d == 0
    is_ctx = u < n_ctx_units
    causal, _, eye = _dir_masks(fwd)
    causal_bf = jnp.where(causal, 1.0, 0.0).astype(BF16)
    ones_bf = jnp.ones((CHUNK, CHUNK), BF16)
    eye_f = jnp.where(eye, 1.0, 0.0)
    qo, ko, vo = 0, W_MIX, 2 * W_MIX

    def pre(c, carry):
        r0 = pl.multiple_of(c * CHUNK, CHUNK)
        sm = sm_ref[pl.ds(r0, CHUNK), :]
        i_all = sm + par_ref[2:3, :]
        f_all = -_softplus(-(sm + par_ref[3:4, :]))
        for h in range(N_HEADS):
            q = big_ref[pl.ds(r0, CHUNK), qo + h * HEAD_DIM:qo + (h + 1) * HEAD_DIM]
            k = big_ref[pl.ds(r0, CHUNK), ko + h * HEAD_DIM:ko + (h + 1) * HEAD_DIM]
            ks = k.astype(F32) * HEAD_DIM ** -0.5
            it = jnp.where(fwd, i_all[:, SM_I + h:SM_I + h + 1], i_all[:, SM_I + N_HEADS + h:SM_I + N_HEADS + h + 1])
            lf = jnp.where(fwd, f_all[:, SM_F + h:SM_F + h + 1], f_all[:, SM_F + N_HEADS + h:SM_F + N_HEADS + h + 1])
            fi = _dot_exact_lhs(causal_bf, jnp.broadcast_to(lf, (CHUNK, HEAD_DIM)))
            f64 = fi[:, :CHUNK]
            hrow = _dot_exact_lhs(ones_bf, (f64 - it) * eye_f)
            dmat = jnp.where(causal, f64 - hrow, -jnp.inf)
            m_intra = jnp.max(dmat, axis=-1, keepdims=True)
            f_end = jnp.where(fwd, fi[CHUNK - 1:CHUNK, :], fi[0:1, :])
            d_end = f_end - fi + it
            m_end = jnp.max(d_end, axis=0, keepdims=True)
            dm_scr[c, h] = dmat
            qk_scr[c, h] = _bdot_nt(q, ks)
            fi_scr[c, h] = fi
            mi_scr[c, h] = jnp.broadcast_to(m_intra, (CHUNK, HEAD_DIM))
            de_scr[c, h] = d_end
            me_scr[c, h] = jnp.broadcast_to(m_end, (8, HEAD_DIM))
        return carry

    lax.fori_loop(0, NCH, pre, 0)

    first_lat = jnp.logical_and(u >= n_ctx_units, (u - n_ctx_units) % lat_units == 0)

    @pl.when(first_lat)
    def _():
        c_scr[...] = c0_ref[0, 0, 0]
        for h in range(N_HEADS):
            n_scr[h] = jnp.broadcast_to(n0_ref[0, 0, 0, h:h + 1, :], (8, HEAD_DIM))
            m_scr[h] = jnp.broadcast_to(m0_ref[0, 0, 0, h:h + 1, :], (8, HEAD_DIM))

    @pl.when(u == n_ctx_units)
    def _():
        cf_ref[...] = jnp.zeros(cf_ref.shape, F32)
        nf_ref[...] = jnp.zeros(nf_ref.shape, F32)
        mf_ref[...] = jnp.zeros(mf_ref.shape, F32)

    def step(n, carry):
        c = jnp.where(fwd, n, NCH - 1 - n)
        r0 = pl.multiple_of(c * CHUNK, CHUNK)

        @pl.when(jnp.logical_and(is_ctx, n % seq_chunks == 0))
        def _():
            c_scr[...] = jnp.zeros(c_scr.shape, F32)
            n_scr[...] = jnp.zeros(n_scr.shape, F32)
            m_scr[...] = jnp.zeros(m_scr.shape, F32)

        for h in range(N_HEADS):
            q = big_ref[pl.ds(r0, CHUNK), qo + h * HEAD_DIM:qo + (h + 1) * HEAD_DIM]
            k = big_ref[pl.ds(r0, CHUNK), ko + h * HEAD_DIM:ko + (h + 1) * HEAD_DIM]
            v = big_ref[pl.ds(r0, CHUNK), vo + h * HEAD_DIM:vo + (h + 1) * HEAD_DIM]
            ks = k.astype(F32) * HEAD_DIM ** -0.5
            cs = c_scr[h]
            ns = n_scr[h][0:1, :]
            ms = m_scr[h][0:1, :]
            fi = fi_scr[c, h]
            inter = fi + ms
            mt = jnp.maximum(inter, mi_scr[c, h])
            w_int = jnp.exp(inter - mt)
            pm = jnp.exp(dm_scr[c, h] - mt[:, :CHUNK]) * qk_scr[c, h]
            num = w_int * jnp.dot(q, cs.astype(BF16), preferred_element_type=F32) + _bdot(pm, v)
            qn = jnp.sum(q.astype(F32) * ns, axis=-1, keepdims=True)
            den = w_int[:, 0:1] * qn + jnp.sum(pm, axis=-1, keepdims=True)
            hh = num / jnp.maximum(jnp.abs(den), jnp.exp(-mt[:, 0:1]))
            f_end = jnp.where(fwd, fi[CHUNK - 1:CHUNK, :], fi[0:1, :])
            inter_end = f_end + ms
            m_new = jnp.maximum(inter_end, me_scr[c, h][0:1, :])
            w_tok = jnp.exp(de_scr[c, h] - m_new)
            s_int = jnp.exp(inter_end - m_new)
            kw = ks * w_tok
            c_new = cs * s_int[:, 0:1] + _bdot_tn(kw, v)
            n_new = ns * s_int + jnp.sum(kw, axis=0, keepdims=True)
            c_scr[h] = c_new
            n_scr[h] = jnp.broadcast_to(n_new, (8, HEAD_DIM))
            m_scr[h] = jnp.broadcast_to(m_new, (8, HEAD_DIM))
            o_ref[0, pl.ds(r0, CHUNK), h * HEAD_DIM:(h + 1) * HEAD_DIM] = hh

        @pl.when(jnp.logical_and(is_ctx, n % seq_chunks == seq_chunks - 1))
        def _():
            sq = c // seq_chunks
            cf_ref[sq, 0] = c_scr[...]
            nf_ref[sq, 0] = n_scr[...]
            mf_ref[sq, 0] = m_scr[...]

        return carry

    lax.fori_loop(0, NCH, step, 0)


def _mlstm(proj_big, proj_f32, par, st_c, st_n8, st_m8, l, rows: Rows):
    n_ctx_units, lat_units, n_units = _unit_layout(rows)
    seq_per_unit = UNIT // rows.ctx_len
    nt = rows.total
    sm_col = W_B // LANES
    qkv_blk = OFF_QC // (3 * W_MIX)
    assert OFF_QC % (3 * W_MIX) == 0
    kern = functools.partial(_mlstm_kernel, rows=rows)
    hshape = (NCH, N_HEADS, CHUNK, HEAD_DIM)
    n_out = rows.n_ctx + seq_per_unit
    return pl.pallas_call(
        kern,
        grid=(2, n_units),
        in_specs=[
            pl.BlockSpec((UNIT, 3 * W_MIX), lambda d, u: (_unit_rowblock(u, d, rows), qkv_blk)),
            pl.BlockSpec((UNIT, LANES), lambda d, u: (_unit_rowblock(u, d, rows), sm_col)),
            pl.BlockSpec((8, LANES), lambda d, u: (0, 0)),
            pl.BlockSpec((1, 1, 1, N_HEADS, HEAD_DIM, HEAD_DIM),
                         lambda d, u: (_unit_lat_seq(u, rows), l, d, 0, 0, 0)),
            pl.BlockSpec((1, 1, 1, N_HEADS, HEAD_DIM), lambda d, u: (_unit_lat_seq(u, rows), l, d, 0, 0)),
            pl.BlockSpec((1, 1, 1, N_HEADS, HEAD_DIM), lambda d, u: (_unit_lat_seq(u, rows), l, d, 0, 0)),
        ],
        out_specs=[
            pl.BlockSpec((1, UNIT, W_MIX), lambda d, u: (d, _unit_rowblock(u, d, rows), 0)),
            pl.BlockSpec((seq_per_unit, 1, N_HEADS, HEAD_DIM, HEAD_DIM),
                         lambda d, u: (jnp.minimum(u, n_ctx_units), d, 0, 0, 0)),
            pl.BlockSpec((seq_per_unit, 1, N_HEADS, 8, HEAD_DIM),
                         lambda d, u: (jnp.minimum(u, n_ctx_units), d, 0, 0, 0)),
            pl.BlockSpec((seq_per_unit, 1, N_HEADS, 8, HEAD_DIM),
                         lambda d, u: (jnp.minimum(u, n_ctx_units), d, 0, 0, 0)),
        ],
        out_shape=[
            jax.ShapeDtypeStruct((2, nt, W_MIX), F32),
            jax.ShapeDtypeStruct((n_out, 2, N_HEADS, HEAD_DIM, HEAD_DIM), F32),
            jax.ShapeDtypeStruct((n_out, 2, N_HEADS, 8, HEAD_DIM), F32),
            jax.ShapeDtypeStruct((n_out, 2, N_HEADS, 8, HEAD_DIM), F32),
        ],
        scratch_shapes=[
            pltpu.VMEM((N_HEADS, HEAD_DIM, HEAD_DIM), F32),
            pltpu.VMEM((N_HEADS, 8, HEAD_DIM), F32),
            pltpu.VMEM((N_HEADS, 8, HEAD_DIM), F32),
            pltpu.VMEM((NCH, N_HEADS, CHUNK, CHUNK), F32),
            pltpu.VMEM((NCH, N_HEADS, CHUNK, CHUNK), F32),
            pltpu.VMEM(hshape, F32),
            pltpu.VMEM(hshape, F32),
            pltpu.VMEM(hshape, F32),
            pltpu.VMEM((NCH, N_HEADS, 8, HEAD_DIM), F32),
        ],
        compiler_params=_cparams(("arbitrary", "arbitrary")),
        name="mlstm",
    )(proj_big, proj_f32, par, st_c, st_n8, st_m8)


def _gelu_tanh(x):
    return 0.5 * x * (1.0 + jnp.tanh(math.sqrt(2.0 / math.pi) * (x + 0.044715 * (x * x * x))))


def _tile_scan(a, b, reverse):
    n = a.shape[0]
    row = lax.broadcasted_iota(jnp.int32, a.shape, 0)
    s = 1
    while s < n:
        if reverse:
            ok = row < n - s
            a_sh = jnp.where(ok, pltpu.roll(a, n - s, 0), 1.0)
            b_sh = jnp.where(ok, pltpu.roll(b, n - s, 0), 0.0)
        else:
            ok = row >= s
            a_sh = jnp.where(ok, pltpu.roll(a, s, 0), 1.0)
            b_sh = jnp.where(ok, pltpu.roll(b, s, 0), 0.0)
        b = a * b_sh + b
        a = a * a_sh
        s *= 2
    return a, b


def _lru_kernel(x_ref, g_ref, cw_ref, cb_ref, wa_ref, ba_ref, wx_ref, bx_ref, lam_ref, h0_ref,
                y_ref, hf_ref, xc_scr, hfw_scr, *, rows: Rows, br):
    blk = pl.program_id(0)
    is_ctx = blk * br < rows.ctx_rows
    ts = rows.ctx_len
    n_tiles = br // ts
    seq_len = jnp.where(is_ctx, rows.ctx_len, rows.lat_len)
    xc_scr[...] = _conv_taps(x_ref[...], cw_ref[0], seq_len) + cb_ref[0]

    for dd in range(2):
        reverse = dd == 1
        sp_lam = _softplus(-lam_ref[0, dd])

        def tile(n, carry, dd=dd, reverse=reverse, sp_lam=sp_lam):
            t = (n_tiles - 1 - n) if reverse else n
            r0 = pl.multiple_of(t * ts, ts)
            xc = xc_scr[pl.ds(r0, ts), :]
            r = _sigmoid(_bdot(xc, wa_ref[0, dd, 0]) + ba_ref[0, dd])
            gi = _sigmoid(_bdot(xc, wx_ref[0, dd, 0]) + bx_ref[0, dd])
            log_a = -LRU_C * r * sp_lam
            a = jnp.exp(log_a)
            b = jnp.sqrt(-jnp.tanh(log_a) * (a * a + 1.0)) * (gi * xc)
            a_cum, h = _tile_scan(a, b, reverse)
            carry = jnp.where(is_ctx, 0.0, carry)
            h = h + a_cum * carry
            last = h[0:1, :] if reverse else h[ts - 1:ts, :]
            hf_ref[0, dd, pl.ds(t, 1), :] = last
            if reverse:
                g = g_ref[pl.ds(r0, ts), :].astype(F32)
                y_ref[pl.ds(r0, ts), :] = ((hfw_scr[pl.ds(r0, ts), :] + h) * _gelu_tanh(g)).astype(BF16)
            else:
                hfw_scr[pl.ds(r0, ts), :] = h
            return last

        lax.fori_loop(0, n_tiles, tile, h0_ref[0, 0, dd])


def _lru(proj_f32, proj_big, conv_w, conv_b, wa, ba, wx, bx, lam, h0, l, rows: Rows):
    br = rows.lat_len
    nt = rows.total
    n_blocks = nt // br
    n_ctx_blocks = rows.ctx_rows // br
    n_tiles = br // rows.ctx_len
    gcol = OFF_GB // BW_B

    def lat_seq(b):
        return jnp.maximum(b - n_ctx_blocks, 0)

    vec = lambda a: a.reshape(DEPTH, 2, 1, W_B)
    return pl.pallas_call(
        functools.partial(_lru_kernel, rows=rows, br=br),
        grid=(n_blocks, NB_B),
        in_specs=[
            pl.BlockSpec((br, BW_B), lambda b, j: (b, j)),
            pl.BlockSpec((br, BW_B), lambda b, j: (b, gcol + j)),
            pl.BlockSpec((1, CONV_W, BW_B), lambda b, j: (l, 0, j)),
            pl.BlockSpec((1, 1, BW_B), lambda b, j: (l, 0, j)),
            pl.BlockSpec((1, 2, 1, BW_B, BW_B), lambda b, j: (l, 0, j, 0, 0)),
            pl.BlockSpec((1, 2, 1, BW_B), lambda b, j: (l, 0, 0, j)),
            pl.BlockSpec((1, 2, 1, BW_B, BW_B), lambda b, j: (l, 0, j, 0, 0)),
            pl.BlockSpec((1, 2, 1, BW_B), lambda b, j: (l, 0, 0, j)),
            pl.BlockSpec((1, 2, 1, BW_B), lambda b, j: (l, 0, 0, j)),
            pl.BlockSpec((1, 1, 2, 1, BW_B), lambda b, j: (lat_seq(b), l, 0, 0, j)),
        ],
        out_specs=[
            pl.BlockSpec((br, BW_B), lambda b, j: (b, j)),
            pl.BlockSpec((1, 2, n_tiles, BW_B), lambda b, j: (b, 0, 0, j)),
        ],
        out_shape=[
            jax.ShapeDtypeStruct((nt, W_B), BF16),
            jax.ShapeDtypeStruct((n_blocks, 2, n_tiles, W_B), F32),
        ],
        scratch_shapes=[pltpu.VMEM((br, BW_B), F32), pltpu.VMEM((br, BW_B), F32)],
        compiler_params=_cparams(("arbitrary", "arbitrary")),
        name="lru",
    )(proj_f32, proj_big, conv_w, conv_b.reshape(DEPTH, 1, W_B), wa, vec(ba), wx, vec(bx), vec(lam),
      h0.reshape(h0.shape[0], DEPTH, 2, 1, W_B))


def _route(lg, le):
    lane = lax.broadcasted_iota(jnp.int32, lg.shape, 1)
    neg = -jnp.inf
    lgm = jnp.where(lane < N_GROUPS, lg, neg)
    gmax = jnp.max(lgm, axis=-1, keepdims=True)
    p_grp = 1.0 / jnp.sum(jnp.exp(lgm - gmax), axis=-1, keepdims=True)
    g_sel = jnp.min(jnp.where(lgm == gmax, lane, LANES), axis=-1, keepdims=True)
    in_grp = jnp.logical_and(lane >= g_sel * E_PER_GROUP, lane < (g_sel + 1) * E_PER_GROUP)
    lem = jnp.where(in_grp, le, neg)
    v1 = jnp.max(lem, axis=-1, keepdims=True)
    i1 = jnp.min(jnp.where(lem == v1, lane, LANES), axis=-1, keepdims=True)
    lem2 = jnp.where(lane == i1, neg, lem)
    v2 = jnp.max(lem2, axis=-1, keepdims=True)
    i2 = jnp.min(jnp.where(lem2 == v2, lane, LANES), axis=-1, keepdims=True)
    e2 = jnp.exp(v2 - v1)
    w1 = p_grp / (1.0 + e2)
    w2 = p_grp * e2 / (1.0 + e2)
    return jnp.where(lane == i1, w1, 0.0) + jnp.where(lane == i2, w2, 0.0)


def _merge_kernel(x_ref, mod_ref, oa_ref, z_ref, yb_ref, hc_ref, oc_ref, gt_ref, dn_ref, mn_ref,
                  wpa_ref, wpb_ref, wpc_ref, wout_ref, g1_ref, b1_ref, wr_ref, br_ref,
                  x1_ref, h2_ref, gate_ref):
    oa = oa_ref[0] + oa_ref[1]
    hc = hc_ref[0] + hc_ref[1]
    ya, yc = [], []
    for h in range(N_HEADS):
        sl = slice(h * HEAD_DIM, (h + 1) * HEAD_DIM)
        o_h = oa[:, sl]
        o_h = o_h * lax.rsqrt(jnp.mean(o_h * o_h, axis=-1, keepdims=True) + RMS_EPS) * dn_ref[...]
        ya.append((o_h * _silu(z_ref[:, sl].astype(F32))).astype(BF16))
        c_h = _ln(hc[:, sl]) * mn_ref[:, sl]
        yc.append((_sigmoid(oc_ref[:, sl].astype(F32)) * c_h).astype(BF16))
    ya = jnp.concatenate(ya, axis=-1)
    yc = jnp.concatenate(yc, axis=-1)
    ga = _sigmoid(gt_ref[:, 0:D_MODEL].astype(F32))
    gb = _sigmoid(gt_ref[:, D_MODEL:2 * D_MODEL].astype(F32))
    gc = _sigmoid(gt_ref[:, 2 * D_MODEL:3 * D_MODEL].astype(F32))
    d = functools.partial(jnp.dot, preferred_element_type=F32)
    merged = ga * d(ya, wpa_ref[...]) + gb * d(yb_ref[...], wpb_ref[...]) + gc * d(yc, wpc_ref[...])
    mixed = d(merged.astype(BF16), wout_ref[...])
    gate1 = mod_ref[0, 2:3, :]
    shift2 = mod_ref[0, 3:4, :]
    scale2 = mod_ref[0, 4:5, :]
    x1 = _ln(DN_ALPHA * x_ref[...] + gate1 * mixed) * g1_ref[...] + b1_ref[...]
    x1_ref[...] = x1
    h2 = _ln(x1) * (1.0 + scale2) + shift2
    h2_ref[...] = h2.astype(BF16)
    logits = _dot3(h2, wr_ref[...]) + br_ref[...]
    lg = logits
    le = pltpu.roll(logits, LANES - 64, 1)
    gate_ref[...] = _route(lg, le)


def _merge(x, mod_l, o_a, proj_big, y_b, h_c, dn, mn, wpa, wpb, wpc, wout, g1, b1, wr, br, rows: Rows, tm):
    nt = rows.total
    row = lambda i: (i, 0)
    const = lambda i: (0, 0)
    cw = W_MIX
    return pl.pallas_call(
        _merge_kernel,
        grid=(nt // tm,),
        in_specs=[
            pl.BlockSpec((tm, D_MODEL), row),
            pl.BlockSpec((1, 6, D_MODEL), lambda i: (_cond_index(i * tm, rows), 0, 0)),
            pl.BlockSpec((2, tm, cw), lambda i: (0, i, 0)),
            pl.BlockSpec((tm, cw), lambda i: (i, OFF_Z // cw)),
            pl.BlockSpec((tm, cw), row),
            pl.BlockSpec((2, tm, cw), lambda i: (0, i, 0)),
            pl.BlockSpec((tm, cw), lambda i: (i, OFF_OC // cw)),
            pl.BlockSpec((tm, 3 * D_MODEL), lambda i: (i, OFF_GATES // (3 * D_MODEL))),
            pl.BlockSpec((1, HEAD_DIM), const),
            pl.BlockSpec((1, cw), const),
            pl.BlockSpec((cw, D_MODEL), const),
            pl.BlockSpec((cw, D_MODEL), const),
            pl.BlockSpec((cw, D_MODEL), const),
            pl.BlockSpec((D_MODEL, D_MODEL), const),
            pl.BlockSpec((1, D_MODEL), const),
            pl.BlockSpec((1, D_MODEL), const),
            pl.BlockSpec((D_MODEL, LANES), const),
            pl.BlockSpec((1, LANES), const),
        ],
        out_specs=[
            pl.BlockSpec((tm, D_MODEL), row),
            pl.BlockSpec((tm, D_MODEL), row),
            pl.BlockSpec((tm, LANES), row),
        ],
        out_shape=[
            jax.ShapeDtypeStruct((nt, D_MODEL), F32),
            jax.ShapeDtypeStruct((nt, D_MODEL), BF16),
            jax.ShapeDtypeStruct((nt, LANES), F32),
        ],
        compiler_params=_cparams(("arbitrary",)),
        name="merge",
    )(x, mod_l, o_a, proj_big, y_b, h_c, proj_big, proj_big, dn, mn, wpa, wpb, wpc, wout, g1, b1, wr, br)


def _moe_kernel(h_ref, gate_ref, w1_ref, w3_ref, w2_ref, x1_ref, mod_ref, g2_ref, b2_ref, o_ref, acc_scr):
    e = pl.program_id(1)

    @pl.when(e == 0)
    def _():
        acc_scr[...] = jnp.zeros(acc_scr.shape, F32)

    h = h_ref[...]
    lane = lax.broadcasted_iota(jnp.int32, gate_ref.shape, 1)
    gcol = jnp.sum(jnp.where(lane == e, gate_ref[...], 0.0), axis=-1, keepdims=True)
    a = jnp.dot(h, w1_ref[0].astype(BF16), preferred_element_type=F32)
    b = jnp.dot(h, w3_ref[0].astype(BF16), preferred_element_type=F32)
    hid = _silu(a) * b * gcol
    acc_scr[...] += jnp.dot(hid.astype(BF16), w2_ref[0].astype(BF16), preferred_element_type=F32)

    @pl.when(e == N_EXPERTS - 1)
    def _():
        gate2 = mod_ref[0, 5:6, :]
        o_ref[...] = _ln(DN_ALPHA * x1_ref[...] + gate2 * acc_scr[...]) * g2_ref[...] + b2_ref[...]


def _moe(h2, gate, w1, w3, w2, x1, mod_l, g2, b2, l, rows: Rows, tm):
    nt = rows.total
    row = lambda i, e: (i, 0)
    const = lambda i, e: (0, 0)
    return pl.pallas_call(
        _moe_kernel,
        grid=(nt // tm, N_EXPERTS),
        in_specs=[
            pl.BlockSpec((tm, D_MODEL), row),
            pl.BlockSpec((tm, LANES), row),
            pl.BlockSpec((1, D_MODEL, D_EXPERT), lambda i, e: (l * N_EXPERTS + e, 0, 0)),
            pl.BlockSpec((1, D_MODEL, D_EXPERT), lambda i, e: (l * N_EXPERTS + e, 0, 0)),
            pl.BlockSpec((1, D_EXPERT, D_MODEL), lambda i, e: (l * N_EXPERTS + e, 0, 0)),
            pl.BlockSpec((tm, D_MODEL), row),
            pl.BlockSpec((1, 6, D_MODEL), lambda i, e: (_cond_index(i * tm, rows), 0, 0)),
            pl.BlockSpec((1, D_MODEL), const),
            pl.BlockSpec((1, D_MODEL), const),
        ],
        out_specs=pl.BlockSpec((tm, D_MODEL), row),
        out_shape=jax.ShapeDtypeStruct((nt, D_MODEL), F32),
        scratch_shapes=[pltpu.VMEM((tm, D_MODEL), F32)],
        compiler_params=_cparams(("arbitrary", "arbitrary")),
        name="moe",
    )(h2, gate, w1, w3, w2, x1, mod_l, g2, b2)


def _grid_pos_embed(n_tokens):
    rows = n_tokens // GRID_W
    r, col = jnp.meshgrid(jnp.arange(rows, dtype=F32), jnp.arange(GRID_W, dtype=F32), indexing='ij')
    quarter = D_MODEL // 4
    freqs = jnp.exp(-math.log(POS_BASE) * jnp.arange(quarter, dtype=F32) / quarter)
    ar = r.reshape(-1, 1) * freqs
    ac = col.reshape(-1, 1) * freqs
    return jnp.concatenate([jnp.sin(ar), jnp.cos(ar), jnp.sin(ac), jnp.cos(ac)], axis=-1)


def _pack_w_in(w_in_l):
    sizes = (3 * W_MIX, W_B, W_MIX, 8, 8, W_B, W_MIX, W_MIX, W_MIX, W_MIX, 8, 8, 3 * D_MODEL)
    parts, start = [], 0
    for s in sizes:
        parts.append(w_in_l[:, start:start + s])
        start += s
    qkv_a, x_b, z_a, beta, alpha, g_b, q_c, k_c, v_c, o_c, i_c, f_c, gates = parts
    big = jnp.concatenate([gates, qkv_a, q_c, k_c, v_c, z_a, g_b, o_c], axis=1).astype(BF16)
    pad = jnp.zeros((D_MODEL, LANES - 32), F32)
    small = jnp.concatenate([x_b, beta, alpha, i_c, f_c, pad], axis=1).astype(BF16)
    return big, small


def _lane_row(vals, off):
    out = jnp.zeros((8, LANES), F32)
    for r, (v, o) in enumerate(zip(vals, off)):
        out = out.at[r, o:o + 8].set(v.reshape(-1))
    return out


def kernel(x_prompt, x_sample, state_delta, state_lru, state_mlstm_C, state_mlstm_n, state_mlstm_m, c, c_ctx,
           w_mod, b_mod, w_in, conv_a, delta_a_log, delta_dt_bias, delta_norm, conv_b_w, conv_b_b,
           lru_wa, lru_ba, lru_wx, lru_bx, lru_lambda, mlstm_bi, mlstm_bf, mlstm_norm,
           w_pa, w_pb, w_pc, w_out, ln1_g, ln1_b, ln2_g, ln2_b, w_rg, b_rg, w_re, b_re, w_e1, w_e3, w_e2):
    n_ctx, ctx_len, _ = x_prompt.shape
    n_lat, lat_len, _ = x_sample.shape
    rows = Rows(n_ctx, ctx_len, n_lat, lat_len)
    assert rows.ctx_rows % UNIT == 0 and lat_len % UNIT == 0 and UNIT % ctx_len == 0
    assert rows.ctx_rows % lat_len == 0 and ctx_len % CHUNK == 0 and n_lat <= 7

    tm = min(1024, lat_len)
    pos = _grid_pos_embed(lat_len)
    x = _prep(x_prompt.reshape(rows.ctx_rows, D_MODEL), x_sample.reshape(n_lat * lat_len, D_MODEL), pos, rows, tm)

    cond8 = jnp.zeros((8, D_MODEL), F32).at[0].set(c_ctx).at[1:1 + n_lat].set(c)
    mod = _modulation(cond8, w_mod, b_mod).reshape(DEPTH, 8, 6, D_MODEL)

    m_bcast = jnp.broadcast_to(state_mlstm_m[..., None], state_mlstm_m.shape + (HEAD_DIM,))
    w_e1f = w_e1.reshape(DEPTH * N_EXPERTS, D_MODEL, D_EXPERT)
    w_e3f = w_e3.reshape(DEPTH * N_EXPERTS, D_MODEL, D_EXPERT)
    w_e2f = w_e2.reshape(DEPTH * N_EXPERTS, D_EXPERT, D_MODEL)

    finals = []
    for l in range(DEPTH):
        w_big, w_small = _pack_w_in(w_in[l])
        par = _lane_row([delta_a_log[l], delta_dt_bias[l], mlstm_bi[l], mlstm_bf[l]],
                        [SM_ALPHA, SM_ALPHA, SM_I, SM_F])
        proj_big, proj_f32 = _projection(x, mod[l], w_big, w_small, rows, tm, 1280)
        qkv_c = _conv_a(proj_big, conv_a, l, rows, lat_len)
        o_a, sf_a = _delta(qkv_c, proj_f32, par, state_delta, l, rows)
        y_b, hf_b = _lru(proj_f32, proj_big, conv_b_w, conv_b_b, lru_wa, lru_ba, lru_wx, lru_bx, lru_lambda,
                         state_lru, l, rows)
        h_c, cf, nf, mf = _mlstm(proj_big, proj_f32, par, state_mlstm_C, state_mlstm_n, m_bcast, l, rows)
        wr = jnp.zeros((D_MODEL, LANES), F32).at[:, :N_GROUPS].set(w_rg[l]).at[:, 64:64 + N_EXPERTS].set(w_re[l])
        br = jnp.zeros((1, LANES), F32).at[0, :N_GROUPS].set(b_rg[l]).at[0, 64:64 + N_EXPERTS].set(b_re[l])
        x1, h2, gate = _merge(x, mod[l], o_a, proj_big, y_b, h_c,
                              delta_norm[l].reshape(1, HEAD_DIM), mlstm_norm[l].reshape(1, W_MIX),
                              w_pa[l].astype(BF16), w_pb[l].astype(BF16), w_pc[l].astype(BF16),
                              w_out[l].astype(BF16), ln1_g[l].reshape(1, D_MODEL), ln1_b[l].reshape(1, D_MODEL),
                              wr, br, rows, 256)
        x = _moe(h2, gate, w_e1f, w_e3f, w_e2f, x1, mod[l], ln2_g[l].reshape(1, D_MODEL),
                 ln2_b[l].reshape(1, D_MODEL), l, rows, tm)
        n_ctx_blocks = rows.ctx_rows // lat_len
        lru_fin = jnp.swapaxes(hf_b[:n_ctx_blocks], 1, 2).reshape(n_ctx, 2, W_B)
        finals.append((sf_a[:n_ctx], lru_fin, cf[:n_ctx], nf[:n_ctx, :, :, 0, :], mf[:n_ctx, :, :, 0, 0]))

    new_delta, new_lru, new_mc, new_mn, new_mm = (jnp.stack([f[i] for f in finals], axis=1) for i in range(5))
    y_prompt = x[:rows.ctx_rows].reshape(n_ctx, ctx_len, D_MODEL)
    y_sample = x[rows.ctx_rows:].reshape(n_lat, lat_len, D_MODEL)
    return (y_prompt, y_sample, new_delta, new_lru, new_mc, new_mn, new_mm)
```

```python
import functools
import math
from typing import NamedTuple

import jax
import jax.numpy as jnp
from jax import lax
from jax.experimental import pallas as pl
from jax.experimental.pallas import tpu as pltpu

F32 = jnp.float32
BF16 = jnp.bfloat16

D_MODEL = 1024
DEPTH = 2
GRID_W = 64
POS_BASE = 10000.0
CONV_W = 4
LN_EPS = 1e-5
RMS_EPS = 1e-6
N_HEADS = 4
HEAD_DIM = 128
W_MIX = N_HEADS * HEAD_DIM
CHUNK = 64
W_B = 512
NB_B = 4
BW_B = W_B // NB_B
LRU_C = 8.0
N_GROUPS = 4
E_PER_GROUP = 8
N_EXPERTS = N_GROUPS * E_PER_GROUP
D_EXPERT = 256
DN_ALPHA = (2 * DEPTH) ** 0.25

LANES = 128
VMEM_LIMIT = 56 * 1024 * 1024

OFF_GATES, OFF_QKV, OFF_QC, OFF_KC, OFF_VC, OFF_Z, OFF_GB, OFF_OC = 0, 3072, 4608, 5120, 5632, 6144, 6656, 7168
N_BIG = OFF_OC + W_MIX
N_F32 = W_B + LANES
SM_BETA, SM_ALPHA, SM_I, SM_F = 0, 8, 16, 24


class Rows(NamedTuple):
    n_ctx: int
    ctx_len: int
    n_lat: int
    lat_len: int

    @property
    def ctx_rows(self):
        return self.n_ctx * self.ctx_len

    @property
    def total(self):
        return self.ctx_rows + self.n_lat * self.lat_len


def _cparams(sem):
    return pltpu.CompilerParams(dimension_semantics=sem, vmem_limit_bytes=VMEM_LIMIT)


def _sigmoid(x):
    return 1.0 / (1.0 + jnp.exp(-x))


def _silu(x):
    return x * _sigmoid(x)


def _softplus(x):
    return jnp.maximum(x, 0.0) + jnp.log1p(jnp.exp(-jnp.abs(x)))


def _ln(x):
    mu = jnp.mean(x, axis=-1, keepdims=True)
    xc = x - mu
    var = jnp.mean(xc * xc, axis=-1, keepdims=True)
    return xc * lax.rsqrt(var + LN_EPS)


def _bdot(a, b):
    return jnp.dot(a.astype(BF16), b.astype(BF16), preferred_element_type=F32)


def _bdot_nt(a, b):
    return lax.dot_general(a.astype(BF16), b.astype(BF16), (((1,), (1,)), ((), ())), preferred_element_type=F32)


def _bdot_tn(a, b):
    return lax.dot_general(a.astype(BF16), b.astype(BF16), (((0,), (0,)), ((), ())), preferred_element_type=F32)


def _split2(a):
    hi = a.astype(BF16)
    lo = (a - hi.astype(F32)).astype(BF16)
    return hi, lo


def _split3(a):
    hi = a.astype(BF16)
    r = a - hi.astype(F32)
    mid = r.astype(BF16)
    lo = (r - mid.astype(F32)).astype(BF16)
    return hi, mid, lo


def _dot3(a, b):
    ah, al = _split2(a)
    bh, bl = _split2(b)
    d = functools.partial(jnp.dot, preferred_element_type=F32)
    return d(ah, bh) + (d(ah, bl) + d(al, bh))


def _dot_exact_lhs(m_bf16, x):
    xh, xm, xl = _split3(x)
    d = functools.partial(jnp.dot, preferred_element_type=F32)
    return d(m_bf16, xh) + (d(m_bf16, xm) + d(m_bf16, xl))


def _cond_index(row0, rows: Rows):
    return jnp.maximum(0, (row0 - rows.ctx_rows + rows.lat_len) // rows.lat_len)


def _prep_kernel(xp_ref, xs_ref, pos_ref, o_ref, *, n_ctx_tiles):
    i = pl.program_id(0)

    @pl.when(i < n_ctx_tiles)
    def _():
        o_ref[...] = xp_ref[...]

    @pl.when(i >= n_ctx_tiles)
    def _():
        o_ref[...] = xs_ref[...] + pos_ref[...]


def _prep(xp2, xs2, pos, rows: Rows, tm):
    n_ctx_tiles = rows.ctx_rows // tm
    n_tiles = rows.total // tm
    pos_tiles = rows.lat_len // tm
    return pl.pallas_call(
        functools.partial(_prep_kernel, n_ctx_tiles=n_ctx_tiles),
        grid=(n_tiles,),
        in_specs=[
            pl.BlockSpec((tm, D_MODEL), lambda i: (jnp.minimum(i, n_ctx_tiles - 1), 0)),
            pl.BlockSpec((tm, D_MODEL), lambda i: (jnp.maximum(i - n_ctx_tiles, 0), 0)),
            pl.BlockSpec((tm, D_MODEL), lambda i: (jnp.maximum(i - n_ctx_tiles, 0) % pos_tiles, 0)),
        ],
        out_specs=pl.BlockSpec((tm, D_MODEL), lambda i: (i, 0)),
        out_shape=jax.ShapeDtypeStruct((rows.total, D_MODEL), F32),
        compiler_params=_cparams(("arbitrary",)),
        name="prep",
    )(xp2, xs2, pos)


def _mod_kernel(c_ref, w_ref, b_ref, o_ref):
    o_ref[0] = _bdot(_silu(c_ref[...]), w_ref[0]) + b_ref[0]


def _modulation(cond8, w_mod, b_mod):
    tn = 1024
    n6 = 6 * D_MODEL
    return pl.pallas_call(
        _mod_kernel,
        grid=(DEPTH, n6 // tn),
        in_specs=[
            pl.BlockSpec((8, D_MODEL), lambda l, n: (0, 0)),
            pl.BlockSpec((1, D_MODEL, tn), lambda l, n: (l, 0, n)),
            pl.BlockSpec((1, 1, tn), lambda l, n: (l, 0, n)),
        ],
        out_specs=pl.BlockSpec((1, 8, tn), lambda l, n: (l, 0, n)),
        out_shape=jax.ShapeDtypeStruct((DEPTH, 8, n6), F32),
        compiler_params=_cparams(("arbitrary", "arbitrary")),
        name="modulation",
    )(cond8, w_mod, b_mod.reshape(DEPTH, 1, n6))


def _proj_kernel(x_ref, mod_ref, wb_ref, wf_ref, ob_ref, of_ref, h_scr):
    n = pl.program_id(1)

    @pl.when(n == 0)
    def _():
        shift1 = mod_ref[0, 0:1, :]
        scale1 = mod_ref[0, 1:2, :]
        h = (_ln(x_ref[...]) * (1.0 + scale1) + shift1).astype(BF16)
        h_scr[...] = h
        of_ref[...] = jnp.dot(h, wf_ref[...], preferred_element_type=F32)

    ob_ref[...] = jnp.dot(h_scr[...], wb_ref[...], preferred_element_type=F32).astype(BF16)


def _projection(x, mod_l, w_big, w_f32, rows: Rows, tm, tn):
    nt = rows.total
    return pl.pallas_call(
        _proj_kernel,
        grid=(nt // tm, N_BIG // tn),
        in_specs=[
            pl.BlockSpec((tm, D_MODEL), lambda i, n: (i, 0)),
            pl.BlockSpec((1, 6, D_MODEL), lambda i, n: (_cond_index(i * tm, rows), 0, 0)),
            pl.BlockSpec((D_MODEL, tn), lambda i, n: (0, n)),
            pl.BlockSpec((D_MODEL, N_F32), lambda i, n: (0, 0)),
        ],
        out_specs=[
            pl.BlockSpec((tm, tn), lambda i, n: (i, n)),
            pl.BlockSpec((tm, N_F32), lambda i, n: (i, 0)),
        ],
        out_shape=[
            jax.ShapeDtypeStruct((nt, N_BIG), BF16),
            jax.ShapeDtypeStruct((nt, N_F32), F32),
        ],
        scratch_shapes=[pltpu.VMEM((tm, D_MODEL), BF16)],
        compiler_params=_cparams(("arbitrary", "arbitrary")),
        name="projection",
    )(x, mod_l, w_big, w_f32)


def _conv_taps(x, w_ref, seq_len):
    n = x.shape[0]
    pos = lax.broadcasted_iota(jnp.int32, x.shape, 0) & (seq_len - 1)
    xm1 = jnp.where(pos >= 1, pltpu.roll(x, 1, 0), 0.0)
    xp1 = jnp.where(pos <= seq_len - 2, pltpu.roll(x, n - 1, 0), 0.0)
    xp2 = jnp.where(pos <= seq_len - 3, pltpu.roll(x, n - 2, 0), 0.0)
    return xm1 * w_ref[0:1, :] + x * w_ref[1:2, :] + xp1 * w_ref[2:3, :] + xp2 * w_ref[3:4, :]


def _conv_a_kernel(x_ref, w_ref, o_ref, *, rows: Rows, br):
    b = pl.program_id(0)
    j = pl.program_id(1)
    seq_len = jnp.where(b * br < rows.ctx_rows, rows.ctx_len, rows.lat_len)
    y = _silu(_conv_taps(x_ref[...].astype(F32), w_ref[0], seq_len))
    nrm = lax.rsqrt(jnp.sum(y * y, axis=-1, keepdims=True) + RMS_EPS)
    fac = jnp.where(j < N_HEADS, nrm * HEAD_DIM ** -0.5, jnp.where(j < 2 * N_HEADS, nrm, 1.0))
    o_ref[...] = (y * fac).astype(BF16)


def _conv_a(proj_big, conv_w, l, rows: Rows, br):
    nt = rows.total
    ncol = 3 * N_HEADS
    return pl.pallas_call(
        functools.partial(_conv_a_kernel, rows=rows, br=br),
        grid=(nt // br, ncol),
        in_specs=[
            pl.BlockSpec((br, LANES), lambda b, j: (b, OFF_QKV // LANES + j)),
            pl.BlockSpec((1, CONV_W, LANES), lambda b, j: (l, 0, j)),
        ],
        out_specs=pl.BlockSpec((br, LANES), lambda b, j: (b, j)),
        out_shape=jax.ShapeDtypeStruct((nt, ncol * LANES), BF16),
        compiler_params=_cparams(("arbitrary", "arbitrary")),
        name="conv_a",
    )(proj_big, conv_w)


UNIT = 1024
NCH = UNIT // CHUNK


def _unit_layout(rows: Rows):
    n_ctx_units = rows.ctx_rows // UNIT
    lat_units = rows.lat_len // UNIT
    return n_ctx_units, lat_units, n_ctx_units + rows.n_lat * lat_units


def _unit_rowblock(u, d, rows: Rows):
    n_ctx_units, lat_units, _ = _unit_layout(rows)
    v = jnp.maximum(u - n_ctx_units, 0)
    b = v // lat_units
    j = v % lat_units
    jj = j + d * (lat_units - 1 - 2 * j)
    return jnp.where(u < n_ctx_units, u, n_ctx_units + b * lat_units + jj)


def _unit_lat_seq(u, rows: Rows):
    n_ctx_units, lat_units, _ = _unit_layout(rows)
    return jnp.maximum(u - n_ctx_units, 0) // lat_units


def _dir_masks(fwd):
    ii = lax.broadcasted_iota(jnp.int32, (CHUNK, CHUNK), 0)
    jj = lax.broadcasted_iota(jnp.int32, (CHUNK, CHUNK), 1)
    s = jnp.where(fwd, ii - jj, jj - ii)
    return s >= 0, s > 0, ii == jj


def _bmm(a, b):
    return jnp.einsum('gik,gkj->gij', a.astype(BF16), b.astype(BF16), preferred_element_type=F32)


def _bmm_nt(a, b):
    return jnp.einsum('gik,gjk->gij', a.astype(BF16), b.astype(BF16), preferred_element_type=F32)


def _chunk_cumsum(x, reverse):
    n = x.shape[0]
    pos = lax.broadcasted_iota(jnp.int32, x.shape, 0) & (CHUNK - 1)
    s = 1
    while s < CHUNK:
        if reverse:
            x = x + jnp.where(pos < CHUNK - s, pltpu.roll(x, n - s, 0), 0.0)
        else:
            x = x + jnp.where(pos >= s, pltpu.roll(x, s, 0), 0.0)
        s *= 2
    return x


def _dir_select(fwd, x):
    return jnp.where(fwd, x, pltpu.roll(x, LANES - N_HEADS, 1))


GROUP_CHUNKS = 4
GROUP_ROWS = GROUP_CHUNKS * CHUNK


def _group_columns(col_arr, row_arr, lane0):
    cols, rws = [], []
    for cc in range(GROUP_CHUNKS):
        for h in range(N_HEADS):
            cols.append(jnp.broadcast_to(col_arr[cc * CHUNK:(cc + 1) * CHUNK, lane0 + h:lane0 + h + 1],
                                         (CHUNK, HEAD_DIM)))
            if row_arr is not None:
                rws.append(jnp.broadcast_to(row_arr[lane0 + h:lane0 + h + 1, cc * CHUNK:(cc + 1) * CHUNK],
                                            (CHUNK, CHUNK)))
    return jnp.stack(cols), (jnp.stack(rws) if rws else None)


def _group_heads(ref, r0, col0):
    return jnp.stack([ref[pl.ds(r0 + cc * CHUNK, CHUNK), col0 + h * HEAD_DIM:col0 + (h + 1) * HEAD_DIM]
                      for cc in range(GROUP_CHUNKS) for h in range(N_HEADS)])


def _delta_kernel(qkv_ref, sm_ref, par_ref, s0_ref, o_ref, sf_ref,
                  s_scr, uv_scr, uk_scr, qk_scr, qd_scr, kd_scr, ge_scr, *, rows: Rows):
    d = pl.program_id(0)
    u = pl.program_id(1)
    n_ctx_units, lat_units, _ = _unit_layout(rows)
    seq_chunks = rows.ctx_len // CHUNK
    seq_per_unit = UNIT // rows.ctx_len
    fwd = d == 0
    is_ctx = u < n_ctx_units
    causal, strict, eye = _dir_masks(fwd)
    eye_f = jnp.where(eye, 1.0, 0.0)
    ii = lax.broadcasted_iota(jnp.int32, (CHUNK, CHUNK), 0)
    jj = lax.broadcasted_iota(jnp.int32, (CHUNK, CHUNK), 1)
    pair_masks = [jnp.logical_and((ii >> (s + 1)) == (jj >> (s + 1)), (ii >> s) != (jj >> s))
                  for s in range(CHUNK.bit_length() - 1)]
    ng = GROUP_CHUNKS * N_HEADS

    def pre(it, carry):
        r0 = pl.multiple_of(it * GROUP_ROWS, GROUP_ROWS)
        g0 = pl.multiple_of(it * ng, ng)
        sm = sm_ref[pl.ds(r0, GROUP_ROWS), :]
        g_all = -jnp.exp(par_ref[0:1, :]) * _softplus(sm + par_ref[1:2, :])
        gcum = _dir_select(fwd, jnp.where(fwd, _chunk_cumsum(g_all, False), _chunk_cumsum(g_all, True)))
        bsel = _dir_select(fwd, _sigmoid(sm))
        gi, grow = _group_columns(gcum, gcum.T, SM_ALPHA)
        beta = jnp.stack([bsel[cc * CHUNK:(cc + 1) * CHUNK, SM_BETA + h:SM_BETA + h + 1]
                          for cc in range(GROUP_CHUNKS) for h in range(N_HEADS)])
        q = _group_heads(qkv_ref, r0, 0)
        k = _group_heads(qkv_ref, r0, W_MIX)
        v = _group_heads(qkv_ref, r0, 2 * W_MIX)
        kf = k.astype(F32)
        decay = jnp.exp(jnp.where(causal, gi[:, :, :CHUNK] - grow, -jnp.inf))
        lmat = jnp.where(strict, beta * _bmm_nt(k, k) * decay, 0.0)
        eg = jnp.exp(gi)
        rhs = jnp.concatenate([v.astype(F32) * beta, kf * (beta * eg)], axis=-1)
        t = eye_f - jnp.where(pair_masks[0], lmat, 0.0)
        for pm in pair_masks[1:]:
            tb = t.astype(BF16)
            t = t - _bmm(_bmm(tb, jnp.where(pm, lmat, 0.0)), tb)
        t0 = t.astype(BF16)
        mh, ml = _split2(eye_f + lmat)
        resid = eye_f - (_bmm(mh, t0) + _bmm(ml, t0))
        t1 = t0.astype(F32) + _bmm(t0, resid)
        uu = _bmm(t1, rhs)
        qk = jnp.where(causal, _bmm_nt(q, k) * decay, 0.0)
        g_end = jnp.where(fwd, gi[:, CHUNK - 1:CHUNK, :], gi[:, 0:1, :])
        uv_scr[pl.ds(g0, ng)] = uu[:, :, :HEAD_DIM]
        uk_scr[pl.ds(g0, ng)] = uu[:, :, HEAD_DIM:].astype(BF16)
        qk_scr[pl.ds(g0, ng)] = qk.astype(BF16)
        qd_scr[pl.ds(g0, ng)] = (q.astype(F32) * eg).astype(BF16)
        kd_scr[pl.ds(g0, ng)] = (kf * jnp.exp(g_end - gi)).astype(BF16)
        ge_scr[pl.ds(g0, ng)] = jnp.broadcast_to(jnp.exp(g_end), (ng, 8, HEAD_DIM))
        return carry

    lax.fori_loop(0, NCH // GROUP_CHUNKS, pre, 0)

    first_lat = jnp.logical_and(u >= n_ctx_units, (u - n_ctx_units) % lat_units == 0)

    @pl.when(first_lat)
    def _():
        s_scr[...] = s0_ref[0, 0, 0]

    @pl.when(u == n_ctx_units)
    def _():
        sf_ref[...] = jnp.zeros(sf_ref.shape, F32)

    def step(n, carry):
        c = jnp.where(fwd, n, NCH - 1 - n)
        r0 = pl.multiple_of(c * CHUNK, CHUNK)
        hs = pl.ds(pl.multiple_of(c * N_HEADS, N_HEADS), N_HEADS)

        @pl.when(jnp.logical_and(is_ctx, n % seq_chunks == 0))
        def _():
            s_scr[...] = jnp.zeros(s_scr.shape, F32)

        s = s_scr[...]
        sb = s.astype(BF16)
        ub = (uv_scr[hs] - _bmm(uk_scr[hs], sb)).astype(BF16)
        o = _bmm(qd_scr[hs], sb) + _bmm(qk_scr[hs], ub)
        kd = kd_scr[hs]
        ge = ge_scr[hs]
        for h in range(N_HEADS):
            s_scr[h] = s[h] * ge[h, 0:1, :] + lax.dot_general(
                kd[h], ub[h], (((0,), (0,)), ((), ())), preferred_element_type=F32)
            o_ref[0, pl.ds(r0, CHUNK), h * HEAD_DIM:(h + 1) * HEAD_DIM] = o[h]

        @pl.when(jnp.logical_and(is_ctx, n % seq_chunks == seq_chunks - 1))
        def _():
            sf_ref[c // seq_chunks, 0] = s_scr[...]

        return carry

    lax.fori_loop(0, NCH, step, 0)


def _delta(qkv_c, proj_f32, par, state, l, rows: Rows):
    n_ctx_units, lat_units, n_units = _unit_layout(rows)
    seq_per_unit = UNIT // rows.ctx_len
    nt = rows.total
    sm_col = W_B // LANES
    kern = functools.partial(_delta_kernel, rows=rows)
    hshape = (NCH * N_HEADS, CHUNK, HEAD_DIM)
    return pl.pallas_call(
        kern,
        grid=(2, n_units),
        in_specs=[
            pl.BlockSpec((UNIT, 3 * W_MIX), lambda d, u: (_unit_rowblock(u, d, rows), 0)),
            pl.BlockSpec((UNIT, LANES), lambda d, u: (_unit_rowblock(u, d, rows), sm_col)),
            pl.BlockSpec((8, LANES), lambda d, u: (0, 0)),
            pl.BlockSpec((1, 1, 1, N_HEADS, HEAD_DIM, HEAD_DIM),
                         lambda d, u: (_unit_lat_seq(u, rows), l, d, 0, 0, 0)),
        ],
        out_specs=[
            pl.BlockSpec((1, UNIT, W_MIX), lambda d, u: (d, _unit_rowblock(u, d, rows), 0)),
            pl.BlockSpec((seq_per_unit, 1, N_HEADS, HEAD_DIM, HEAD_DIM),
                         lambda d, u: (jnp.minimum(u, n_ctx_units), d, 0, 0, 0)),
        ],
        out_shape=[
            jax.ShapeDtypeStruct((2, nt, W_MIX), F32),
            jax.ShapeDtypeStruct((rows.n_ctx + seq_per_unit, 2, N_HEADS, HEAD_DIM, HEAD_DIM), F32),
        ],
        scratch_shapes=[
            pltpu.VMEM((N_HEADS, HEAD_DIM, HEAD_DIM), F32),
            pltpu.VMEM(hshape, F32),
            pltpu.VMEM(hshape, BF16),
            pltpu.VMEM((NCH * N_HEADS, CHUNK, CHUNK), BF16),
            pltpu.VMEM(hshape, BF16),
            pltpu.VMEM(hshape, BF16),
            pltpu.VMEM((NCH * N_HEADS, 8, HEAD_DIM), F32),
        ],
        compiler_params=_cparams(("arbitrary", "arbitrary")),
        name="delta",
    )(qkv_c, proj_f32, par, state)


def _mlstm_kernel(big_ref, sm_ref, par_ref, c0_ref, n0_ref, m0_ref, o_ref, cf_ref, nf_ref, mf_ref,
                  c_scr, n_scr, m_scr, dm_scr, qk_scr, fi_scr, mi_scr, de_scr, me_scr, *, rows: Rows):
    d = pl.program_id(0)
    u = pl.program_id(1)
    n_ctx_units, lat_units, _ = _unit_layout(rows)
    seq_chunks = rows.ctx_len // CHUNK
    fwd = d == 0
    is_ctx = u < n_ctx_units
    causal, _, _ = _dir_masks(fwd)
    qo, ko, vo = 0, W_MIX, 2 * W_MIX
    ng = GROUP_CHUNKS * N_HEADS

    def pre(it, carry):
        r0 = pl.multiple_of(it * GROUP_ROWS, GROUP_ROWS)
        gs = pl.ds(pl.multiple_of(it * ng, ng), ng)
        sm = sm_ref[pl.ds(r0, GROUP_ROWS), :]
        f_all = -_softplus(-(sm + par_ref[3:4, :]))
        fcum = _dir_select(fwd, jnp.where(fwd, _chunk_cumsum(f_all, False), _chunk_cumsum(f_all, True)))
        i_al = pltpu.roll(_dir_select(fwd, sm + par_ref[2:3, :]), SM_F - SM_I, 1)
        fi, hrow = _group_columns(fcum, (fcum - i_al).T, SM_F)
        it_b, _ = _group_columns(i_al, None, SM_F)
        q = _group_heads(big_ref, r0, qo)
        ks = _group_heads(big_ref, r0, ko).astype(F32) * HEAD_DIM ** -0.5
        dmat = jnp.where(causal, fi[:, :, :CHUNK] - hrow, -jnp.inf)
        f_end = jnp.where(fwd, fi[:, CHUNK - 1:CHUNK, :], fi[:, 0:1, :])
        d_end = f_end - fi + it_b
        dm_scr[gs] = dmat
        qk_scr[gs] = _bmm_nt(q, ks)
        fi_scr[gs] = fi
        mi_scr[gs] = jnp.broadcast_to(jnp.max(dmat, axis=-1, keepdims=True), (ng, CHUNK, HEAD_DIM))
        de_scr[gs] = d_end
        me_scr[gs] = jnp.broadcast_to(jnp.max(d_end, axis=1, keepdims=True), (ng, 8, HEAD_DIM))
        return carry

    lax.fori_loop(0, NCH // GROUP_CHUNKS, pre, 0)

    first_lat = jnp.logical_and(u >= n_ctx_units, (u - n_ctx_units) % lat_units == 0)

    @pl.when(first_lat)
    def _():
        c_scr[...] = c0_ref[0, 0, 0]
        for h in range(N_HEADS):
            n_scr[h] = jnp.broadcast_to(n0_ref[0, 0, 0, h:h + 1, :], (8, HEAD_DIM))
            m_scr[h] = jnp.broadcast_to(m0_ref[0, 0, 0, h:h + 1, :], (8, HEAD_DIM))

    @pl.when(u == n_ctx_units)
    def _():
        cf_ref[...] = jnp.zeros(cf_ref.shape, F32)
        nf_ref[...] = jnp.zeros(nf_ref.shape, F32)
        mf_ref[...] = jnp.zeros(mf_ref.shape, F32)

    def step(n, carry):
        c = jnp.where(fwd, n, NCH - 1 - n)
        r0 = pl.multiple_of(c * CHUNK, CHUNK)

        @pl.when(jnp.logical_and(is_ctx, n % seq_chunks == 0))
        def _():
            c_scr[...] = jnp.zeros(c_scr.shape, F32)
            n_scr[...] = jnp.zeros(n_scr.shape, F32)
            m_scr[...] = jnp.zeros(m_scr.shape, F32)

        hs = pl.ds(pl.multiple_of(c * N_HEADS, N_HEADS), N_HEADS)
        heads = lambda off: jnp.stack([big_ref[pl.ds(r0, CHUNK), off + h * HEAD_DIM:off + (h + 1) * HEAD_DIM]
                                       for h in range(N_HEADS)])
        q, v = heads(qo), heads(vo)
        ks = heads(ko).astype(F32) * HEAD_DIM ** -0.5
        cs = c_scr[...]
        ns = n_scr[...][:, 0:1, :]
        ms = m_scr[...][:, 0:1, :]
        fi = fi_scr[hs]
        inter = fi + ms
        mt = jnp.maximum(inter, mi_scr[hs])
        w_int = jnp.exp(inter - mt)
        pm = jnp.exp(dm_scr[hs] - mt[:, :, :CHUNK]) * qk_scr[hs]
        num = w_int * _bmm(q, cs) + _bmm(pm, v)
        qn = jnp.sum(q.astype(F32) * ns, axis=-1, keepdims=True)
        den = w_int[:, :, 0:1] * qn + jnp.sum(pm, axis=-1, keepdims=True)
        hh = num / jnp.maximum(jnp.abs(den), jnp.exp(-mt[:, :, 0:1]))
        f_end = jnp.where(fwd, fi[:, CHUNK - 1:CHUNK, :], fi[:, 0:1, :])
        inter_end = f_end + ms
        m_new = jnp.maximum(inter_end, me_scr[hs][:, 0:1, :])
        w_tok = jnp.exp(de_scr[hs] - m_new)
        s_int = jnp.exp(inter_end - m_new)
        kw = ks * w_tok
        n_new = ns * s_int + jnp.sum(kw, axis=1, keepdims=True)
        n_scr[...] = jnp.broadcast_to(n_new, n_scr.shape)
        m_scr[...] = jnp.broadcast_to(m_new, m_scr.shape)
        for h in range(N_HEADS):
            c_scr[h] = cs[h] * s_int[h, :, 0:1] + _bdot_tn(kw[h], v[h])
            o_ref[0, pl.ds(r0, CHUNK), h * HEAD_DIM:(h + 1) * HEAD_DIM] = hh[h]

        @pl.when(jnp.logical_and(is_ctx, n % seq_chunks == seq_chunks - 1))
        def _():
            sq = c // seq_chunks
            cf_ref[sq, 0] = c_scr[...]
            nf_ref[sq, 0] = n_scr[...]
            mf_ref[sq, 0] = m_scr[...]

        return carry

    lax.fori_loop(0, NCH, step, 0)


def _mlstm(proj_big, proj_f32, par, st_c, st_n8, st_m8, l, rows: Rows):
    n_ctx_units, lat_units, n_units = _unit_layout(rows)
    seq_per_unit = UNIT // rows.ctx_len
    nt = rows.total
    sm_col = W_B // LANES
    qkv_blk = OFF_QC // (3 * W_MIX)
    assert OFF_QC % (3 * W_MIX) == 0
    kern = functools.partial(_mlstm_kernel, rows=rows)
    hshape = (NCH * N_HEADS, CHUNK, HEAD_DIM)
    n_out = rows.n_ctx + seq_per_unit
    return pl.pallas_call(
        kern,
        grid=(2, n_units),
        in_specs=[
            pl.BlockSpec((UNIT, 3 * W_MIX), lambda d, u: (_unit_rowblock(u, d, rows), qkv_blk)),
            pl.BlockSpec((UNIT, LANES), lambda d, u: (_unit_rowblock(u, d, rows), sm_col)),
            pl.BlockSpec((8, LANES), lambda d, u: (0, 0)),
            pl.BlockSpec((1, 1, 1, N_HEADS, HEAD_DIM, HEAD_DIM),
                         lambda d, u: (_unit_lat_seq(u, rows), l, d, 0, 0, 0)),
            pl.BlockSpec((1, 1, 1, N_HEADS, HEAD_DIM), lambda d, u: (_unit_lat_seq(u, rows), l, d, 0, 0)),
            pl.BlockSpec((1, 1, 1, N_HEADS, HEAD_DIM), lambda d, u: (_unit_lat_seq(u, rows), l, d, 0, 0)),
        ],
        out_specs=[
            pl.BlockSpec((1, UNIT, W_MIX), lambda d, u: (d, _unit_rowblock(u, d, rows), 0)),
            pl.BlockSpec((seq_per_unit, 1, N_HEADS, HEAD_DIM, HEAD_DIM),
                         lambda d, u: (jnp.minimum(u, n_ctx_units), d, 0, 0, 0)),
            pl.BlockSpec((seq_per_unit, 1, N_HEADS, 8, HEAD_DIM),
                         lambda d, u: (jnp.minimum(u, n_ctx_units), d, 0, 0, 0)),
            pl.BlockSpec((seq_per_unit, 1, N_HEADS, 8, HEAD_DIM),
                         lambda d, u: (jnp.minimum(u, n_ctx_units), d, 0, 0, 0)),
        ],
        out_shape=[
            jax.ShapeDtypeStruct((2, nt, W_MIX), F32),
            jax.ShapeDtypeStruct((n_out, 2, N_HEADS, HEAD_DIM, HEAD_DIM), F32),
            jax.ShapeDtypeStruct((n_out, 2, N_HEADS, 8, HEAD_DIM), F32),
            jax.ShapeDtypeStruct((n_out, 2, N_HEADS, 8, HEAD_DIM), F32),
        ],
        scratch_shapes=[
            pltpu.VMEM((N_HEADS, HEAD_DIM, HEAD_DIM), F32),
            pltpu.VMEM((N_HEADS, 8, HEAD_DIM), F32),
            pltpu.VMEM((N_HEADS, 8, HEAD_DIM), F32),
            pltpu.VMEM((NCH * N_HEADS, CHUNK, CHUNK), F32),
            pltpu.VMEM((NCH * N_HEADS, CHUNK, CHUNK), F32),
            pltpu.VMEM(hshape, F32),
            pltpu.VMEM(hshape, F32),
            pltpu.VMEM(hshape, F32),
            pltpu.VMEM((NCH * N_HEADS, 8, HEAD_DIM), F32),
        ],
        compiler_params=_cparams(("arbitrary", "arbitrary")),
        name="mlstm",
    )(proj_big, proj_f32, par, st_c, st_n8, st_m8)


def _gelu_tanh(x):
    return 0.5 * x * (1.0 + jnp.tanh(math.sqrt(2.0 / math.pi) * (x + 0.044715 * (x * x * x))))


def _tile_scan(a, b, reverse):
    n = a.shape[0]
    row = lax.broadcasted_iota(jnp.int32, a.shape, 0)
    s = 1
    while s < n:
        if reverse:
            ok = row < n - s
            a_sh = jnp.where(ok, pltpu.roll(a, n - s, 0), 1.0)
            b_sh = jnp.where(ok, pltpu.roll(b, n - s, 0), 0.0)
        else:
            ok = row >= s
            a_sh = jnp.where(ok, pltpu.roll(a, s, 0), 1.0)
            b_sh = jnp.where(ok, pltpu.roll(b, s, 0), 0.0)
        b = a * b_sh + b
        a = a * a_sh
        s *= 2
    return a, b


def _lru_kernel(x_ref, g_ref, cw_ref, cb_ref, wa_ref, ba_ref, wx_ref, bx_ref, lam_ref, h0_ref,
                y_ref, hf_ref, xc_scr, hfw_scr, *, rows: Rows, br):
    blk = pl.program_id(0)
    is_ctx = blk * br < rows.ctx_rows
    ts = rows.ctx_len
    n_tiles = br // ts
    seq_len = jnp.where(is_ctx, rows.ctx_len, rows.lat_len)
    xc_scr[...] = _conv_taps(x_ref[...], cw_ref[0], seq_len) + cb_ref[0]

    for dd in range(2):
        reverse = dd == 1
        sp_lam = _softplus(-lam_ref[0, dd])

        def tile(n, carry, dd=dd, reverse=reverse, sp_lam=sp_lam):
            t = (n_tiles - 1 - n) if reverse else n
            r0 = pl.multiple_of(t * ts, ts)
            xc = xc_scr[pl.ds(r0, ts), :]
            r = _sigmoid(_bdot(xc, wa_ref[0, dd, 0]) + ba_ref[0, dd])
            gi = _sigmoid(_bdot(xc, wx_ref[0, dd, 0]) + bx_ref[0, dd])
            log_a = -LRU_C * r * sp_lam
            a = jnp.exp(log_a)
            b = jnp.sqrt(-jnp.tanh(log_a) * (a * a + 1.0)) * (gi * xc)
            a_cum, h = _tile_scan(a, b, reverse)
            carry = jnp.where(is_ctx, 0.0, carry)
            h = h + a_cum * carry
            last = h[0:1, :] if reverse else h[ts - 1:ts, :]
            hf_ref[0, dd, pl.ds(t, 1), :] = last
            if reverse:
                g = g_ref[pl.ds(r0, ts), :].astype(F32)
                y_ref[pl.ds(r0, ts), :] = ((hfw_scr[pl.ds(r0, ts), :] + h) * _gelu_tanh(g)).astype(BF16)
            else:
                hfw_scr[pl.ds(r0, ts), :] = h
            return last

        lax.fori_loop(0, n_tiles, tile, h0_ref[0, 0, dd])


def _lru(proj_f32, proj_big, conv_w, conv_b, wa, ba, wx, bx, lam, h0, l, rows: Rows):
    br = rows.lat_len
    nt = rows.total
    n_blocks = nt // br
    n_ctx_blocks = rows.ctx_rows // br
    n_tiles = br // rows.ctx_len
    gcol = OFF_GB // BW_B

    def lat_seq(b):
        return jnp.maximum(b - n_ctx_blocks, 0)

    vec = lambda a: a.reshape(DEPTH, 2, 1, W_B)
    return pl.pallas_call(
        functools.partial(_lru_kernel, rows=rows, br=br),
        grid=(n_blocks, NB_B),
        in_specs=[
            pl.BlockSpec((br, BW_B), lambda b, j: (b, j)),
            pl.BlockSpec((br, BW_B), lambda b, j: (b, gcol + j)),
            pl.BlockSpec((1, CONV_W, BW_B), lambda b, j: (l, 0, j)),
            pl.BlockSpec((1, 1, BW_B), lambda b, j: (l, 0, j)),
            pl.BlockSpec((1, 2, 1, BW_B, BW_B), lambda b, j: (l, 0, j, 0, 0)),
            pl.BlockSpec((1, 2, 1, BW_B), lambda b, j: (l, 0, 0, j)),
            pl.BlockSpec((1, 2, 1, BW_B, BW_B), lambda b, j: (l, 0, j, 0, 0)),
            pl.BlockSpec((1, 2, 1, BW_B), lambda b, j: (l, 0, 0, j)),
            pl.BlockSpec((1, 2, 1, BW_B), lambda b, j: (l, 0, 0, j)),
            pl.BlockSpec((1, 1, 2, 1, BW_B), lambda b, j: (lat_seq(b), l, 0, 0, j)),
        ],
        out_specs=[
            pl.BlockSpec((br, BW_B), lambda b, j: (b, j)),
            pl.BlockSpec((1, 2, n_tiles, BW_B), lambda b, j: (b, 0, 0, j)),
        ],
        out_shape=[
            jax.ShapeDtypeStruct((nt, W_B), BF16),
            jax.ShapeDtypeStruct((n_blocks, 2, n_tiles, W_B), F32),
        ],
        scratch_shapes=[pltpu.VMEM((br, BW_B), F32), pltpu.VMEM((br, BW_B), F32)],
        compiler_params=_cparams(("arbitrary", "arbitrary")),
        name="lru",
    )(proj_f32, proj_big, conv_w, conv_b.reshape(DEPTH, 1, W_B), wa, vec(ba), wx, vec(bx), vec(lam),
      h0.reshape(h0.shape[0], DEPTH, 2, 1, W_B))


def _route(lg, le):
    lane = lax.broadcasted_iota(jnp.int32, lg.shape, 1)
    neg = -jnp.inf
    lgm = jnp.where(lane < N_GROUPS, lg, neg)
    gmax = jnp.max(lgm, axis=-1, keepdims=True)
    p_grp = 1.0 / jnp.sum(jnp.exp(lgm - gmax), axis=-1, keepdims=True)
    g_sel = jnp.min(jnp.where(lgm == gmax, lane, LANES), axis=-1, keepdims=True)
    in_grp = jnp.logical_and(lane >= g_sel * E_PER_GROUP, lane < (g_sel + 1) * E_PER_GROUP)
    lem = jnp.where(in_grp, le, neg)
    v1 = jnp.max(lem, axis=-1, keepdims=True)
    i1 = jnp.min(jnp.where(lem == v1, lane, LANES), axis=-1, keepdims=True)
    lem2 = jnp.where(lane == i1, neg, lem)
    v2 = jnp.max(lem2, axis=-1, keepdims=True)
    i2 = jnp.min(jnp.where(lem2 == v2, lane, LANES), axis=-1, keepdims=True)
    e2 = jnp.exp(v2 - v1)
    w1 = p_grp / (1.0 + e2)
    w2 = p_grp * e2 / (1.0 + e2)
    return jnp.where(lane == i1, w1, 0.0) + jnp.where(lane == i2, w2, 0.0)


def _merge_kernel(x_ref, mod_ref, oa_ref, z_ref, yb_ref, hc_ref, oc_ref, gt_ref, dn_ref, mn_ref,
                  wpa_ref, wpb_ref, wpc_ref, wout_ref, g1_ref, b1_ref, wr_ref, br_ref,
                  x1_ref, h2_ref, gate_ref):
    oa = oa_ref[0] + oa_ref[1]
    hc = hc_ref[0] + hc_ref[1]
    ya, yc = [], []
    for h in range(N_HEADS):
        sl = slice(h * HEAD_DIM, (h + 1) * HEAD_DIM)
        o_h = oa[:, sl]
        o_h = o_h * lax.rsqrt(jnp.mean(o_h * o_h, axis=-1, keepdims=True) + RMS_EPS) * dn_ref[...]
        ya.append((o_h * _silu(z_ref[:, sl].astype(F32))).astype(BF16))
        c_h = _ln(hc[:, sl]) * mn_ref[:, sl]
        yc.append((_sigmoid(oc_ref[:, sl].astype(F32)) * c_h).astype(BF16))
    ya = jnp.concatenate(ya, axis=-1)
    yc = jnp.concatenate(yc, axis=-1)
    ga = _sigmoid(gt_ref[:, 0:D_MODEL].astype(F32))
    gb = _sigmoid(gt_ref[:, D_MODEL:2 * D_MODEL].astype(F32))
    gc = _sigmoid(gt_ref[:, 2 * D_MODEL:3 * D_MODEL].astype(F32))
    d = functools.partial(jnp.dot, preferred_element_type=F32)
    merged = ga * d(ya, wpa_ref[...]) + gb * d(yb_ref[...], wpb_ref[...]) + gc * d(yc, wpc_ref[...])
    mixed = d(merged.astype(BF16), wout_ref[...])
    gate1 = mod_ref[0, 2:3, :]
    shift2 = mod_ref[0, 3:4, :]
    scale2 = mod_ref[0, 4:5, :]
    x1 = _ln(DN_ALPHA * x_ref[...] + gate1 * mixed) * g1_ref[...] + b1_ref[...]
    x1_ref[...] = x1
    h2 = _ln(x1) * (1.0 + scale2) + shift2
    h2_ref[...] = h2.astype(BF16)
    logits = _dot3(h2, wr_ref[...]) + br_ref[...]
    lg = logits
    le = pltpu.roll(logits, LANES - 64, 1)
    gate_ref[...] = _route(lg, le)


def _merge(x, mod_l, o_a, proj_big, y_b, h_c, dn, mn, wpa, wpb, wpc, wout, g1, b1, wr, br, rows: Rows, tm):
    nt = rows.total
    row = lambda i: (i, 0)
    const = lambda i: (0, 0)
    cw = W_MIX
    return pl.pallas_call(
        _merge_kernel,
        grid=(nt // tm,),
        in_specs=[
            pl.BlockSpec((tm, D_MODEL), row),
            pl.BlockSpec((1, 6, D_MODEL), lambda i: (_cond_index(i * tm, rows), 0, 0)),
            pl.BlockSpec((2, tm, cw), lambda i: (0, i, 0)),
            pl.BlockSpec((tm, cw), lambda i: (i, OFF_Z // cw)),
            pl.BlockSpec((tm, cw), row),
            pl.BlockSpec((2, tm, cw), lambda i: (0, i, 0)),
            pl.BlockSpec((tm, cw), lambda i: (i, OFF_OC // cw)),
            pl.BlockSpec((tm, 3 * D_MODEL), lambda i: (i, OFF_GATES // (3 * D_MODEL))),
            pl.BlockSpec((1, HEAD_DIM), const),
            pl.BlockSpec((1, cw), const),
            pl.BlockSpec((cw, D_MODEL), const),
            pl.BlockSpec((cw, D_MODEL), const),
            pl.BlockSpec((cw, D_MODEL), const),
            pl.BlockSpec((D_MODEL, D_MODEL), const),
            pl.BlockSpec((1, D_MODEL), const),
            pl.BlockSpec((1, D_MODEL), const),
            pl.BlockSpec((D_MODEL, LANES), const),
            pl.BlockSpec((1, LANES), const),
        ],
        out_specs=[
            pl.BlockSpec((tm, D_MODEL), row),
            pl.BlockSpec((tm, D_MODEL), row),
            pl.BlockSpec((tm, LANES), row),
        ],
        out_shape=[
            jax.ShapeDtypeStruct((nt, D_MODEL), F32),
            jax.ShapeDtypeStruct((nt, D_MODEL), BF16),
            jax.ShapeDtypeStruct((nt, LANES), F32),
        ],
        compiler_params=_cparams(("arbitrary",)),
        name="merge",
    )(x, mod_l, o_a, proj_big, y_b, h_c, proj_big, proj_big, dn, mn, wpa, wpb, wpc, wout, g1, b1, wr, br)


def _moe_kernel(h_ref, gate_ref, w1_ref, w3_ref, w2_ref, x1_ref, mod_ref, g2_ref, b2_ref, o_ref, acc_scr):
    e = pl.program_id(1)

    @pl.when(e == 0)
    def _():
        acc_scr[...] = jnp.zeros(acc_scr.shape, F32)

    h = h_ref[...]
    lane = lax.broadcasted_iota(jnp.int32, gate_ref.shape, 1)
    gcol = jnp.sum(jnp.where(lane == e, gate_ref[...], 0.0), axis=-1, keepdims=True)
    a = jnp.dot(h, w1_ref[0].astype(BF16), preferred_element_type=F32)
    b = jnp.dot(h, w3_ref[0].astype(BF16), preferred_element_type=F32)
    hid = _silu(a) * b * gcol
    acc_scr[...] += jnp.dot(hid.astype(BF16), w2_ref[0].astype(BF16), preferred_element_type=F32)

    @pl.when(e == N_EXPERTS - 1)
    def _():
        gate2 = mod_ref[0, 5:6, :]
        o_ref[...] = _ln(DN_ALPHA * x1_ref[...] + gate2 * acc_scr[...]) * g2_ref[...] + b2_ref[...]


def _moe(h2, gate, w1, w3, w2, x1, mod_l, g2, b2, l, rows: Rows, tm):
    nt = rows.total
    row = lambda i, e: (i, 0)
    const = lambda i, e: (0, 0)
    return pl.pallas_call(
        _moe_kernel,
        grid=(nt // tm, N_EXPERTS),
        in_specs=[
            pl.BlockSpec((tm, D_MODEL), row),
            pl.BlockSpec((tm, LANES), row),
            pl.BlockSpec((1, D_MODEL, D_EXPERT), lambda i, e: (l * N_EXPERTS + e, 0, 0)),
            pl.BlockSpec((1, D_MODEL, D_EXPERT), lambda i, e: (l * N_EXPERTS + e, 0, 0)),
            pl.BlockSpec((1, D_EXPERT, D_MODEL), lambda i, e: (l * N_EXPERTS + e, 0, 0)),
            pl.BlockSpec((tm, D_MODEL), row),
            pl.BlockSpec((1, 6, D_MODEL), lambda i, e: (_cond_index(i * tm, rows), 0, 0)),
            pl.BlockSpec((1, D_MODEL), const),
            pl.BlockSpec((1, D_MODEL), const),
        ],
        out_specs=pl.BlockSpec((tm, D_MODEL), row),
        out_shape=jax.ShapeDtypeStruct((nt, D_MODEL), F32),
        scratch_shapes=[pltpu.VMEM((tm, D_MODEL), F32)],
        compiler_params=_cparams(("arbitrary", "arbitrary")),
        name="moe",
    )(h2, gate, w1, w3, w2, x1, mod_l, g2, b2)


def _grid_pos_embed(n_tokens):
    rows = n_tokens // GRID_W
    r, col = jnp.meshgrid(jnp.arange(rows, dtype=F32), jnp.arange(GRID_W, dtype=F32), indexing='ij')
    quarter = D_MODEL // 4
    freqs = jnp.exp(-math.log(POS_BASE) * jnp.arange(quarter, dtype=F32) / quarter)
    ar = r.reshape(-1, 1) * freqs
    ac = col.reshape(-1, 1) * freqs
    return jnp.concatenate([jnp.sin(ar), jnp.cos(ar), jnp.sin(ac), jnp.cos(ac)], axis=-1)


def _pack_w_in(w_in_l):
    sizes = (3 * W_MIX, W_B, W_MIX, 8, 8, W_B, W_MIX, W_MIX, W_MIX, W_MIX, 8, 8, 3 * D_MODEL)
    parts, start = [], 0
    for s in sizes:
        parts.append(w_in_l[:, start:start + s])
        start += s
    qkv_a, x_b, z_a, beta, alpha, g_b, q_c, k_c, v_c, o_c, i_c, f_c, gates = parts
    big = jnp.concatenate([gates, qkv_a, q_c, k_c, v_c, z_a, g_b, o_c], axis=1).astype(BF16)
    pad = jnp.zeros((D_MODEL, LANES - 32), F32)
    small = jnp.concatenate([x_b, beta, alpha, i_c, f_c, pad], axis=1).astype(BF16)
    return big, small


def _lane_row(vals, off):
    out = jnp.zeros((8, LANES), F32)
    for r, (v, o) in enumerate(zip(vals, off)):
        out = out.at[r, o:o + 8].set(v.reshape(-1))
    return out


def kernel(x_prompt, x_sample, state_delta, state_lru, state_mlstm_C, state_mlstm_n, state_mlstm_m, c, c_ctx,
           w_mod, b_mod, w_in, conv_a, delta_a_log, delta_dt_bias, delta_norm, conv_b_w, conv_b_b,
           lru_wa, lru_ba, lru_wx, lru_bx, lru_lambda, mlstm_bi, mlstm_bf, mlstm_norm,
           w_pa, w_pb, w_pc, w_out, ln1_g, ln1_b, ln2_g, ln2_b, w_rg, b_rg, w_re, b_re, w_e1, w_e3, w_e2):
    n_ctx, ctx_len, _ = x_prompt.shape
    n_lat, lat_len, _ = x_sample.shape
    rows = Rows(n_ctx, ctx_len, n_lat, lat_len)
    assert rows.ctx_rows % UNIT == 0 and lat_len % UNIT == 0 and UNIT % ctx_len == 0
    assert rows.ctx_rows % lat_len == 0 and ctx_len % CHUNK == 0 and n_lat <= 7

    tm = min(1024, lat_len)
    pos = _grid_pos_embed(lat_len)
    x = _prep(x_prompt.reshape(rows.ctx_rows, D_MODEL), x_sample.reshape(n_lat * lat_len, D_MODEL), pos, rows, tm)

    cond8 = jnp.zeros((8, D_MODEL), F32).at[0].set(c_ctx).at[1:1 + n_lat].set(c)
    mod = _modulation(cond8, w_mod, b_mod).reshape(DEPTH, 8, 6, D_MODEL)

    m_bcast = jnp.broadcast_to(state_mlstm_m[..., None], state_mlstm_m.shape + (HEAD_DIM,))
    w_e1f = w_e1.reshape(DEPTH * N_EXPERTS, D_MODEL, D_EXPERT)
    w_e3f = w_e3.reshape(DEPTH * N_EXPERTS, D_MODEL, D_EXPERT)
    w_e2f = w_e2.reshape(DEPTH * N_EXPERTS, D_EXPERT, D_MODEL)

    finals = []
    for l in range(DEPTH):
        w_big, w_small = _pack_w_in(w_in[l])
        par = _lane_row([delta_a_log[l], delta_dt_bias[l], mlstm_bi[l], mlstm_bf[l]],
                        [SM_ALPHA, SM_ALPHA, SM_I, SM_F])
        proj_big, proj_f32 = _projection(x, mod[l], w_big, w_small, rows, tm, 1280)
        qkv_c = _conv_a(proj_big, conv_a, l, rows, lat_len)
        o_a, sf_a = _delta(qkv_c, proj_f32, par, state_delta, l, rows)
        y_b, hf_b = _lru(proj_f32, proj_big, conv_b_w, conv_b_b, lru_wa, lru_ba, lru_wx, lru_bx, lru_lambda,
                         state_lru, l, rows)
        h_c, cf, nf, mf = _mlstm(proj_big, proj_f32, par, state_mlstm_C, state_mlstm_n, m_bcast, l, rows)
        wr = jnp.zeros((D_MODEL, LANES), F32).at[:, :N_GROUPS].set(w_rg[l]).at[:, 64:64 + N_EXPERTS].set(w_re[l])
        br = jnp.zeros((1, LANES), F32).at[0, :N_GROUPS].set(b_rg[l]).at[0, 64:64 + N_EXPERTS].set(b_re[l])
        x1, h2, gate = _merge(x, mod[l], o_a, proj_big, y_b, h_c,
                              delta_norm[l].reshape(1, HEAD_DIM), mlstm_norm[l].reshape(1, W_MIX),
                              w_pa[l].astype(BF16), w_pb[l].astype(BF16), w_pc[l].astype(BF16),
                              w_out[l].astype(BF16), ln1_g[l].reshape(1, D_MODEL), ln1_b[l].reshape(1, D_MODEL),
                              wr, br, rows, 256)
        x = _moe(h2, gate, w_e1f, w_e3f, w_e2f, x1, mod[l], ln2_g[l].reshape(1, D_MODEL),
                 ln2_b[l].reshape(1, D_MODEL), l, rows, tm)
        n_ctx_blocks = rows.ctx_rows // lat_len
        lru_fin = jnp.swapaxes(hf_b[:n_ctx_blocks], 1, 2).reshape(n_ctx, 2, W_B)
        finals.append((sf_a[:n_ctx], lru_fin, cf[:n_ctx], nf[:n_ctx, :, :, 0, :], mf[:n_ctx, :, :, 0, 0]))

    new_delta, new_lru, new_mc, new_mn, new_mm = (jnp.stack([f[i] for f in finals], axis=1) for i in range(5))
    y_prompt = x[:rows.ctx_rows].reshape(n_ctx, ctx_len, D_MODEL)
    y_sample = x[rows.ctx_rows:].reshape(n_lat, lat_len, D_MODEL)
    return (y_prompt, y_sample, new_delta, new_lru, new_mc, new_mn, new_mm)
```

```python
import functools
import math
from typing import NamedTuple

import jax
import jax.numpy as jnp
from jax import lax
from jax.experimental import pallas as pl
from jax.experimental.pallas import tpu as pltpu

F32 = jnp.float32
BF16 = jnp.bfloat16

D_MODEL = 1024
DEPTH = 2
GRID_W = 64
POS_BASE = 10000.0
CONV_W = 4
LN_EPS = 1e-5
RMS_EPS = 1e-6
N_HEADS = 4
HEAD_DIM = 128
W_MIX = N_HEADS * HEAD_DIM
CHUNK = 64
W_B = 512
NB_B = 4
BW_B = W_B // NB_B
LRU_C = 8.0
N_GROUPS = 4
E_PER_GROUP = 8
N_EXPERTS = N_GROUPS * E_PER_GROUP
D_EXPERT = 256
DN_ALPHA = (2 * DEPTH) ** 0.25

LANES = 128
VMEM_LIMIT = 56 * 1024 * 1024

OFF_GATES, OFF_QKV, OFF_QC, OFF_KC, OFF_VC, OFF_Z, OFF_GB, OFF_OC = 0, 3072, 4608, 5120, 5632, 6144, 6656, 7168
N_BIG = OFF_OC + W_MIX
N_F32 = W_B + LANES
SM_BETA, SM_ALPHA, SM_I, SM_F = 0, 8, 16, 24
ROUTE_GROUP_LANE = 8


class Rows(NamedTuple):
    n_ctx: int
    ctx_len: int
    n_lat: int
    lat_len: int

    @property
    def ctx_rows(self):
        return self.n_ctx * self.ctx_len

    @property
    def total(self):
        return self.ctx_rows + self.n_lat * self.lat_len


def _cparams(sem):
    return pltpu.CompilerParams(dimension_semantics=sem, vmem_limit_bytes=VMEM_LIMIT)


def _sigmoid(x):
    return 1.0 / (1.0 + jnp.exp(-x))


def _silu(x):
    return x * _sigmoid(x)


def _softplus(x):
    return jnp.maximum(x, 0.0) + jnp.log1p(jnp.exp(-jnp.abs(x)))


def _ln(x):
    mu = jnp.mean(x, axis=-1, keepdims=True)
    xc = x - mu
    var = jnp.mean(xc * xc, axis=-1, keepdims=True)
    return xc * lax.rsqrt(var + LN_EPS)


def _bdot(a, b):
    return jnp.dot(a.astype(BF16), b.astype(BF16), preferred_element_type=F32)


def _bdot_nt(a, b):
    return lax.dot_general(a.astype(BF16), b.astype(BF16), (((1,), (1,)), ((), ())), preferred_element_type=F32)


def _bdot_tn(a, b):
    return lax.dot_general(a.astype(BF16), b.astype(BF16), (((0,), (0,)), ((), ())), preferred_element_type=F32)


def _split2(a):
    hi = a.astype(BF16)
    lo = (a - hi.astype(F32)).astype(BF16)
    return hi, lo


def _split3(a):
    hi = a.astype(BF16)
    r = a - hi.astype(F32)
    mid = r.astype(BF16)
    lo = (r - mid.astype(F32)).astype(BF16)
    return hi, mid, lo


def _dot3(a, b):
    ah, al = _split2(a)
    bh, bl = _split2(b)
    d = functools.partial(jnp.dot, preferred_element_type=F32)
    return d(ah, bh) + (d(ah, bl) + d(al, bh))


def _dot_exact_lhs(m_bf16, x):
    xh, xm, xl = _split3(x)
    d = functools.partial(jnp.dot, preferred_element_type=F32)
    return d(m_bf16, xh) + (d(m_bf16, xm) + d(m_bf16, xl))


def _cond_index(row0, rows: Rows):
    return jnp.maximum(0, (row0 - rows.ctx_rows + rows.lat_len) // rows.lat_len)


def _prep_kernel(xp_ref, xs_ref, pos_ref, o_ref, *, n_ctx_tiles):
    i = pl.program_id(0)

    @pl.when(i < n_ctx_tiles)
    def _():
        o_ref[...] = xp_ref[...]

    @pl.when(i >= n_ctx_tiles)
    def _():
        o_ref[...] = xs_ref[...] + pos_ref[...]


def _prep(xp2, xs2, pos, rows: Rows, tm):
    n_ctx_tiles = rows.ctx_rows // tm
    n_tiles = rows.total // tm
    pos_tiles = rows.lat_len // tm
    return pl.pallas_call(
        functools.partial(_prep_kernel, n_ctx_tiles=n_ctx_tiles),
        grid=(n_tiles,),
        in_specs=[
            pl.BlockSpec((tm, D_MODEL), lambda i: (jnp.minimum(i, n_ctx_tiles - 1), 0)),
            pl.BlockSpec((tm, D_MODEL), lambda i: (jnp.maximum(i - n_ctx_tiles, 0), 0)),
            pl.BlockSpec((tm, D_MODEL), lambda i: (jnp.maximum(i - n_ctx_tiles, 0) % pos_tiles, 0)),
        ],
        out_specs=pl.BlockSpec((tm, D_MODEL), lambda i: (i, 0)),
        out_shape=jax.ShapeDtypeStruct((rows.total, D_MODEL), F32),
        compiler_params=_cparams(("arbitrary",)),
        name="prep",
    )(xp2, xs2, pos)


def _mod_kernel(c_ref, w_ref, b_ref, o_ref):
    o_ref[0] = _bdot(_silu(c_ref[...]), w_ref[0]) + b_ref[0]


def _modulation(cond8, w_mod, b_mod):
    tn = 1024
    n6 = 6 * D_MODEL
    return pl.pallas_call(
        _mod_kernel,
        grid=(DEPTH, n6 // tn),
        in_specs=[
            pl.BlockSpec((8, D_MODEL), lambda l, n: (0, 0)),
            pl.BlockSpec((1, D_MODEL, tn), lambda l, n: (l, 0, n)),
            pl.BlockSpec((1, 1, tn), lambda l, n: (l, 0, n)),
        ],
        out_specs=pl.BlockSpec((1, 8, tn), lambda l, n: (l, 0, n)),
        out_shape=jax.ShapeDtypeStruct((DEPTH, 8, n6), F32),
        compiler_params=_cparams(("arbitrary", "arbitrary")),
        name="modulation",
    )(cond8, w_mod, b_mod.reshape(DEPTH, 1, n6))


def _proj_kernel(x_ref, mod_ref, wb_ref, wf_ref, ob_ref, of_ref, h_scr):
    n = pl.program_id(1)

    @pl.when(n == 0)
    def _():
        shift1 = mod_ref[0, 0:1, :]
        scale1 = mod_ref[0, 1:2, :]
        h = (_ln(x_ref[...]) * (1.0 + scale1) + shift1).astype(BF16)
        h_scr[...] = h
        of_ref[...] = jnp.dot(h, wf_ref[...], preferred_element_type=F32)

    ob_ref[...] = jnp.dot(h_scr[...], wb_ref[...], preferred_element_type=F32).astype(BF16)


def _projection(x, mod_l, w_big, w_f32, rows: Rows, tm, tn):
    nt = rows.total
    return pl.pallas_call(
        _proj_kernel,
        grid=(nt // tm, N_BIG // tn),
        in_specs=[
            pl.BlockSpec((tm, D_MODEL), lambda i, n: (i, 0)),
            pl.BlockSpec((1, 6, D_MODEL), lambda i, n: (_cond_index(i * tm, rows), 0, 0)),
            pl.BlockSpec((D_MODEL, tn), lambda i, n: (0, n)),
            pl.BlockSpec((D_MODEL, N_F32), lambda i, n: (0, 0)),
        ],
        out_specs=[
            pl.BlockSpec((tm, tn), lambda i, n: (i, n)),
            pl.BlockSpec((tm, N_F32), lambda i, n: (i, 0)),
        ],
        out_shape=[
            jax.ShapeDtypeStruct((nt, N_BIG), BF16),
            jax.ShapeDtypeStruct((nt, N_F32), F32),
        ],
        scratch_shapes=[pltpu.VMEM((tm, D_MODEL), BF16)],
        compiler_params=_cparams(("arbitrary", "arbitrary")),
        name="projection",
    )(x, mod_l, w_big, w_f32)


def _conv_taps(x, w_ref, seq_len):
    n = x.shape[0]
    pos = lax.broadcasted_iota(jnp.int32, x.shape, 0) & (seq_len - 1)
    xm1 = jnp.where(pos >= 1, pltpu.roll(x, 1, 0), 0.0)
    xp1 = jnp.where(pos <= seq_len - 2, pltpu.roll(x, n - 1, 0), 0.0)
    xp2 = jnp.where(pos <= seq_len - 3, pltpu.roll(x, n - 2, 0), 0.0)
    return xm1 * w_ref[0:1, :] + x * w_ref[1:2, :] + xp1 * w_ref[2:3, :] + xp2 * w_ref[3:4, :]


def _conv_a_kernel(x_ref, w_ref, o_ref, *, rows: Rows, br):
    b = pl.program_id(0)
    j = pl.program_id(1)
    seq_len = jnp.where(b * br < rows.ctx_rows, rows.ctx_len, rows.lat_len)
    y = _silu(_conv_taps(x_ref[...].astype(F32), w_ref[0], seq_len))
    nrm = lax.rsqrt(jnp.sum(y * y, axis=-1, keepdims=True) + RMS_EPS)
    fac = jnp.where(j < N_HEADS, nrm * HEAD_DIM ** -0.5, jnp.where(j < 2 * N_HEADS, nrm, 1.0))
    o_ref[...] = (y * fac).astype(BF16)


def _conv_a(proj_big, conv_w, l, rows: Rows, br):
    nt = rows.total
    ncol = 3 * N_HEADS
    return pl.pallas_call(
        functools.partial(_conv_a_kernel, rows=rows, br=br),
        grid=(nt // br, ncol),
        in_specs=[
            pl.BlockSpec((br, LANES), lambda b, j: (b, OFF_QKV // LANES + j)),
            pl.BlockSpec((1, CONV_W, LANES), lambda b, j: (l, 0, j)),
        ],
        out_specs=pl.BlockSpec((br, LANES), lambda b, j: (b, j)),
        out_shape=jax.ShapeDtypeStruct((nt, ncol * LANES), BF16),
        compiler_params=_cparams(("arbitrary", "arbitrary")),
        name="conv_a",
    )(proj_big, conv_w)


UNIT = 1024
NCH = UNIT // CHUNK


def _unit_layout(rows: Rows):
    n_ctx_units = rows.ctx_rows // UNIT
    lat_units = rows.lat_len // UNIT
    return n_ctx_units, lat_units, n_ctx_units + rows.n_lat * lat_units


def _unit_rowblock(u, d, rows: Rows):
    n_ctx_units, lat_units, _ = _unit_layout(rows)
    v = jnp.maximum(u - n_ctx_units, 0)
    b = v // lat_units
    j = v % lat_units
    jj = j + d * (lat_units - 1 - 2 * j)
    return jnp.where(u < n_ctx_units, u, n_ctx_units + b * lat_units + jj)


def _unit_lat_seq(u, rows: Rows):
    n_ctx_units, lat_units, _ = _unit_layout(rows)
    return jnp.maximum(u - n_ctx_units, 0) // lat_units


def _dir_masks(fwd):
    ii = lax.broadcasted_iota(jnp.int32, (CHUNK, CHUNK), 0)
    jj = lax.broadcasted_iota(jnp.int32, (CHUNK, CHUNK), 1)
    s = jnp.where(fwd, ii - jj, jj - ii)
    return s >= 0, s > 0, ii == jj


def _bmm(a, b):
    return jnp.einsum('gik,gkj->gij', a.astype(BF16), b.astype(BF16), preferred_element_type=F32)


def _bmm_nt(a, b):
    return jnp.einsum('gik,gjk->gij', a.astype(BF16), b.astype(BF16), preferred_element_type=F32)


def _chunk_cumsum(x, reverse):
    n = x.shape[0]
    pos = lax.broadcasted_iota(jnp.int32, x.shape, 0) & (CHUNK - 1)
    s = 1
    while s < CHUNK:
        if reverse:
            x = x + jnp.where(pos < CHUNK - s, pltpu.roll(x, n - s, 0), 0.0)
        else:
            x = x + jnp.where(pos >= s, pltpu.roll(x, s, 0), 0.0)
        s *= 2
    return x


def _dir_select(fwd, x):
    return jnp.where(fwd, x, pltpu.roll(x, LANES - N_HEADS, 1))


GROUP_CHUNKS = 4
GROUP_ROWS = GROUP_CHUNKS * CHUNK


def _group_columns(col_arr, row_arr, lane0):
    cols, rws = [], []
    for cc in range(GROUP_CHUNKS):
        for h in range(N_HEADS):
            cols.append(jnp.broadcast_to(col_arr[cc * CHUNK:(cc + 1) * CHUNK, lane0 + h:lane0 + h + 1],
                                         (CHUNK, HEAD_DIM)))
            if row_arr is not None:
                rws.append(jnp.broadcast_to(row_arr[lane0 + h:lane0 + h + 1, cc * CHUNK:(cc + 1) * CHUNK],
                                            (CHUNK, CHUNK)))
    return jnp.stack(cols), (jnp.stack(rws) if rws else None)


def _group_heads(ref, r0, col0):
    return jnp.stack([ref[pl.ds(r0 + cc * CHUNK, CHUNK), col0 + h * HEAD_DIM:col0 + (h + 1) * HEAD_DIM]
                      for cc in range(GROUP_CHUNKS) for h in range(N_HEADS)])


def _delta_kernel(qkv_ref, sm_ref, par_ref, s0_ref, o_ref, sf_ref,
                  s_scr, uv_scr, uk_scr, qk_scr, qd_scr, kd_scr, ge_scr, *, rows: Rows):
    d = pl.program_id(0)
    u = pl.program_id(1)
    n_ctx_units, lat_units, _ = _unit_layout(rows)
    seq_chunks = rows.ctx_len // CHUNK
    seq_per_unit = UNIT // rows.ctx_len
    fwd = d == 0
    is_ctx = u < n_ctx_units
    causal, strict, eye = _dir_masks(fwd)
    eye_f = jnp.where(eye, 1.0, 0.0)
    ii = lax.broadcasted_iota(jnp.int32, (CHUNK, CHUNK), 0)
    jj = lax.broadcasted_iota(jnp.int32, (CHUNK, CHUNK), 1)
    pair_masks = [jnp.logical_and((ii >> (s + 1)) == (jj >> (s + 1)), (ii >> s) != (jj >> s))
                  for s in range(CHUNK.bit_length() - 1)]
    ng = GROUP_CHUNKS * N_HEADS

    def pre(it, carry):
        r0 = pl.multiple_of(it * GROUP_ROWS, GROUP_ROWS)
        g0 = pl.multiple_of(it * ng, ng)
        sm = sm_ref[pl.ds(r0, GROUP_ROWS), :]
        g_all = -jnp.exp(par_ref[0:1, :]) * _softplus(sm + par_ref[1:2, :])
        gcum = _dir_select(fwd, jnp.where(fwd, _chunk_cumsum(g_all, False), _chunk_cumsum(g_all, True)))
        bsel = _dir_select(fwd, _sigmoid(sm))
        gi, grow = _group_columns(gcum, gcum.T, SM_ALPHA)
        beta = jnp.stack([bsel[cc * CHUNK:(cc + 1) * CHUNK, SM_BETA + h:SM_BETA + h + 1]
                          for cc in range(GROUP_CHUNKS) for h in range(N_HEADS)])
        q = _group_heads(qkv_ref, r0, 0)
        k = _group_heads(qkv_ref, r0, W_MIX)
        v = _group_heads(qkv_ref, r0, 2 * W_MIX)
        kf = k.astype(F32)
        decay = jnp.exp(jnp.where(causal, gi[:, :, :CHUNK] - grow, -jnp.inf))
        lmat = jnp.where(strict, beta * _bmm_nt(k, k) * decay, 0.0)
        eg = jnp.exp(gi)
        rhs = jnp.concatenate([v.astype(F32) * beta, kf * (beta * eg)], axis=-1)
        t = eye_f - jnp.where(pair_masks[0], lmat, 0.0)
        for pm in pair_masks[1:]:
            tb = t.astype(BF16)
            t = t - _bmm(_bmm(tb, jnp.where(pm, lmat, 0.0)), tb)
        t0 = t.astype(BF16)
        mh, ml = _split2(eye_f + lmat)
        resid = eye_f - (_bmm(mh, t0) + _bmm(ml, t0))
        t1 = t0.astype(F32) + _bmm(t0, resid)
        uu = _bmm(t1, rhs)
        qk = jnp.where(causal, _bmm_nt(q, k) * decay, 0.0)
        g_end = jnp.where(fwd, gi[:, CHUNK - 1:CHUNK, :], gi[:, 0:1, :])
        uv_scr[pl.ds(g0, ng)] = uu[:, :, :HEAD_DIM]
        uk_scr[pl.ds(g0, ng)] = uu[:, :, HEAD_DIM:].astype(BF16)
        qk_scr[pl.ds(g0, ng)] = qk.astype(BF16)
        qd_scr[pl.ds(g0, ng)] = (q.astype(F32) * eg).astype(BF16)
        kd_scr[pl.ds(g0, ng)] = (kf * jnp.exp(g_end - gi)).astype(BF16)
        ge_scr[pl.ds(g0, ng)] = jnp.broadcast_to(jnp.exp(g_end), (ng, 8, HEAD_DIM))
        return carry

    lax.fori_loop(0, NCH // GROUP_CHUNKS, pre, 0)

    first_lat = jnp.logical_and(u >= n_ctx_units, (u - n_ctx_units) % lat_units == 0)

    @pl.when(first_lat)
    def _():
        s_scr[...] = s0_ref[0, 0, 0]

    @pl.when(u == n_ctx_units)
    def _():
        sf_ref[...] = jnp.zeros(sf_ref.shape, F32)

    def step(n, carry):
        c = jnp.where(fwd, n, NCH - 1 - n)
        r0 = pl.multiple_of(c * CHUNK, CHUNK)
        hs = pl.ds(pl.multiple_of(c * N_HEADS, N_HEADS), N_HEADS)

        @pl.when(jnp.logical_and(is_ctx, n % seq_chunks == 0))
        def _():
            s_scr[...] = jnp.zeros(s_scr.shape, F32)

        s = s_scr[...]
        sb = s.astype(BF16)
        ub = (uv_scr[hs] - _bmm(uk_scr[hs], sb)).astype(BF16)
        o = _bmm(qd_scr[hs], sb) + _bmm(qk_scr[hs], ub)
        kd = kd_scr[hs]
        ge = ge_scr[hs]
        for h in range(N_HEADS):
            s_scr[h] = s[h] * ge[h, 0:1, :] + lax.dot_general(
                kd[h], ub[h], (((0,), (0,)), ((), ())), preferred_element_type=F32)
            o_ref[0, pl.ds(r0, CHUNK), h * HEAD_DIM:(h + 1) * HEAD_DIM] = o[h]

        @pl.when(jnp.logical_and(is_ctx, n % seq_chunks == seq_chunks - 1))
        def _():
            sf_ref[c // seq_chunks, 0] = s_scr[...]

        return carry

    lax.fori_loop(0, NCH, step, 0)


def _delta(qkv_c, proj_f32, par, state, l, rows: Rows):
    n_ctx_units, lat_units, n_units = _unit_layout(rows)
    seq_per_unit = UNIT // rows.ctx_len
    nt = rows.total
    sm_col = W_B // LANES
    kern = functools.partial(_delta_kernel, rows=rows)
    hshape = (NCH * N_HEADS, CHUNK, HEAD_DIM)
    return pl.pallas_call(
        kern,
        grid=(2, n_units),
        in_specs=[
            pl.BlockSpec((UNIT, 3 * W_MIX), lambda d, u: (_unit_rowblock(u, d, rows), 0)),
            pl.BlockSpec((UNIT, LANES), lambda d, u: (_unit_rowblock(u, d, rows), sm_col)),
            pl.BlockSpec((8, LANES), lambda d, u: (0, 0)),
            pl.BlockSpec((1, 1, 1, N_HEADS, HEAD_DIM, HEAD_DIM),
                         lambda d, u: (_unit_lat_seq(u, rows), l, d, 0, 0, 0)),
        ],
        out_specs=[
            pl.BlockSpec((1, UNIT, W_MIX), lambda d, u: (d, _unit_rowblock(u, d, rows), 0)),
            pl.BlockSpec((seq_per_unit, 1, N_HEADS, HEAD_DIM, HEAD_DIM),
                         lambda d, u: (jnp.minimum(u, n_ctx_units), d, 0, 0, 0)),
        ],
        out_shape=[
            jax.ShapeDtypeStruct((2, nt, W_MIX), F32),
            jax.ShapeDtypeStruct((rows.n_ctx + seq_per_unit, 2, N_HEADS, HEAD_DIM, HEAD_DIM), F32),
        ],
        scratch_shapes=[
            pltpu.VMEM((N_HEADS, HEAD_DIM, HEAD_DIM), F32),
            pltpu.VMEM(hshape, F32),
            pltpu.VMEM(hshape, BF16),
            pltpu.VMEM((NCH * N_HEADS, CHUNK, CHUNK), BF16),
            pltpu.VMEM(hshape, BF16),
            pltpu.VMEM(hshape, BF16),
            pltpu.VMEM((NCH * N_HEADS, 8, HEAD_DIM), F32),
        ],
        compiler_params=_cparams(("arbitrary", "arbitrary")),
        name="delta",
    )(qkv_c, proj_f32, par, state)


def _mlstm_kernel(big_ref, sm_ref, par_ref, c0_ref, n0_ref, m0_ref, o_ref, cf_ref, nf_ref, mf_ref,
                  c_scr, n_scr, m_scr, dm_scr, qk_scr, fi_scr, mi_scr, de_scr, me_scr, *, rows: Rows):
    d = pl.program_id(0)
    u = pl.program_id(1)
    n_ctx_units, lat_units, _ = _unit_layout(rows)
    seq_chunks = rows.ctx_len // CHUNK
    fwd = d == 0
    is_ctx = u < n_ctx_units
    causal, _, _ = _dir_masks(fwd)
    qo, ko, vo = 0, W_MIX, 2 * W_MIX
    ng = GROUP_CHUNKS * N_HEADS

    def pre(it, carry):
        r0 = pl.multiple_of(it * GROUP_ROWS, GROUP_ROWS)
        gs = pl.ds(pl.multiple_of(it * ng, ng), ng)
        sm = sm_ref[pl.ds(r0, GROUP_ROWS), :]
        f_all = -_softplus(-(sm + par_ref[3:4, :]))
        fcum = _dir_select(fwd, jnp.where(fwd, _chunk_cumsum(f_all, False), _chunk_cumsum(f_all, True)))
        i_al = pltpu.roll(_dir_select(fwd, sm + par_ref[2:3, :]), SM_F - SM_I, 1)
        fi, hrow = _group_columns(fcum, (fcum - i_al).T, SM_F)
        it_b, _ = _group_columns(i_al, None, SM_F)
        q = _group_heads(big_ref, r0, qo)
        ks = _group_heads(big_ref, r0, ko).astype(F32) * HEAD_DIM ** -0.5
        dmat = jnp.where(causal, fi[:, :, :CHUNK] - hrow, -jnp.inf)
        f_end = jnp.where(fwd, fi[:, CHUNK - 1:CHUNK, :], fi[:, 0:1, :])
        d_end = f_end - fi + it_b
        dm_scr[gs] = dmat
        qk_scr[gs] = _bmm_nt(q, ks)
        fi_scr[gs] = fi
        mi_scr[gs] = jnp.broadcast_to(jnp.max(dmat, axis=-1, keepdims=True), (ng, CHUNK, HEAD_DIM))
        de_scr[gs] = d_end
        me_scr[gs] = jnp.broadcast_to(jnp.max(d_end, axis=1, keepdims=True), (ng, 8, HEAD_DIM))
        return carry

    lax.fori_loop(0, NCH // GROUP_CHUNKS, pre, 0)

    first_lat = jnp.logical_and(u >= n_ctx_units, (u - n_ctx_units) % lat_units == 0)

    @pl.when(first_lat)
    def _():
        c_scr[...] = c0_ref[0, 0, 0]
        for h in range(N_HEADS):
            n_scr[h] = jnp.broadcast_to(n0_ref[0, 0, 0, h:h + 1, :], (8, HEAD_DIM))
            m_scr[h] = jnp.broadcast_to(m0_ref[0, 0, 0, h:h + 1, :], (8, HEAD_DIM))

    @pl.when(u == n_ctx_units)
    def _():
        cf_ref[...] = jnp.zeros(cf_ref.shape, F32)
        nf_ref[...] = jnp.zeros(nf_ref.shape, F32)
        mf_ref[...] = jnp.zeros(mf_ref.shape, F32)

    def step(n, carry):
        c = jnp.where(fwd, n, NCH - 1 - n)
        r0 = pl.multiple_of(c * CHUNK, CHUNK)

        @pl.when(jnp.logical_and(is_ctx, n % seq_chunks == 0))
        def _():
            c_scr[...] = jnp.zeros(c_scr.shape, F32)
            n_scr[...] = jnp.zeros(n_scr.shape, F32)
            m_scr[...] = jnp.zeros(m_scr.shape, F32)

        hs = pl.ds(pl.multiple_of(c * N_HEADS, N_HEADS), N_HEADS)
        heads = lambda off: jnp.stack([big_ref[pl.ds(r0, CHUNK), off + h * HEAD_DIM:off + (h + 1) * HEAD_DIM]
                                       for h in range(N_HEADS)])
        q, v = heads(qo), heads(vo)
        ks = heads(ko).astype(F32) * HEAD_DIM ** -0.5
        cs = c_scr[...]
        ns = n_scr[...][:, 0:1, :]
        ms = m_scr[...][:, 0:1, :]
        fi = fi_scr[hs]
        inter = fi + ms
        mt = jnp.maximum(inter, mi_scr[hs])
        w_int = jnp.exp(inter - mt)
        pm = jnp.exp(dm_scr[hs] - mt[:, :, :CHUNK]) * qk_scr[hs]
        num = w_int * _bmm(q, cs) + _bmm(pm, v)
        qn = jnp.sum(q.astype(F32) * ns, axis=-1, keepdims=True)
        den = w_int[:, :, 0:1] * qn + jnp.sum(pm, axis=-1, keepdims=True)
        hh = num / jnp.maximum(jnp.abs(den), jnp.exp(-mt[:, :, 0:1]))
        f_end = jnp.where(fwd, fi[:, CHUNK - 1:CHUNK, :], fi[:, 0:1, :])
        inter_end = f_end + ms
        m_new = jnp.maximum(inter_end, me_scr[hs][:, 0:1, :])
        w_tok = jnp.exp(de_scr[hs] - m_new)
        s_int = jnp.exp(inter_end - m_new)
        kw = ks * w_tok
        n_new = ns * s_int + jnp.sum(kw, axis=1, keepdims=True)
        n_scr[...] = jnp.broadcast_to(n_new, n_scr.shape)
        m_scr[...] = jnp.broadcast_to(m_new, m_scr.shape)
        for h in range(N_HEADS):
            c_scr[h] = cs[h] * s_int[h, :, 0:1] + _bdot_tn(kw[h], v[h])
            o_ref[0, pl.ds(r0, CHUNK), h * HEAD_DIM:(h + 1) * HEAD_DIM] = hh[h]

        @pl.when(jnp.logical_and(is_ctx, n % seq_chunks == seq_chunks - 1))
        def _():
            sq = c // seq_chunks
            cf_ref[sq, 0] = c_scr[...]
            nf_ref[sq, 0] = n_scr[...]
            mf_ref[sq, 0] = m_scr[...]

        return carry

    lax.fori_loop(0, NCH, step, 0)


def _mlstm(proj_big, proj_f32, par, st_c, st_n8, st_m8, l, rows: Rows):
    n_ctx_units, lat_units, n_units = _unit_layout(rows)
    seq_per_unit = UNIT // rows.ctx_len
    nt = rows.total
    sm_col = W_B // LANES
    qkv_blk = OFF_QC // (3 * W_MIX)
    assert OFF_QC % (3 * W_MIX) == 0
    kern = functools.partial(_mlstm_kernel, rows=rows)
    hshape = (NCH * N_HEADS, CHUNK, HEAD_DIM)
    n_out = rows.n_ctx + seq_per_unit
    return pl.pallas_call(
        kern,
        grid=(2, n_units),
        in_specs=[
            pl.BlockSpec((UNIT, 3 * W_MIX), lambda d, u: (_unit_rowblock(u, d, rows), qkv_blk)),
            pl.BlockSpec((UNIT, LANES), lambda d, u: (_unit_rowblock(u, d, rows), sm_col)),
            pl.BlockSpec((8, LANES), lambda d, u: (0, 0)),
            pl.BlockSpec((1, 1, 1, N_HEADS, HEAD_DIM, HEAD_DIM),
                         lambda d, u: (_unit_lat_seq(u, rows), l, d, 0, 0, 0)),
            pl.BlockSpec((1, 1, 1, N_HEADS, HEAD_DIM), lambda d, u: (_unit_lat_seq(u, rows), l, d, 0, 0)),
            pl.BlockSpec((1, 1, 1, N_HEADS, HEAD_DIM), lambda d, u: (_unit_lat_seq(u, rows), l, d, 0, 0)),
        ],
        out_specs=[
            pl.BlockSpec((1, UNIT, W_MIX), lambda d, u: (d, _unit_rowblock(u, d, rows), 0)),
            pl.BlockSpec((seq_per_unit, 1, N_HEADS, HEAD_DIM, HEAD_DIM),
                         lambda d, u: (jnp.minimum(u, n_ctx_units), d, 0, 0, 0)),
            pl.BlockSpec((seq_per_unit, 1, N_HEADS, 8, HEAD_DIM),
                         lambda d, u: (jnp.minimum(u, n_ctx_units), d, 0, 0, 0)),
            pl.BlockSpec((seq_per_unit, 1, N_HEADS, 8, HEAD_DIM),
                         lambda d, u: (jnp.minimum(u, n_ctx_units), d, 0, 0, 0)),
        ],
        out_shape=[
            jax.ShapeDtypeStruct((2, nt, W_MIX), F32),
            jax.ShapeDtypeStruct((n_out, 2, N_HEADS, HEAD_DIM, HEAD_DIM), F32),
            jax.ShapeDtypeStruct((n_out, 2, N_HEADS, 8, HEAD_DIM), F32),
            jax.ShapeDtypeStruct((n_out, 2, N_HEADS, 8, HEAD_DIM), F32),
        ],
        scratch_shapes=[
            pltpu.VMEM((N_HEADS, HEAD_DIM, HEAD_DIM), F32),
            pltpu.VMEM((N_HEADS, 8, HEAD_DIM), F32),
            pltpu.VMEM((N_HEADS, 8, HEAD_DIM), F32),
            pltpu.VMEM((NCH * N_HEADS, CHUNK, CHUNK), F32),
            pltpu.VMEM((NCH * N_HEADS, CHUNK, CHUNK), F32),
            pltpu.VMEM(hshape, F32),
            pltpu.VMEM(hshape, F32),
            pltpu.VMEM(hshape, F32),
            pltpu.VMEM((NCH * N_HEADS, 8, HEAD_DIM), F32),
        ],
        compiler_params=_cparams(("arbitrary", "arbitrary")),
        name="mlstm",
    )(proj_big, proj_f32, par, st_c, st_n8, st_m8)


def _gelu_tanh(x):
    return 0.5 * x * (1.0 + jnp.tanh(math.sqrt(2.0 / math.pi) * (x + 0.044715 * (x * x * x))))


def _tile_scan(a, b, reverse):
    n = a.shape[0]
    row = lax.broadcasted_iota(jnp.int32, a.shape, 0)
    s = 1
    while s < n:
        if reverse:
            ok = row < n - s
            a_sh = jnp.where(ok, pltpu.roll(a, n - s, 0), 1.0)
            b_sh = jnp.where(ok, pltpu.roll(b, n - s, 0), 0.0)
        else:
            ok = row >= s
            a_sh = jnp.where(ok, pltpu.roll(a, s, 0), 1.0)
            b_sh = jnp.where(ok, pltpu.roll(b, s, 0), 0.0)
        b = a * b_sh + b
        a = a * a_sh
        s *= 2
    return a, b


def _lru_kernel(x_ref, g_ref, cw_ref, cb_ref, wa_ref, ba_ref, wx_ref, bx_ref, lam_ref, h0_ref,
                y_ref, hf_ref, xc_scr, hfw_scr, *, rows: Rows, br):
    blk = pl.program_id(0)
    is_ctx = blk * br < rows.ctx_rows
    ts = rows.ctx_len
    n_tiles = br // ts
    seq_len = jnp.where(is_ctx, rows.ctx_len, rows.lat_len)
    xc_scr[...] = _conv_taps(x_ref[...], cw_ref[0], seq_len) + cb_ref[0]

    for dd in range(2):
        reverse = dd == 1
        sp_lam = _softplus(-lam_ref[0, dd])

        def tile(n, carry, dd=dd, reverse=reverse, sp_lam=sp_lam):
            t = (n_tiles - 1 - n) if reverse else n
            r0 = pl.multiple_of(t * ts, ts)
            xc = xc_scr[pl.ds(r0, ts), :]
            r = _sigmoid(_bdot(xc, wa_ref[0, dd, 0]) + ba_ref[0, dd])
            gi = _sigmoid(_bdot(xc, wx_ref[0, dd, 0]) + bx_ref[0, dd])
            log_a = -LRU_C * r * sp_lam
            a = jnp.exp(log_a)
            b = jnp.sqrt(-jnp.tanh(log_a) * (a * a + 1.0)) * (gi * xc)
            a_cum, h = _tile_scan(a, b, reverse)
            carry = jnp.where(is_ctx, 0.0, carry)
            h = h + a_cum * carry
            last = h[0:1, :] if reverse else h[ts - 1:ts, :]
            hf_ref[0, dd, pl.ds(t, 1), :] = last
            if reverse:
                g = g_ref[pl.ds(r0, ts), :].astype(F32)
                y_ref[pl.ds(r0, ts), :] = ((hfw_scr[pl.ds(r0, ts), :] + h) * _gelu_tanh(g)).astype(BF16)
            else:
                hfw_scr[pl.ds(r0, ts), :] = h
            return last

        lax.fori_loop(0, n_tiles, tile, h0_ref[0, 0, dd])


def _lru(proj_f32, proj_big, conv_w, conv_b, wa, ba, wx, bx, lam, h0, l, rows: Rows):
    br = rows.lat_len
    nt = rows.total
    n_blocks = nt // br
    n_ctx_blocks = rows.ctx_rows // br
    n_tiles = br // rows.ctx_len
    gcol = OFF_GB // BW_B

    def lat_seq(b):
        return jnp.maximum(b - n_ctx_blocks, 0)

    vec = lambda a: a.reshape(DEPTH, 2, 1, W_B)
    return pl.pallas_call(
        functools.partial(_lru_kernel, rows=rows, br=br),
        grid=(n_blocks, NB_B),
        in_specs=[
            pl.BlockSpec((br, BW_B), lambda b, j: (b, j)),
            pl.BlockSpec((br, BW_B), lambda b, j: (b, gcol + j)),
            pl.BlockSpec((1, CONV_W, BW_B), lambda b, j: (l, 0, j)),
            pl.BlockSpec((1, 1, BW_B), lambda b, j: (l, 0, j)),
            pl.BlockSpec((1, 2, 1, BW_B, BW_B), lambda b, j: (l, 0, j, 0, 0)),
            pl.BlockSpec((1, 2, 1, BW_B), lambda b, j: (l, 0, 0, j)),
            pl.BlockSpec((1, 2, 1, BW_B, BW_B), lambda b, j: (l, 0, j, 0, 0)),
            pl.BlockSpec((1, 2, 1, BW_B), lambda b, j: (l, 0, 0, j)),
            pl.BlockSpec((1, 2, 1, BW_B), lambda b, j: (l, 0, 0, j)),
            pl.BlockSpec((1, 1, 2, 1, BW_B), lambda b, j: (lat_seq(b), l, 0, 0, j)),
        ],
        out_specs=[
            pl.BlockSpec((br, BW_B), lambda b, j: (b, j)),
            pl.BlockSpec((1, 2, n_tiles, BW_B), lambda b, j: (b, 0, 0, j)),
        ],
        out_shape=[
            jax.ShapeDtypeStruct((nt, W_B), BF16),
            jax.ShapeDtypeStruct((n_blocks, 2, n_tiles, W_B), F32),
        ],
        scratch_shapes=[pltpu.VMEM((br, BW_B), F32), pltpu.VMEM((br, BW_B), F32)],
        compiler_params=_cparams(("arbitrary", "arbitrary")),
        name="lru",
    )(proj_f32, proj_big, conv_w, conv_b.reshape(DEPTH, 1, W_B), wa, vec(ba), wx, vec(bx), vec(lam),
      h0.reshape(h0.shape[0], DEPTH, 2, 1, W_B))


def _route(lg, le):
    lane = lax.broadcasted_iota(jnp.int32, lg.shape, 1)
    neg = -jnp.inf
    lgm = jnp.where(lane < N_GROUPS, lg, neg)
    gmax = jnp.max(lgm, axis=-1, keepdims=True)
    p_grp = 1.0 / jnp.sum(jnp.exp(lgm - gmax), axis=-1, keepdims=True)
    g_sel = jnp.min(jnp.where(lgm == gmax, lane, LANES), axis=-1, keepdims=True)
    in_grp = jnp.logical_and(lane >= g_sel * E_PER_GROUP, lane < (g_sel + 1) * E_PER_GROUP)
    lem = jnp.where(in_grp, le, neg)
    v1 = jnp.max(lem, axis=-1, keepdims=True)
    i1 = jnp.min(jnp.where(lem == v1, lane, LANES), axis=-1, keepdims=True)
    lem2 = jnp.where(lane == i1, neg, lem)
    v2 = jnp.max(lem2, axis=-1, keepdims=True)
    i2 = jnp.min(jnp.where(lem2 == v2, lane, LANES), axis=-1, keepdims=True)
    e2 = jnp.exp(v2 - v1)
    w1 = p_grp / (1.0 + e2)
    w2 = p_grp * e2 / (1.0 + e2)
    gate = jnp.where(lane == i1, w1, 0.0) + jnp.where(lane == i2, w2, 0.0)
    local = gate
    for g in range(1, N_GROUPS):
        local = jnp.where(g_sel == g, pltpu.roll(gate, LANES - g * E_PER_GROUP, 1), local)
    return jnp.where(lane < E_PER_GROUP, local, jnp.where(lane == ROUTE_GROUP_LANE, g_sel.astype(F32), 0.0))


def _merge_kernel(x_ref, mod_ref, oa_ref, z_ref, yb_ref, hc_ref, oc_ref, gt_ref, dn_ref, mn_ref,
                  wpa_ref, wpb_ref, wpc_ref, wout_ref, g1_ref, b1_ref, wr_ref, br_ref,
                  x1_ref, h2_ref, gate_ref):
    oa = oa_ref[0] + oa_ref[1]
    hc = hc_ref[0] + hc_ref[1]
    ya, yc = [], []
    for h in range(N_HEADS):
        sl = slice(h * HEAD_DIM, (h + 1) * HEAD_DIM)
        o_h = oa[:, sl]
        o_h = o_h * lax.rsqrt(jnp.mean(o_h * o_h, axis=-1, keepdims=True) + RMS_EPS) * dn_ref[...]
        ya.append((o_h * _silu(z_ref[:, sl].astype(F32))).astype(BF16))
        c_h = _ln(hc[:, sl]) * mn_ref[:, sl]
        yc.append((_sigmoid(oc_ref[:, sl].astype(F32)) * c_h).astype(BF16))
    ya = jnp.concatenate(ya, axis=-1)
    yc = jnp.concatenate(yc, axis=-1)
    ga = _sigmoid(gt_ref[:, 0:D_MODEL].astype(F32))
    gb = _sigmoid(gt_ref[:, D_MODEL:2 * D_MODEL].astype(F32))
    gc = _sigmoid(gt_ref[:, 2 * D_MODEL:3 * D_MODEL].astype(F32))
    d = functools.partial(jnp.dot, preferred_element_type=F32)
    merged = ga * d(ya, wpa_ref[...]) + gb * d(yb_ref[...], wpb_ref[...]) + gc * d(yc, wpc_ref[...])
    mixed = d(merged.astype(BF16), wout_ref[...])
    gate1 = mod_ref[0, 2:3, :]
    shift2 = mod_ref[0, 3:4, :]
    scale2 = mod_ref[0, 4:5, :]
    x1 = _ln(DN_ALPHA * x_ref[...] + gate1 * mixed) * g1_ref[...] + b1_ref[...]
    x1_ref[...] = x1
    h2 = _ln(x1) * (1.0 + scale2) + shift2
    h2_ref[...] = h2.astype(BF16)
    logits = _dot3(h2, wr_ref[...]) + br_ref[...]
    lg = logits
    le = pltpu.roll(logits, LANES - 64, 1)
    gate_ref[...] = _route(lg, le)


def _merge(x, mod_l, o_a, proj_big, y_b, h_c, dn, mn, wpa, wpb, wpc, wout, g1, b1, wr, br, rows: Rows, tm):
    nt = rows.total
    row = lambda i: (i, 0)
    const = lambda i: (0, 0)
    cw = W_MIX
    return pl.pallas_call(
        _merge_kernel,
        grid=(nt // tm,),
        in_specs=[
            pl.BlockSpec((tm, D_MODEL), row),
            pl.BlockSpec((1, 6, D_MODEL), lambda i: (_cond_index(i * tm, rows), 0, 0)),
            pl.BlockSpec((2, tm, cw), lambda i: (0, i, 0)),
            pl.BlockSpec((tm, cw), lambda i: (i, OFF_Z // cw)),
            pl.BlockSpec((tm, cw), row),
            pl.BlockSpec((2, tm, cw), lambda i: (0, i, 0)),
            pl.BlockSpec((tm, cw), lambda i: (i, OFF_OC // cw)),
            pl.BlockSpec((tm, 3 * D_MODEL), lambda i: (i, OFF_GATES // (3 * D_MODEL))),
            pl.BlockSpec((1, HEAD_DIM), const),
            pl.BlockSpec((1, cw), const),
            pl.BlockSpec((cw, D_MODEL), const),
            pl.BlockSpec((cw, D_MODEL), const),
            pl.BlockSpec((cw, D_MODEL), const),
            pl.BlockSpec((D_MODEL, D_MODEL), const),
            pl.BlockSpec((1, D_MODEL), const),
            pl.BlockSpec((1, D_MODEL), const),
            pl.BlockSpec((D_MODEL, LANES), const),
            pl.BlockSpec((1, LANES), const),
        ],
        out_specs=[
            pl.BlockSpec((tm, D_MODEL), row),
            pl.BlockSpec((tm, D_MODEL), row),
            pl.BlockSpec((tm, LANES), row),
        ],
        out_shape=[
            jax.ShapeDtypeStruct((nt, D_MODEL), F32),
            jax.ShapeDtypeStruct((nt, D_MODEL), BF16),
            jax.ShapeDtypeStruct((nt, LANES), F32),
        ],
        compiler_params=_cparams(("arbitrary",)),
        name="merge",
    )(x, mod_l, o_a, proj_big, y_b, h_c, proj_big, proj_big, dn, mn, wpa, wpb, wpc, wout, g1, b1, wr, br)


MOE_TILE = 512


class MoePlan(NamedTuple):
    dest_row: jax.Array
    dest_col: jax.Array
    tile_group: jax.Array
    by_tile: tuple
    by_block: tuple


def _pair_list(mask, n_pairs, minor):
    flat = mask.reshape(-1)
    cnt = jnp.sum(flat.astype(jnp.int32))
    idx = jnp.nonzero(flat, size=n_pairs, fill_value=0)[0].astype(jnp.int32)
    pos = jnp.arange(n_pairs, dtype=jnp.int32)
    valid = pos < cnt
    idx = jnp.where(valid, idx, idx[jnp.maximum(cnt - 1, 0)])
    major, mnr = idx // minor, idx % minor
    prev = jnp.concatenate([jnp.full((1,), -1, jnp.int32), major[:-1]])
    nxt = jnp.concatenate([major[1:], jnp.full((1,), -1, jnp.int32)])
    first = jnp.logical_and(valid, major != prev)
    last = jnp.logical_and(valid, jnp.logical_or(major != nxt, pos == cnt - 1))
    i32 = lambda a: a.astype(jnp.int32)
    return major, mnr, i32(first), i32(last), i32(valid)


def _moe_plan(g_sel, nt):
    n_blocks = nt // MOE_TILE
    n_tiles = n_blocks + N_GROUPS
    n_pairs = n_tiles + N_GROUPS * n_blocks
    oh = (g_sel[:, None] == jnp.arange(N_GROUPS, dtype=jnp.int32)[None, :]).astype(jnp.int32)
    counts = jnp.sum(oh, axis=0)
    tiles_g = (counts + MOE_TILE - 1) // MOE_TILE
    tile_end = jnp.cumsum(tiles_g)
    tile_start = tile_end - tiles_g
    rank = jnp.sum((jnp.cumsum(oh, axis=0) - oh) * oh, axis=1)
    dest = jnp.sum(oh * tile_start[None, :], axis=1) * MOE_TILE + rank
    tile_ids = jnp.arange(n_tiles, dtype=jnp.int32)
    tile_group = jnp.minimum(jnp.sum((tile_ids[:, None] >= tile_end[None, :]).astype(jnp.int32), axis=1),
                             N_GROUPS - 1)
    t_oh = ((dest // MOE_TILE)[:, None] == tile_ids[None, :]).astype(F32)
    b_oh = ((jnp.arange(nt, dtype=jnp.int32) // MOE_TILE)[:, None]
            == jnp.arange(n_blocks, dtype=jnp.int32)[None, :]).astype(F32)
    mask = jnp.einsum('tj,tb->jb', t_oh, b_oh) > 0.5
    mask = mask.at[:, 0].set(jnp.logical_or(mask[:, 0], tile_ids >= tile_end[-1]))
    tj, tb, tf, tl, tv = _pair_list(mask, n_pairs, n_blocks)
    cb, cj, cf, cl, cv = _pair_list(mask.T, n_pairs, n_tiles)
    return MoePlan(dest.reshape(n_blocks, 1, MOE_TILE),
                   jnp.broadcast_to(dest[:, None], (nt, LANES)),
                   tile_group, (tj, tb, tf, tl, tv), (cj, cb, cf, cl, cv))


def _moe_experts_kernel(pj_ref, pb_ref, pf_ref, pl_ref, pv_ref, tg_ref,
                        h_ref, dest_ref, route_ref, w1_ref, w3_ref, w2_ref, y_ref, x_scr, g_scr):
    p = pl.program_id(0)

    @pl.when(pf_ref[p] == 1)
    def _():
        x_scr[...] = jnp.zeros(x_scr.shape, F32)
        g_scr[...] = jnp.zeros(g_scr.shape, F32)

    @pl.when(pv_ref[p] == 1)
    def _():
        row = lax.broadcasted_iota(jnp.int32, (MOE_TILE, MOE_TILE), 0) + pj_ref[p] * MOE_TILE
        sel = jnp.where(dest_ref[0] == row, 1.0, 0.0).astype(BF16)
        d = functools.partial(jnp.dot, preferred_element_type=F32)
        x_scr[...] += d(sel, h_ref[...])
        gh, gm, gl = _split3(route_ref[...])
        g_scr[...] += d(sel, gh) + (d(sel, gm) + d(sel, gl))

    @pl.when(pl_ref[p] == 1)
    def _():
        x = x_scr[...].astype(BF16)
        gate = g_scr[...]
        d = functools.partial(jnp.dot, preferred_element_type=F32)
        acc = jnp.zeros((MOE_TILE, D_MODEL), F32)
        for e in range(E_PER_GROUP):
            hid = _silu(d(x, w1_ref[0, e])) * d(x, w3_ref[0, e]) * gate[:, e:e + 1]
            acc = acc + d(hid.astype(BF16), w2_ref[0, e])
        y_ref[...] = acc.astype(BF16)


def _moe_experts(h2, route, plan: MoePlan, w1, w3, w2, l, nt):
    n_blocks = nt // MOE_TILE
    n_tiles = n_blocks + N_GROUPS
    tj, tb, tf, tl, tv = plan.by_tile
    n_pairs = tj.shape[0]
    wmap = lambda p, pj, pb, pf, pl_, pv, tg: (l * N_GROUPS + tg[pj[p]], 0, 0, 0)
    grid_spec = pltpu.PrefetchScalarGridSpec(
        num_scalar_prefetch=6,
        grid=(n_pairs,),
        in_specs=[
            pl.BlockSpec((MOE_TILE, D_MODEL), lambda p, pj, pb, *_: (pb[p], 0)),
            pl.BlockSpec((1, 1, MOE_TILE), lambda p, pj, pb, *_: (pb[p], 0, 0)),
            pl.BlockSpec((MOE_TILE, LANES), lambda p, pj, pb, *_: (pb[p], 0)),
            pl.BlockSpec((1, E_PER_GROUP, D_MODEL, D_EXPERT), wmap),
            pl.BlockSpec((1, E_PER_GROUP, D_MODEL, D_EXPERT), wmap),
            pl.BlockSpec((1, E_PER_GROUP, D_EXPERT, D_MODEL), wmap),
        ],
        out_specs=pl.BlockSpec((MOE_TILE, D_MODEL), lambda p, pj, *_: (pj[p], 0)),
        scratch_shapes=[pltpu.VMEM((MOE_TILE, D_MODEL), F32), pltpu.VMEM((MOE_TILE, LANES), F32)],
    )
    return pl.pallas_call(
        _moe_experts_kernel,
        grid_spec=grid_spec,
        out_shape=jax.ShapeDtypeStruct((n_tiles * MOE_TILE, D_MODEL), BF16),
        compiler_params=_cparams(("arbitrary",)),
        name="moe_experts",
    )(tj, tb, tf, tl, tv, plan.tile_group, h2, plan.dest_row, route, w1, w3, w2)


def _moe_combine_kernel(cj_ref, cb_ref, cf_ref, cl_ref, cv_ref,
                        y_ref, dest_ref, x1_ref, mod_ref, g2_ref, b2_ref, o_ref, acc_scr):
    p = pl.program_id(0)

    @pl.when(cf_ref[p] == 1)
    def _():
        acc_scr[...] = jnp.zeros(acc_scr.shape, F32)

    @pl.when(cv_ref[p] == 1)
    def _():
        col = lax.broadcasted_iota(jnp.int32, (MOE_TILE, MOE_TILE), 1) + cj_ref[p] * MOE_TILE
        sel = jnp.where(dest_ref[:, 0:1] == col, 1.0, 0.0).astype(BF16)
        acc_scr[...] += jnp.dot(sel, y_ref[...], preferred_element_type=F32)

    @pl.when(cl_ref[p] == 1)
    def _():
        gate2 = mod_ref[0, 5:6, :]
        o_ref[...] = _ln(DN_ALPHA * x1_ref[...] + gate2 * acc_scr[...]) * g2_ref[...] + b2_ref[...]


def _moe_combine(y, plan: MoePlan, x1, mod_l, g2, b2, rows: Rows):
    nt = rows.total
    cj, cb, cf, cl, cv = plan.by_block
    n_pairs = cj.shape[0]
    blk = lambda p, cj, cb, *_: (cb[p], 0)
    const = lambda p, *_: (0, 0)
    grid_spec = pltpu.PrefetchScalarGridSpec(
        num_scalar_prefetch=5,
        grid=(n_pairs,),
        in_specs=[
            pl.BlockSpec((MOE_TILE, D_MODEL), lambda p, cj, *_: (cj[p], 0)),
            pl.BlockSpec((MOE_TILE, LANES), blk),
            pl.BlockSpec((MOE_TILE, D_MODEL), blk),
            pl.BlockSpec((1, 6, D_MODEL), lambda p, cj, cb, *_: (_cond_index(cb[p] * MOE_TILE, rows), 0, 0)),
            pl.BlockSpec((1, D_MODEL), const),
            pl.BlockSpec((1, D_MODEL), const),
        ],
        out_specs=pl.BlockSpec((MOE_TILE, D_MODEL), blk),
        scratch_shapes=[pltpu.VMEM((MOE_TILE, D_MODEL), F32)],
    )
    return pl.pallas_call(
        _moe_combine_kernel,
        grid_spec=grid_spec,
        out_shape=jax.ShapeDtypeStruct((nt, D_MODEL), F32),
        compiler_params=_cparams(("arbitrary",)),
        name="moe_combine",
    )(cj, cb, cf, cl, cv, y, plan.dest_col, x1, mod_l, g2, b2)


def _grid_pos_embed(n_tokens):
    rows = n_tokens // GRID_W
    r, col = jnp.meshgrid(jnp.arange(rows, dtype=F32), jnp.arange(GRID_W, dtype=F32), indexing='ij')
    quarter = D_MODEL // 4
    freqs = jnp.exp(-math.log(POS_BASE) * jnp.arange(quarter, dtype=F32) / quarter)
    ar = r.reshape(-1, 1) * freqs
    ac = col.reshape(-1, 1) * freqs
    return jnp.concatenate([jnp.sin(ar), jnp.cos(ar), jnp.sin(ac), jnp.cos(ac)], axis=-1)


def _pack_w_in(w_in_l):
    sizes = (3 * W_MIX, W_B, W_MIX, 8, 8, W_B, W_MIX, W_MIX, W_MIX, W_MIX, 8, 8, 3 * D_MODEL)
    parts, start = [], 0
    for s in sizes:
        parts.append(w_in_l[:, start:start + s])
        start += s
    qkv_a, x_b, z_a, beta, alpha, g_b, q_c, k_c, v_c, o_c, i_c, f_c, gates = parts
    big = jnp.concatenate([gates, qkv_a, q_c, k_c, v_c, z_a, g_b, o_c], axis=1).astype(BF16)
    pad = jnp.zeros((D_MODEL, LANES - 32), F32)
    small = jnp.concatenate([x_b, beta, alpha, i_c, f_c, pad], axis=1).astype(BF16)
    return big, small


def _lane_row(vals, off):
    out = jnp.zeros((8, LANES), F32)
    for r, (v, o) in enumerate(zip(vals, off)):
        out = out.at[r, o:o + 8].set(v.reshape(-1))
    return out


def kernel(x_prompt, x_sample, state_delta, state_lru, state_mlstm_C, state_mlstm_n, state_mlstm_m, c, c_ctx,
           w_mod, b_mod, w_in, conv_a, delta_a_log, delta_dt_bias, delta_norm, conv_b_w, conv_b_b,
           lru_wa, lru_ba, lru_wx, lru_bx, lru_lambda, mlstm_bi, mlstm_bf, mlstm_norm,
           w_pa, w_pb, w_pc, w_out, ln1_g, ln1_b, ln2_g, ln2_b, w_rg, b_rg, w_re, b_re, w_e1, w_e3, w_e2):
    n_ctx, ctx_len, _ = x_prompt.shape
    n_lat, lat_len, _ = x_sample.shape
    rows = Rows(n_ctx, ctx_len, n_lat, lat_len)
    assert rows.ctx_rows % UNIT == 0 and lat_len % UNIT == 0 and UNIT % ctx_len == 0
    assert rows.ctx_rows % lat_len == 0 and ctx_len % CHUNK == 0 and n_lat <= 7
    assert rows.ctx_rows % MOE_TILE == 0 and lat_len % MOE_TILE == 0

    tm = min(1024, lat_len)
    pos = _grid_pos_embed(lat_len)
    x = _prep(x_prompt.reshape(rows.ctx_rows, D_MODEL), x_sample.reshape(n_lat * lat_len, D_MODEL), pos, rows, tm)

    cond8 = jnp.zeros((8, D_MODEL), F32).at[0].set(c_ctx).at[1:1 + n_lat].set(c)
    mod = _modulation(cond8, w_mod, b_mod).reshape(DEPTH, 8, 6, D_MODEL)

    m_bcast = jnp.broadcast_to(state_mlstm_m[..., None], state_mlstm_m.shape + (HEAD_DIM,))
    w_e1g = w_e1.astype(BF16).reshape(DEPTH * N_GROUPS, E_PER_GROUP, D_MODEL, D_EXPERT)
    w_e3g = w_e3.astype(BF16).reshape(DEPTH * N_GROUPS, E_PER_GROUP, D_MODEL, D_EXPERT)
    w_e2g = w_e2.astype(BF16).reshape(DEPTH * N_GROUPS, E_PER_GROUP, D_EXPERT, D_MODEL)

    finals = []
    for l in range(DEPTH):
        w_big, w_small = _pack_w_in(w_in[l])
        par = _lane_row([delta_a_log[l], delta_dt_bias[l], mlstm_bi[l], mlstm_bf[l]],
                        [SM_ALPHA, SM_ALPHA, SM_I, SM_F])
        proj_big, proj_f32 = _projection(x, mod[l], w_big, w_small, rows, tm, 1280)
        qkv_c = _conv_a(proj_big, conv_a, l, rows, lat_len)
        o_a, sf_a = _delta(qkv_c, proj_f32, par, state_delta, l, rows)
        y_b, hf_b = _lru(proj_f32, proj_big, conv_b_w, conv_b_b, lru_wa, lru_ba, lru_wx, lru_bx, lru_lambda,
                         state_lru, l, rows)
        h_c, cf, nf, mf = _mlstm(proj_big, proj_f32, par, state_mlstm_C, state_mlstm_n, m_bcast, l, rows)
        wr = jnp.zeros((D_MODEL, LANES), F32).at[:, :N_GROUPS].set(w_rg[l]).at[:, 64:64 + N_EXPERTS].set(w_re[l])
        br = jnp.zeros((1, LANES), F32).at[0, :N_GROUPS].set(b_rg[l]).at[0, 64:64 + N_EXPERTS].set(b_re[l])
        x1, h2, gate = _merge(x, mod[l], o_a, proj_big, y_b, h_c,
                              delta_norm[l].reshape(1, HEAD_DIM), mlstm_norm[l].reshape(1, W_MIX),
                              w_pa[l].astype(BF16), w_pb[l].astype(BF16), w_pc[l].astype(BF16),
                              w_out[l].astype(BF16), ln1_g[l].reshape(1, D_MODEL), ln1_b[l].reshape(1, D_MODEL),
                              wr, br, rows, 256)
        plan = _moe_plan(gate[:, ROUTE_GROUP_LANE].astype(jnp.int32), rows.total)
        y_moe = _moe_experts(h2, gate, plan, w_e1g, w_e3g, w_e2g, l, rows.total)
        x = _moe_combine(y_moe, plan, x1, mod[l], ln2_g[l].reshape(1, D_MODEL), ln2_b[l].reshape(1, D_MODEL), rows)
        n_ctx_blocks = rows.ctx_rows // lat_len
        lru_fin = jnp.swapaxes(hf_b[:n_ctx_blocks], 1, 2).reshape(n_ctx, 2, W_B)
        finals.append((sf_a[:n_ctx], lru_fin, cf[:n_ctx], nf[:n_ctx, :, :, 0, :], mf[:n_ctx, :, :, 0, 0]))

    new_delta, new_lru, new_mc, new_mn, new_mm = (jnp.stack([f[i] for f in finals], axis=1) for i in range(5))
    y_prompt = x[:rows.ctx_rows].reshape(n_ctx, ctx_len, D_MODEL)
    y_sample = x[rows.ctx_rows:].reshape(n_lat, lat_len, D_MODEL)
    return (y_prompt, y_sample, new_delta, new_lru, new_mc, new_mn, new_mm)
```

```python
import functools
import math
from typing import NamedTuple

import jax
import jax.numpy as jnp
from jax import lax
from jax.experimental import pallas as pl
from jax.experimental.pallas import tpu as pltpu

F32 = jnp.float32
BF16 = jnp.bfloat16

D_MODEL = 1024
DEPTH = 2
GRID_W = 64
POS_BASE = 10000.0
CONV_W = 4
LN_EPS = 1e-5
RMS_EPS = 1e-6
N_HEADS = 4
HEAD_DIM = 128
W_MIX = N_HEADS * HEAD_DIM
CHUNK = 64
W_B = 512
NB_B = 4
BW_B = W_B // NB_B
LRU_C = 8.0
N_GROUPS = 4
E_PER_GROUP = 8
N_EXPERTS = N_GROUPS * E_PER_GROUP
D_EXPERT = 256
DN_ALPHA = (2 * DEPTH) ** 0.25

LANES = 128
VMEM_LIMIT = 56 * 1024 * 1024

OFF_GATES, OFF_QKV, OFF_QC, OFF_KC, OFF_VC, OFF_Z, OFF_GB, OFF_OC = 0, 3072, 4608, 5120, 5632, 6144, 6656, 7168
N_BIG = OFF_OC + W_MIX
N_F32 = W_B + LANES
SM_BETA, SM_ALPHA, SM_I, SM_F = 0, 8, 16, 24
ROUTE_GROUP_LANE = 8


class Rows(NamedTuple):
    n_ctx: int
    ctx_len: int
    n_lat: int
    lat_len: int

    @property
    def ctx_rows(self):
        return self.n_ctx * self.ctx_len

    @property
    def total(self):
        return self.ctx_rows + self.n_lat * self.lat_len


def _cparams(sem):
    return pltpu.CompilerParams(dimension_semantics=sem, vmem_limit_bytes=VMEM_LIMIT)


def _sigmoid(x):
    return 1.0 / (1.0 + jnp.exp(-x))


def _silu(x):
    return x * _sigmoid(x)


def _softplus(x):
    return jnp.maximum(x, 0.0) + jnp.log1p(jnp.exp(-jnp.abs(x)))


def _ln(x):
    mu = jnp.mean(x, axis=-1, keepdims=True)
    xc = x - mu
    var = jnp.mean(xc * xc, axis=-1, keepdims=True)
    return xc * lax.rsqrt(var + LN_EPS)


def _bdot(a, b):
    return jnp.dot(a.astype(BF16), b.astype(BF16), preferred_element_type=F32)


def _bdot_nt(a, b):
    return lax.dot_general(a.astype(BF16), b.astype(BF16), (((1,), (1,)), ((), ())), preferred_element_type=F32)


def _bdot_tn(a, b):
    return lax.dot_general(a.astype(BF16), b.astype(BF16), (((0,), (0,)), ((), ())), preferred_element_type=F32)


def _split2(a):
    hi = a.astype(BF16)
    lo = (a - hi.astype(F32)).astype(BF16)
    return hi, lo


def _split3(a):
    hi = a.astype(BF16)
    r = a - hi.astype(F32)
    mid = r.astype(BF16)
    lo = (r - mid.astype(F32)).astype(BF16)
    return hi, mid, lo


def _dot3(a, b):
    ah, al = _split2(a)
    bh, bl = _split2(b)
    d = functools.partial(jnp.dot, preferred_element_type=F32)
    return d(ah, bh) + (d(ah, bl) + d(al, bh))


def _dot_exact_lhs(m_bf16, x):
    xh, xm, xl = _split3(x)
    d = functools.partial(jnp.dot, preferred_element_type=F32)
    return d(m_bf16, xh) + (d(m_bf16, xm) + d(m_bf16, xl))


def _cond_index(row0, rows: Rows):
    return jnp.maximum(0, (row0 - rows.ctx_rows + rows.lat_len) // rows.lat_len)


def _prep_kernel(xp_ref, xs_ref, pos_ref, o_ref, *, n_ctx_tiles):
    i = pl.program_id(0)

    @pl.when(i < n_ctx_tiles)
    def _():
        o_ref[...] = xp_ref[...]

    @pl.when(i >= n_ctx_tiles)
    def _():
        o_ref[...] = xs_ref[...] + pos_ref[...]


def _prep(xp2, xs2, pos, rows: Rows, tm):
    n_ctx_tiles = rows.ctx_rows // tm
    n_tiles = rows.total // tm
    pos_tiles = rows.lat_len // tm
    return pl.pallas_call(
        functools.partial(_prep_kernel, n_ctx_tiles=n_ctx_tiles),
        grid=(n_tiles,),
        in_specs=[
            pl.BlockSpec((tm, D_MODEL), lambda i: (jnp.minimum(i, n_ctx_tiles - 1), 0)),
            pl.BlockSpec((tm, D_MODEL), lambda i: (jnp.maximum(i - n_ctx_tiles, 0), 0)),
            pl.BlockSpec((tm, D_MODEL), lambda i: (jnp.maximum(i - n_ctx_tiles, 0) % pos_tiles, 0)),
        ],
        out_specs=pl.BlockSpec((tm, D_MODEL), lambda i: (i, 0)),
        out_shape=jax.ShapeDtypeStruct((rows.total, D_MODEL), F32),
        compiler_params=_cparams(("arbitrary",)),
        name="prep",
    )(xp2, xs2, pos)


def _mod_kernel(c_ref, w_ref, b_ref, o_ref):
    o_ref[0] = _bdot(_silu(c_ref[...]), w_ref[0]) + b_ref[0]


def _modulation(cond8, w_mod, b_mod):
    tn = 1024
    n6 = 6 * D_MODEL
    return pl.pallas_call(
        _mod_kernel,
        grid=(DEPTH, n6 // tn),
        in_specs=[
            pl.BlockSpec((8, D_MODEL), lambda l, n: (0, 0)),
            pl.BlockSpec((1, D_MODEL, tn), lambda l, n: (l, 0, n)),
            pl.BlockSpec((1, 1, tn), lambda l, n: (l, 0, n)),
        ],
        out_specs=pl.BlockSpec((1, 8, tn), lambda l, n: (l, 0, n)),
        out_shape=jax.ShapeDtypeStruct((DEPTH, 8, n6), F32),
        compiler_params=_cparams(("arbitrary", "arbitrary")),
        name="modulation",
    )(cond8, w_mod, b_mod.reshape(DEPTH, 1, n6))


def _proj_kernel(x_ref, mod_ref, wb_ref, wf_ref, ob_ref, of_ref, h_scr):
    n = pl.program_id(1)

    @pl.when(n == 0)
    def _():
        shift1 = mod_ref[0, 0:1, :]
        scale1 = mod_ref[0, 1:2, :]
        h = (_ln(x_ref[...]) * (1.0 + scale1) + shift1).astype(BF16)
        h_scr[...] = h
        of_ref[...] = jnp.dot(h, wf_ref[...], preferred_element_type=F32)

    ob_ref[...] = jnp.dot(h_scr[...], wb_ref[...], preferred_element_type=F32).astype(BF16)


def _projection(x, mod_l, w_big, w_f32, rows: Rows, tm, tn):
    nt = rows.total
    return pl.pallas_call(
        _proj_kernel,
        grid=(nt // tm, N_BIG // tn),
        in_specs=[
            pl.BlockSpec((tm, D_MODEL), lambda i, n: (i, 0)),
            pl.BlockSpec((1, 6, D_MODEL), lambda i, n: (_cond_index(i * tm, rows), 0, 0)),
            pl.BlockSpec((D_MODEL, tn), lambda i, n: (0, n)),
            pl.BlockSpec((D_MODEL, N_F32), lambda i, n: (0, 0)),
        ],
        out_specs=[
            pl.BlockSpec((tm, tn), lambda i, n: (i, n)),
            pl.BlockSpec((tm, N_F32), lambda i, n: (i, 0)),
        ],
        out_shape=[
            jax.ShapeDtypeStruct((nt, N_BIG), BF16),
            jax.ShapeDtypeStruct((nt, N_F32), F32),
        ],
        scratch_shapes=[pltpu.VMEM((tm, D_MODEL), BF16)],
        compiler_params=_cparams(("arbitrary", "arbitrary")),
        name="projection",
    )(x, mod_l, w_big, w_f32)


def _conv_taps(x, w_ref, seq_len):
    n = x.shape[0]
    pos = lax.broadcasted_iota(jnp.int32, x.shape, 0) & (seq_len - 1)
    xm1 = jnp.where(pos >= 1, pltpu.roll(x, 1, 0), 0.0)
    xp1 = jnp.where(pos <= seq_len - 2, pltpu.roll(x, n - 1, 0), 0.0)
    xp2 = jnp.where(pos <= seq_len - 3, pltpu.roll(x, n - 2, 0), 0.0)
    return xm1 * w_ref[0:1, :] + x * w_ref[1:2, :] + xp1 * w_ref[2:3, :] + xp2 * w_ref[3:4, :]


def _conv_a_kernel(x_ref, w_ref, o_ref, *, rows: Rows, br):
    b = pl.program_id(0)
    j = pl.program_id(1)
    seq_len = jnp.where(b * br < rows.ctx_rows, rows.ctx_len, rows.lat_len)
    y = _silu(_conv_taps(x_ref[...].astype(F32), w_ref[0], seq_len))
    nrm = lax.rsqrt(jnp.sum(y * y, axis=-1, keepdims=True) + RMS_EPS)
    fac = jnp.where(j < N_HEADS, nrm * HEAD_DIM ** -0.5, jnp.where(j < 2 * N_HEADS, nrm, 1.0))
    o_ref[...] = (y * fac).astype(BF16)


def _conv_a(proj_big, conv_w, l, rows: Rows, br):
    nt = rows.total
    ncol = 3 * N_HEADS
    return pl.pallas_call(
        functools.partial(_conv_a_kernel, rows=rows, br=br),
        grid=(nt // br, ncol),
        in_specs=[
            pl.BlockSpec((br, LANES), lambda b, j: (b, OFF_QKV // LANES + j)),
            pl.BlockSpec((1, CONV_W, LANES), lambda b, j: (l, 0, j)),
        ],
        out_specs=pl.BlockSpec((br, LANES), lambda b, j: (b, j)),
        out_shape=jax.ShapeDtypeStruct((nt, ncol * LANES), BF16),
        compiler_params=_cparams(("arbitrary", "arbitrary")),
        name="conv_a",
    )(proj_big, conv_w)


UNIT = 1024
NCH = UNIT // CHUNK


def _unit_layout(rows: Rows):
    n_ctx_units = rows.ctx_rows // UNIT
    lat_units = rows.lat_len // UNIT
    return n_ctx_units, lat_units, n_ctx_units + rows.n_lat * lat_units


def _unit_rowblock(u, d, rows: Rows):
    n_ctx_units, lat_units, _ = _unit_layout(rows)
    v = jnp.maximum(u - n_ctx_units, 0)
    b = v // lat_units
    j = v % lat_units
    jj = j + d * (lat_units - 1 - 2 * j)
    return jnp.where(u < n_ctx_units, u, n_ctx_units + b * lat_units + jj)


def _unit_lat_seq(u, rows: Rows):
    n_ctx_units, lat_units, _ = _unit_layout(rows)
    return jnp.maximum(u - n_ctx_units, 0) // lat_units


def _dir_masks(fwd):
    ii = lax.broadcasted_iota(jnp.int32, (CHUNK, CHUNK), 0)
    jj = lax.broadcasted_iota(jnp.int32, (CHUNK, CHUNK), 1)
    s = jnp.where(fwd, ii - jj, jj - ii)
    return s >= 0, s > 0, ii == jj


def _bmm(a, b):
    return jnp.einsum('gik,gkj->gij', a.astype(BF16), b.astype(BF16), preferred_element_type=F32)


def _bmm_nt(a, b):
    return jnp.einsum('gik,gjk->gij', a.astype(BF16), b.astype(BF16), preferred_element_type=F32)


def _bmm_tn(a, b):
    return jnp.einsum('gik,gij->gkj', a.astype(BF16), b.astype(BF16), preferred_element_type=F32)


def _chunk_cumsum(x, reverse):
    n = x.shape[0]
    pos = lax.broadcasted_iota(jnp.int32, x.shape, 0) & (CHUNK - 1)
    s = 1
    while s < CHUNK:
        if reverse:
            x = x + jnp.where(pos < CHUNK - s, pltpu.roll(x, n - s, 0), 0.0)
        else:
            x = x + jnp.where(pos >= s, pltpu.roll(x, s, 0), 0.0)
        s *= 2
    return x


def _dir_select(fwd, x):
    return jnp.where(fwd, x, pltpu.roll(x, LANES - N_HEADS, 1))


GROUP_CHUNKS = 8
GROUP_ROWS = GROUP_CHUNKS * CHUNK


def _group_columns(col_arr, row_arr, lane0):
    cols, rws = [], []
    for cc in range(GROUP_CHUNKS):
        for h in range(N_HEADS):
            cols.append(jnp.broadcast_to(col_arr[cc * CHUNK:(cc + 1) * CHUNK, lane0 + h:lane0 + h + 1],
                                         (CHUNK, HEAD_DIM)))
            if row_arr is not None:
                rws.append(jnp.broadcast_to(row_arr[lane0 + h:lane0 + h + 1, cc * CHUNK:(cc + 1) * CHUNK],
                                            (CHUNK, CHUNK)))
    return jnp.stack(cols), (jnp.stack(rws) if rws else None)


def _group_heads(ref, r0, col0):
    return jnp.stack([ref[pl.ds(r0 + cc * CHUNK, CHUNK), col0 + h * HEAD_DIM:col0 + (h + 1) * HEAD_DIM]
                      for cc in range(GROUP_CHUNKS) for h in range(N_HEADS)])


def _delta_kernel(qkv_ref, sm_ref, par_ref, s0_ref, o_ref, sf_ref,
                  s_scr, sa_scr, sb_scr, oq_scr, ov_scr, ge_scr, *, rows: Rows):
    d = pl.program_id(0)
    u = pl.program_id(1)
    n_ctx_units, lat_units, _ = _unit_layout(rows)
    seq_chunks = rows.ctx_len // CHUNK
    seq_per_unit = UNIT // rows.ctx_len
    fwd = d == 0
    is_ctx = u < n_ctx_units
    causal, strict, eye = _dir_masks(fwd)
    eye_f = jnp.where(eye, 1.0, 0.0)
    ii = lax.broadcasted_iota(jnp.int32, (CHUNK, CHUNK), 0)
    jj = lax.broadcasted_iota(jnp.int32, (CHUNK, CHUNK), 1)
    pair_masks = [jnp.logical_and((ii >> (s + 1)) == (jj >> (s + 1)), (ii >> s) != (jj >> s))
                  for s in range(CHUNK.bit_length() - 1)]
    ng = GROUP_CHUNKS * N_HEADS

    def pre(it, carry):
        r0 = pl.multiple_of(it * GROUP_ROWS, GROUP_ROWS)
        g0 = pl.multiple_of(it * ng, ng)
        sm = sm_ref[pl.ds(r0, GROUP_ROWS), :]
        g_all = -jnp.exp(par_ref[0:1, :]) * _softplus(sm + par_ref[1:2, :])
        gcum = _dir_select(fwd, jnp.where(fwd, _chunk_cumsum(g_all, False), _chunk_cumsum(g_all, True)))
        bsel = _dir_select(fwd, _sigmoid(sm))
        gi, grow = _group_columns(gcum, gcum.T, SM_ALPHA)
        beta = jnp.stack([bsel[cc * CHUNK:(cc + 1) * CHUNK, SM_BETA + h:SM_BETA + h + 1]
                          for cc in range(GROUP_CHUNKS) for h in range(N_HEADS)])
        q = _group_heads(qkv_ref, r0, 0)
        k = _group_heads(qkv_ref, r0, W_MIX)
        v = _group_heads(qkv_ref, r0, 2 * W_MIX)
        kf = k.astype(F32)
        decay = jnp.exp(jnp.where(causal, gi[:, :, :CHUNK] - grow, -jnp.inf))
        lmat = jnp.where(strict, beta * _bmm_nt(k, k) * decay, 0.0)
        eg = jnp.exp(gi)
        rhs = jnp.concatenate([v.astype(F32) * beta, kf * (beta * eg)], axis=-1)
        t = eye_f - jnp.where(pair_masks[0], lmat, 0.0)
        for pm in pair_masks[1:]:
            tb = t.astype(BF16)
            t = t - _bmm(_bmm(tb, jnp.where(pm, lmat, 0.0)), tb)
        t0 = t.astype(BF16)
        mh, ml = _split2(eye_f + lmat)
        resid = eye_f - (_bmm(mh, t0) + _bmm(ml, t0))
        t1 = t0.astype(F32) + _bmm(t0, resid)
        uu = _bmm(t1, rhs)
        qk = jnp.where(causal, _bmm_nt(q, k) * decay, 0.0)
        g_end = jnp.where(fwd, gi[:, CHUNK - 1:CHUNK, :], gi[:, 0:1, :])
        u_v = uu[:, :, :HEAD_DIM]
        u_k = uu[:, :, HEAD_DIM:]
        k_dec = kf * jnp.exp(g_end - gi)
        sa_scr[pl.ds(g0, ng)] = (-_bmm_tn(k_dec, u_k)).astype(BF16)
        sb_scr[pl.ds(g0, ng)] = _bmm_tn(k_dec, u_v)
        oq_scr[pl.ds(g0, ng)] = (q.astype(F32) * eg - _bmm(qk, u_k)).astype(BF16)
        ov_scr[pl.ds(g0, ng)] = _bmm(qk, u_v)
        ge_scr[pl.ds(g0, ng)] = jnp.broadcast_to(jnp.exp(g_end), (ng, 8, HEAD_DIM))
        return carry

    lax.fori_loop(0, NCH // GROUP_CHUNKS, pre, 0)

    first_lat = jnp.logical_and(u >= n_ctx_units, (u - n_ctx_units) % lat_units == 0)

    @pl.when(first_lat)
    def _():
        s_scr[...] = s0_ref[0, 0, 0]

    @pl.when(u == n_ctx_units)
    def _():
        sf_ref[...] = jnp.zeros(sf_ref.shape, F32)

    def step(n, carry):
        c = jnp.where(fwd, n, NCH - 1 - n)
        r0 = pl.multiple_of(c * CHUNK, CHUNK)
        hs = pl.ds(pl.multiple_of(c * N_HEADS, N_HEADS), N_HEADS)

        @pl.when(jnp.logical_and(is_ctx, n % seq_chunks == 0))
        def _():
            s_scr[...] = jnp.zeros(s_scr.shape, F32)

        s = s_scr[...]
        sb = s.astype(BF16)
        s_scr[...] = s * ge_scr[hs][:, 0:1, :] + (_bmm(sa_scr[hs], sb) + sb_scr[hs])
        o = _bmm(oq_scr[hs], sb) + ov_scr[hs]
        for h in range(N_HEADS):
            o_ref[0, pl.ds(r0, CHUNK), h * HEAD_DIM:(h + 1) * HEAD_DIM] = o[h]

        @pl.when(jnp.logical_and(is_ctx, n % seq_chunks == seq_chunks - 1))
        def _():
            sf_ref[c // seq_chunks, 0] = s_scr[...]

        return carry

    lax.fori_loop(0, NCH, step, 0)


def _delta(qkv_c, proj_f32, par, state, l, rows: Rows):
    n_ctx_units, lat_units, n_units = _unit_layout(rows)
    seq_per_unit = UNIT // rows.ctx_len
    nt = rows.total
    sm_col = W_B // LANES
    kern = functools.partial(_delta_kernel, rows=rows)
    hshape = (NCH * N_HEADS, CHUNK, HEAD_DIM)
    return pl.pallas_call(
        kern,
        grid=(2, n_units),
        in_specs=[
            pl.BlockSpec((UNIT, 3 * W_MIX), lambda d, u: (_unit_rowblock(u, d, rows), 0)),
            pl.BlockSpec((UNIT, LANES), lambda d, u: (_unit_rowblock(u, d, rows), sm_col)),
            pl.BlockSpec((8, LANES), lambda d, u: (0, 0)),
            pl.BlockSpec((1, 1, 1, N_HEADS, HEAD_DIM, HEAD_DIM),
                         lambda d, u: (_unit_lat_seq(u, rows), l, d, 0, 0, 0)),
        ],
        out_specs=[
            pl.BlockSpec((1, UNIT, W_MIX), lambda d, u: (d, _unit_rowblock(u, d, rows), 0)),
            pl.BlockSpec((seq_per_unit, 1, N_HEADS, HEAD_DIM, HEAD_DIM),
                         lambda d, u: (jnp.minimum(u, n_ctx_units), d, 0, 0, 0)),
        ],
        out_shape=[
            jax.ShapeDtypeStruct((2, nt, W_MIX), F32),
            jax.ShapeDtypeStruct((rows.n_ctx + seq_per_unit, 2, N_HEADS, HEAD_DIM, HEAD_DIM), F32),
        ],
        scratch_shapes=[
            pltpu.VMEM((N_HEADS, HEAD_DIM, HEAD_DIM), F32),
            pltpu.VMEM((NCH * N_HEADS, HEAD_DIM, HEAD_DIM), BF16),
            pltpu.VMEM((NCH * N_HEADS, HEAD_DIM, HEAD_DIM), F32),
            pltpu.VMEM(hshape, BF16),
            pltpu.VMEM(hshape, F32),
            pltpu.VMEM((NCH * N_HEADS, 8, HEAD_DIM), F32),
        ],
        compiler_params=_cparams(("arbitrary", "arbitrary")),
        name="delta",
    )(qkv_c, proj_f32, par, state)


def _mlstm_kernel(big_ref, sm_ref, par_ref, c0_ref, n0_ref, m0_ref, o_ref, cf_ref, nf_ref, mf_ref,
                  c_scr, n_scr, m_scr, fi_scr, mi_scr, pv_scr, ps_scr, kv_scr, ks_scr, me_scr, *, rows: Rows):
    d = pl.program_id(0)
    u = pl.program_id(1)
    n_ctx_units, lat_units, _ = _unit_layout(rows)
    seq_chunks = rows.ctx_len // CHUNK
    fwd = d == 0
    is_ctx = u < n_ctx_units
    causal, _, _ = _dir_masks(fwd)
    qo, ko, vo = 0, W_MIX, 2 * W_MIX
    ng = GROUP_CHUNKS * N_HEADS

    def pre(it, carry):
        r0 = pl.multiple_of(it * GROUP_ROWS, GROUP_ROWS)
        gs = pl.ds(pl.multiple_of(it * ng, ng), ng)
        sm = sm_ref[pl.ds(r0, GROUP_ROWS), :]
        f_all = -_softplus(-(sm + par_ref[3:4, :]))
        fcum = _dir_select(fwd, jnp.where(fwd, _chunk_cumsum(f_all, False), _chunk_cumsum(f_all, True)))
        i_al = pltpu.roll(_dir_select(fwd, sm + par_ref[2:3, :]), SM_F - SM_I, 1)
        fi, hrow = _group_columns(fcum, (fcum - i_al).T, SM_F)
        it_b, _ = _group_columns(i_al, None, SM_F)
        q = _group_heads(big_ref, r0, qo)
        v = _group_heads(big_ref, r0, vo)
        ks = _group_heads(big_ref, r0, ko).astype(F32) * HEAD_DIM ** -0.5
        dmat = jnp.where(causal, fi[:, :, :CHUNK] - hrow, -jnp.inf)
        f_end = jnp.where(fwd, fi[:, CHUNK - 1:CHUNK, :], fi[:, 0:1, :])
        d_end = f_end - fi + it_b
        m_intra = jnp.max(dmat, axis=-1, keepdims=True)
        m_end = jnp.max(d_end, axis=1, keepdims=True)
        p_loc = jnp.exp(dmat - m_intra) * _bmm_nt(q, ks)
        kw = ks * jnp.exp(d_end - m_end)
        fi_scr[gs] = fi
        mi_scr[gs] = jnp.broadcast_to(m_intra, (ng, CHUNK, HEAD_DIM))
        pv_scr[gs] = _bmm(p_loc, v)
        ps_scr[gs] = jnp.broadcast_to(jnp.sum(p_loc, axis=-1, keepdims=True), (ng, CHUNK, HEAD_DIM))
        kv_scr[gs] = _bmm_tn(kw, v)
        ks_scr[gs] = jnp.broadcast_to(jnp.sum(kw, axis=1, keepdims=True), (ng, 8, HEAD_DIM))
        me_scr[gs] = jnp.broadcast_to(m_end, (ng, 8, HEAD_DIM))
        return carry

    lax.fori_loop(0, NCH // GROUP_CHUNKS, pre, 0)

    first_lat = jnp.logical_and(u >= n_ctx_units, (u - n_ctx_units) % lat_units == 0)

    @pl.when(first_lat)
    def _():
        c_scr[...] = c0_ref[0, 0, 0]
        for h in range(N_HEADS):
            n_scr[h] = jnp.broadcast_to(n0_ref[0, 0, 0, h:h + 1, :], (8, HEAD_DIM))
            m_scr[h] = jnp.broadcast_to(m0_ref[0, 0, 0, h:h + 1, :], (8, HEAD_DIM))

    @pl.when(u == n_ctx_units)
    def _():
        cf_ref[...] = jnp.zeros(cf_ref.shape, F32)
        nf_ref[...] = jnp.zeros(nf_ref.shape, F32)
        mf_ref[...] = jnp.zeros(mf_ref.shape, F32)

    def step(n, carry):
        c = jnp.where(fwd, n, NCH - 1 - n)
        r0 = pl.multiple_of(c * CHUNK, CHUNK)

        @pl.when(jnp.logical_and(is_ctx, n % seq_chunks == 0))
        def _():
            c_scr[...] = jnp.zeros(c_scr.shape, F32)
            n_scr[...] = jnp.zeros(n_scr.shape, F32)
            m_scr[...] = jnp.zeros(m_scr.shape, F32)

        hs = pl.ds(pl.multiple_of(c * N_HEADS, N_HEADS), N_HEADS)
        q = jnp.stack([big_ref[pl.ds(r0, CHUNK), qo + h * HEAD_DIM:qo + (h + 1) * HEAD_DIM]
                       for h in range(N_HEADS)])
        cs = c_scr[...]
        ns = n_scr[...][:, 0:1, :]
        ms = m_scr[...][:, 0:1, :]
        fi = fi_scr[hs]
        mi = mi_scr[hs]
        inter = fi + ms
        mt = jnp.maximum(inter, mi)
        w_int = jnp.exp(inter - mt)
        w_loc = jnp.exp(mi - mt)
        num = w_int * _bmm(q, cs) + w_loc * pv_scr[hs]
        qn = jnp.sum(q.astype(F32) * ns, axis=-1, keepdims=True)
        den = w_int * qn + w_loc * ps_scr[hs]
        hh = num / jnp.maximum(jnp.abs(den), jnp.exp(-mt))
        f_end = jnp.where(fwd, fi[:, CHUNK - 1:CHUNK, :], fi[:, 0:1, :])
        inter_end = f_end + ms
        m_end = me_scr[hs][:, 0:1, :]
        m_new = jnp.maximum(inter_end, m_end)
        s_int = jnp.exp(inter_end - m_new)
        s_loc = jnp.exp(m_end - m_new)
        c_scr[...] = cs * s_int[:, :, 0:1] + kv_scr[hs] * s_loc[:, :, 0:1]
        n_scr[...] = jnp.broadcast_to(ns * s_int + ks_scr[hs][:, 0:1, :] * s_loc, n_scr.shape)
        m_scr[...] = jnp.broadcast_to(m_new, m_scr.shape)
        for h in range(N_HEADS):
            o_ref[0, pl.ds(r0, CHUNK), h * HEAD_DIM:(h + 1) * HEAD_DIM] = hh[h]

        @pl.when(jnp.logical_and(is_ctx, n % seq_chunks == seq_chunks - 1))
        def _():
            sq = c // seq_chunks
            cf_ref[sq, 0] = c_scr[...]
            nf_ref[sq, 0] = n_scr[...]
            mf_ref[sq, 0] = m_scr[...]

        return carry

    lax.fori_loop(0, NCH, step, 0)


def _mlstm(proj_big, proj_f32, par, st_c, st_n8, st_m8, l, rows: Rows):
    n_ctx_units, lat_units, n_units = _unit_layout(rows)
    seq_per_unit = UNIT // rows.ctx_len
    nt = rows.total
    sm_col = W_B // LANES
    qkv_blk = OFF_QC // (3 * W_MIX)
    assert OFF_QC % (3 * W_MIX) == 0
    kern = functools.partial(_mlstm_kernel, rows=rows)
    hshape = (NCH * N_HEADS, CHUNK, HEAD_DIM)
    n_out = rows.n_ctx + seq_per_unit
    return pl.pallas_call(
        kern,
        grid=(2, n_units),
        in_specs=[
            pl.BlockSpec((UNIT, 3 * W_MIX), lambda d, u: (_unit_rowblock(u, d, rows), qkv_blk)),
            pl.BlockSpec((UNIT, LANES), lambda d, u: (_unit_rowblock(u, d, rows), sm_col)),
            pl.BlockSpec((8, LANES), lambda d, u: (0, 0)),
            pl.BlockSpec((1, 1, 1, N_HEADS, HEAD_DIM, HEAD_DIM),
                         lambda d, u: (_unit_lat_seq(u, rows), l, d, 0, 0, 0)),
            pl.BlockSpec((1, 1, 1, N_HEADS, HEAD_DIM), lambda d, u: (_unit_lat_seq(u, rows), l, d, 0, 0)),
            pl.BlockSpec((1, 1, 1, N_HEADS, HEAD_DIM), lambda d, u: (_unit_lat_seq(u, rows), l, d, 0, 0)),
        ],
        out_specs=[
            pl.BlockSpec((1, UNIT, W_MIX), lambda d, u: (d, _unit_rowblock(u, d, rows), 0)),
            pl.BlockSpec((seq_per_unit, 1, N_HEADS, HEAD_DIM, HEAD_DIM),
                         lambda d, u: (jnp.minimum(u, n_ctx_units), d, 0, 0, 0)),
            pl.BlockSpec((seq_per_unit, 1, N_HEADS, 8, HEAD_DIM),
                         lambda d, u: (jnp.minimum(u, n_ctx_units), d, 0, 0, 0)),
            pl.BlockSpec((seq_per_unit, 1, N_HEADS, 8, HEAD_DIM),
                         lambda d, u: (jnp.minimum(u, n_ctx_units), d, 0, 0, 0)),
        ],
        out_shape=[
            jax.ShapeDtypeStruct((2, nt, W_MIX), F32),
            jax.ShapeDtypeStruct((n_out, 2, N_HEADS, HEAD_DIM, HEAD_DIM), F32),
            jax.ShapeDtypeStruct((n_out, 2, N_HEADS, 8, HEAD_DIM), F32),
            jax.ShapeDtypeStruct((n_out, 2, N_HEADS, 8, HEAD_DIM), F32),
        ],
        scratch_shapes=[
            pltpu.VMEM((N_HEADS, HEAD_DIM, HEAD_DIM), F32),
            pltpu.VMEM((N_HEADS, 8, HEAD_DIM), F32),
            pltpu.VMEM((N_HEADS, 8, HEAD_DIM), F32),
            pltpu.VMEM(hshape, F32),
            pltpu.VMEM(hshape, F32),
            pltpu.VMEM(hshape, F32),
            pltpu.VMEM(hshape, F32),
            pltpu.VMEM((NCH * N_HEADS, HEAD_DIM, HEAD_DIM), F32),
            pltpu.VMEM((NCH * N_HEADS, 8, HEAD_DIM), F32),
            pltpu.VMEM((NCH * N_HEADS, 8, HEAD_DIM), F32),
        ],
        compiler_params=_cparams(("arbitrary", "arbitrary")),
        name="mlstm",
    )(proj_big, proj_f32, par, st_c, st_n8, st_m8)


def _gelu_tanh(x):
    return 0.5 * x * (1.0 + jnp.tanh(math.sqrt(2.0 / math.pi) * (x + 0.044715 * (x * x * x))))


def _tile_scan(a, b, reverse):
    n = a.shape[0]
    row = lax.broadcasted_iota(jnp.int32, a.shape, 0)
    s = 1
    while s < n:
        if reverse:
            ok = row < n - s
            a_sh = jnp.where(ok, pltpu.roll(a, n - s, 0), 1.0)
            b_sh = jnp.where(ok, pltpu.roll(b, n - s, 0), 0.0)
        else:
            ok = row >= s
            a_sh = jnp.where(ok, pltpu.roll(a, s, 0), 1.0)
            b_sh = jnp.where(ok, pltpu.roll(b, s, 0), 0.0)
        b = a * b_sh + b
        a = a * a_sh
        s *= 2
    return a, b


def _lru_kernel(x_ref, g_ref, cw_ref, cb_ref, wa_ref, ba_ref, wx_ref, bx_ref, lam_ref, h0_ref,
                y_ref, hf_ref, xc_scr, hfw_scr, *, rows: Rows, br):
    blk = pl.program_id(0)
    is_ctx = blk * br < rows.ctx_rows
    ts = rows.ctx_len
    n_tiles = br // ts
    seq_len = jnp.where(is_ctx, rows.ctx_len, rows.lat_len)
    xc_scr[...] = _conv_taps(x_ref[...], cw_ref[0], seq_len) + cb_ref[0]

    for dd in range(2):
        reverse = dd == 1
        sp_lam = _softplus(-lam_ref[0, dd])

        def tile(n, carry, dd=dd, reverse=reverse, sp_lam=sp_lam):
            t = (n_tiles - 1 - n) if reverse else n
            r0 = pl.multiple_of(t * ts, ts)
            xc = xc_scr[pl.ds(r0, ts), :]
            r = _sigmoid(_bdot(xc, wa_ref[0, dd, 0]) + ba_ref[0, dd])
            gi = _sigmoid(_bdot(xc, wx_ref[0, dd, 0]) + bx_ref[0, dd])
            log_a = -LRU_C * r * sp_lam
            a = jnp.exp(log_a)
            b = jnp.sqrt(-jnp.tanh(log_a) * (a * a + 1.0)) * (gi * xc)
            a_cum, h = _tile_scan(a, b, reverse)
            carry = jnp.where(is_ctx, 0.0, carry)
            h = h + a_cum * carry
            last = h[0:1, :] if reverse else h[ts - 1:ts, :]
            hf_ref[0, dd, pl.ds(t, 1), :] = last
            if reverse:
                g = g_ref[pl.ds(r0, ts), :].astype(F32)
                y_ref[pl.ds(r0, ts), :] = ((hfw_scr[pl.ds(r0, ts), :] + h) * _gelu_tanh(g)).astype(BF16)
            else:
                hfw_scr[pl.ds(r0, ts), :] = h
            return last

        lax.fori_loop(0, n_tiles, tile, h0_ref[0, 0, dd])


def _lru(proj_f32, proj_big, conv_w, conv_b, wa, ba, wx, bx, lam, h0, l, rows: Rows):
    br = rows.lat_len
    nt = rows.total
    n_blocks = nt // br
    n_ctx_blocks = rows.ctx_rows // br
    n_tiles = br // rows.ctx_len
    gcol = OFF_GB // BW_B

    def lat_seq(b):
        return jnp.maximum(b - n_ctx_blocks, 0)

    vec = lambda a: a.reshape(DEPTH, 2, 1, W_B)
    return pl.pallas_call(
        functools.partial(_lru_kernel, rows=rows, br=br),
        grid=(n_blocks, NB_B),
        in_specs=[
            pl.BlockSpec((br, BW_B), lambda b, j: (b, j)),
            pl.BlockSpec((br, BW_B), lambda b, j: (b, gcol + j)),
            pl.BlockSpec((1, CONV_W, BW_B), lambda b, j: (l, 0, j)),
            pl.BlockSpec((1, 1, BW_B), lambda b, j: (l, 0, j)),
            pl.BlockSpec((1, 2, 1, BW_B, BW_B), lambda b, j: (l, 0, j, 0, 0)),
            pl.BlockSpec((1, 2, 1, BW_B), lambda b, j: (l, 0, 0, j)),
            pl.BlockSpec((1, 2, 1, BW_B, BW_B), lambda b, j: (l, 0, j, 0, 0)),
            pl.BlockSpec((1, 2, 1, BW_B), lambda b, j: (l, 0, 0, j)),
            pl.BlockSpec((1, 2, 1, BW_B), lambda b, j: (l, 0, 0, j)),
            pl.BlockSpec((1, 1, 2, 1, BW_B), lambda b, j: (lat_seq(b), l, 0, 0, j)),
        ],
        out_specs=[
            pl.BlockSpec((br, BW_B), lambda b, j: (b, j)),
            pl.BlockSpec((1, 2, n_tiles, BW_B), lambda b, j: (b, 0, 0, j)),
        ],
        out_shape=[
            jax.ShapeDtypeStruct((nt, W_B), BF16),
            jax.ShapeDtypeStruct((n_blocks, 2, n_tiles, W_B), F32),
        ],
        scratch_shapes=[pltpu.VMEM((br, BW_B), F32), pltpu.VMEM((br, BW_B), F32)],
        compiler_params=_cparams(("arbitrary", "arbitrary")),
        name="lru",
    )(proj_f32, proj_big, conv_w, conv_b.reshape(DEPTH, 1, W_B), wa, vec(ba), wx, vec(bx), vec(lam),
      h0.reshape(h0.shape[0], DEPTH, 2, 1, W_B))


def _route(lg, le):
    lane = lax.broadcasted_iota(jnp.int32, lg.shape, 1)
    neg = -jnp.inf
    lgm = jnp.where(lane < N_GROUPS, lg, neg)
    gmax = jnp.max(lgm, axis=-1, keepdims=True)
    p_grp = 1.0 / jnp.sum(jnp.exp(lgm - gmax), axis=-1, keepdims=True)
    g_sel = jnp.min(jnp.where(lgm == gmax, lane, LANES), axis=-1, keepdims=True)
    in_grp = jnp.logical_and(lane >= g_sel * E_PER_GROUP, lane < (g_sel + 1) * E_PER_GROUP)
    lem = jnp.where(in_grp, le, neg)
    v1 = jnp.max(lem, axis=-1, keepdims=True)
    i1 = jnp.min(jnp.where(lem == v1, lane, LANES), axis=-1, keepdims=True)
    lem2 = jnp.where(lane == i1, neg, lem)
    v2 = jnp.max(lem2, axis=-1, keepdims=True)
    i2 = jnp.min(jnp.where(lem2 == v2, lane, LANES), axis=-1, keepdims=True)
    e2 = jnp.exp(v2 - v1)
    w1 = p_grp / (1.0 + e2)
    w2 = p_grp * e2 / (1.0 + e2)
    gate = jnp.where(lane == i1, w1, 0.0) + jnp.where(lane == i2, w2, 0.0)
    local = gate
    for g in range(1, N_GROUPS):
        local = jnp.where(g_sel == g, pltpu.roll(gate, LANES - g * E_PER_GROUP, 1), local)
    return jnp.where(lane < E_PER_GROUP, local, jnp.where(lane == ROUTE_GROUP_LANE, g_sel.astype(F32), 0.0))


def _merge_kernel(x_ref, mod_ref, oa_ref, z_ref, yb_ref, hc_ref, oc_ref, gt_ref, dn_ref, mn_ref,
                  wpa_ref, wpb_ref, wpc_ref, wout_ref, g1_ref, b1_ref, wr_ref, br_ref,
                  x1_ref, h2_ref, gate_ref):
    oa = oa_ref[0] + oa_ref[1]
    hc = hc_ref[0] + hc_ref[1]
    ya, yc = [], []
    for h in range(N_HEADS):
        sl = slice(h * HEAD_DIM, (h + 1) * HEAD_DIM)
        o_h = oa[:, sl]
        o_h = o_h * lax.rsqrt(jnp.mean(o_h * o_h, axis=-1, keepdims=True) + RMS_EPS) * dn_ref[...]
        ya.append((o_h * _silu(z_ref[:, sl].astype(F32))).astype(BF16))
        c_h = _ln(hc[:, sl]) * mn_ref[:, sl]
        yc.append((_sigmoid(oc_ref[:, sl].astype(F32)) * c_h).astype(BF16))
    ya = jnp.concatenate(ya, axis=-1)
    yc = jnp.concatenate(yc, axis=-1)
    ga = _sigmoid(gt_ref[:, 0:D_MODEL].astype(F32))
    gb = _sigmoid(gt_ref[:, D_MODEL:2 * D_MODEL].astype(F32))
    gc = _sigmoid(gt_ref[:, 2 * D_MODEL:3 * D_MODEL].astype(F32))
    d = functools.partial(jnp.dot, preferred_element_type=F32)
    merged = ga * d(ya, wpa_ref[...]) + gb * d(yb_ref[...], wpb_ref[...]) + gc * d(yc, wpc_ref[...])
    mixed = d(merged.astype(BF16), wout_ref[...])
    gate1 = mod_ref[0, 2:3, :]
    shift2 = mod_ref[0, 3:4, :]
    scale2 = mod_ref[0, 4:5, :]
    x1 = _ln(DN_ALPHA * x_ref[...] + gate1 * mixed) * g1_ref[...] + b1_ref[...]
    x1_ref[...] = x1
    h2 = _ln(x1) * (1.0 + scale2) + shift2
    h2_ref[...] = h2.astype(BF16)
    logits = _dot3(h2, wr_ref[...]) + br_ref[...]
    lg = logits
    le = pltpu.roll(logits, LANES - 64, 1)
    gate_ref[...] = _route(lg, le)


def _merge(x, mod_l, o_a, proj_big, y_b, h_c, dn, mn, wpa, wpb, wpc, wout, g1, b1, wr, br, rows: Rows, tm):
    nt = rows.total
    row = lambda i: (i, 0)
    const = lambda i: (0, 0)
    cw = W_MIX
    return pl.pallas_call(
        _merge_kernel,
        grid=(nt // tm,),
        in_specs=[
            pl.BlockSpec((tm, D_MODEL), row),
            pl.BlockSpec((1, 6, D_MODEL), lambda i: (_cond_index(i * tm, rows), 0, 0)),
            pl.BlockSpec((2, tm, cw), lambda i: (0, i, 0)),
            pl.BlockSpec((tm, cw), lambda i: (i, OFF_Z // cw)),
            pl.BlockSpec((tm, cw), row),
            pl.BlockSpec((2, tm, cw), lambda i: (0, i, 0)),
            pl.BlockSpec((tm, cw), lambda i: (i, OFF_OC // cw)),
            pl.BlockSpec((tm, 3 * D_MODEL), lambda i: (i, OFF_GATES // (3 * D_MODEL))),
            pl.BlockSpec((1, HEAD_DIM), const),
            pl.BlockSpec((1, cw), const),
            pl.BlockSpec((cw, D_MODEL), const),
            pl.BlockSpec((cw, D_MODEL), const),
            pl.BlockSpec((cw, D_MODEL), const),
            pl.BlockSpec((D_MODEL, D_MODEL), const),
            pl.BlockSpec((1, D_MODEL), const),
            pl.BlockSpec((1, D_MODEL), const),
            pl.BlockSpec((D_MODEL, LANES), const),
            pl.BlockSpec((1, LANES), const),
        ],
        out_specs=[
            pl.BlockSpec((tm, D_MODEL), row),
            pl.BlockSpec((tm, D_MODEL), row),
            pl.BlockSpec((tm, LANES), row),
        ],
        out_shape=[
            jax.ShapeDtypeStruct((nt, D_MODEL), F32),
            jax.ShapeDtypeStruct((nt, D_MODEL), BF16),
            jax.ShapeDtypeStruct((nt, LANES), F32),
        ],
        compiler_params=_cparams(("arbitrary",)),
        name="merge",
    )(x, mod_l, o_a, proj_big, y_b, h_c, proj_big, proj_big, dn, mn, wpa, wpb, wpc, wout, g1, b1, wr, br)


MOE_TILE = 512


class MoePlan(NamedTuple):
    dest_row: jax.Array
    dest_col: jax.Array
    tile_group: jax.Array
    by_tile: tuple
    by_block: tuple


def _pair_list(mask, n_pairs, minor):
    flat = mask.reshape(-1)
    cnt = jnp.sum(flat.astype(jnp.int32))
    idx = jnp.nonzero(flat, size=n_pairs, fill_value=0)[0].astype(jnp.int32)
    pos = jnp.arange(n_pairs, dtype=jnp.int32)
    valid = pos < cnt
    idx = jnp.where(valid, idx, idx[jnp.maximum(cnt - 1, 0)])
    major, mnr = idx // minor, idx % minor
    prev = jnp.concatenate([jnp.full((1,), -1, jnp.int32), major[:-1]])
    nxt = jnp.concatenate([major[1:], jnp.full((1,), -1, jnp.int32)])
    first = jnp.logical_and(valid, major != prev)
    last = jnp.logical_and(valid, jnp.logical_or(major != nxt, pos == cnt - 1))
    i32 = lambda a: a.astype(jnp.int32)
    return major, mnr, i32(first), i32(last), i32(valid)


def _moe_plan(g_sel, nt):
    n_blocks = nt // MOE_TILE
    n_tiles = n_blocks + N_GROUPS
    n_pairs = n_tiles + N_GROUPS * n_blocks
    oh = (g_sel[:, None] == jnp.arange(N_GROUPS, dtype=jnp.int32)[None, :]).astype(jnp.int32)
    counts = jnp.sum(oh, axis=0)
    tiles_g = (counts + MOE_TILE - 1) // MOE_TILE
    tile_end = jnp.cumsum(tiles_g)
    tile_start = tile_end - tiles_g
    rank = jnp.sum((jnp.cumsum(oh, axis=0) - oh) * oh, axis=1)
    dest = jnp.sum(oh * tile_start[None, :], axis=1) * MOE_TILE + rank
    tile_ids = jnp.arange(n_tiles, dtype=jnp.int32)
    tile_group = jnp.minimum(jnp.sum((tile_ids[:, None] >= tile_end[None, :]).astype(jnp.int32), axis=1),
                             N_GROUPS - 1)
    t_oh = ((dest // MOE_TILE)[:, None] == tile_ids[None, :]).astype(F32)
    b_oh = ((jnp.arange(nt, dtype=jnp.int32) // MOE_TILE)[:, None]
            == jnp.arange(n_blocks, dtype=jnp.int32)[None, :]).astype(F32)
    mask = jnp.einsum('tj,tb->jb', t_oh, b_oh) > 0.5
    mask = mask.at[:, 0].set(jnp.logical_or(mask[:, 0], tile_ids >= tile_end[-1]))
    tj, tb, tf, tl, tv = _pair_list(mask, n_pairs, n_blocks)
    cb, cj, cf, cl, cv = _pair_list(mask.T, n_pairs, n_tiles)
    return MoePlan(dest.reshape(n_blocks, 1, MOE_TILE),
                   jnp.broadcast_to(dest[:, None], (nt, LANES)),
                   tile_group, (tj, tb, tf, tl, tv), (cj, cb, cf, cl, cv))


def _moe_experts_kernel(pj_ref, pb_ref, pf_ref, pl_ref, pv_ref, tg_ref,
                        h_ref, dest_ref, route_ref, w1_ref, w3_ref, w2_ref, y_ref, x_scr, g_scr):
    p = pl.program_id(0)

    @pl.when(pf_ref[p] == 1)
    def _():
        x_scr[...] = jnp.zeros(x_scr.shape, F32)
        g_scr[...] = jnp.zeros(g_scr.shape, F32)

    @pl.when(pv_ref[p] == 1)
    def _():
        row = lax.broadcasted_iota(jnp.int32, (MOE_TILE, MOE_TILE), 0) + pj_ref[p] * MOE_TILE
        sel = jnp.where(dest_ref[0] == row, 1.0, 0.0).astype(BF16)
        d = functools.partial(jnp.dot, preferred_element_type=F32)
        x_scr[...] += d(sel, h_ref[...])
        gh, gm, gl = _split3(route_ref[...])
        g_scr[...] += d(sel, gh) + (d(sel, gm) + d(sel, gl))

    @pl.when(pl_ref[p] == 1)
    def _():
        x = x_scr[...].astype(BF16)
        gate = g_scr[...]
        d = functools.partial(jnp.dot, preferred_element_type=F32)
        acc = jnp.zeros((MOE_TILE, D_MODEL), F32)
        for e in range(E_PER_GROUP):
            hid = _silu(d(x, w1_ref[0, e])) * d(x, w3_ref[0, e]) * gate[:, e:e + 1]
            acc = acc + d(hid.astype(BF16), w2_ref[0, e])
        y_ref[...] = acc.astype(BF16)


def _moe_experts(h2, route, plan: MoePlan, w1, w3, w2, l, nt):
    n_blocks = nt // MOE_TILE
    n_tiles = n_blocks + N_GROUPS
    tj, tb, tf, tl, tv = plan.by_tile
    n_pairs = tj.shape[0]
    wmap = lambda p, pj, pb, pf, pl_, pv, tg: (l * N_GROUPS + tg[pj[p]], 0, 0, 0)
    grid_spec = pltpu.PrefetchScalarGridSpec(
        num_scalar_prefetch=6,
        grid=(n_pairs,),
        in_specs=[
            pl.BlockSpec((MOE_TILE, D_MODEL), lambda p, pj, pb, *_: (pb[p], 0)),
            pl.BlockSpec((1, 1, MOE_TILE), lambda p, pj, pb, *_: (pb[p], 0, 0)),
            pl.BlockSpec((MOE_TILE, LANES), lambda p, pj, pb, *_: (pb[p], 0)),
            pl.BlockSpec((1, E_PER_GROUP, D_MODEL, D_EXPERT), wmap),
            pl.BlockSpec((1, E_PER_GROUP, D_MODEL, D_EXPERT), wmap),
            pl.BlockSpec((1, E_PER_GROUP, D_EXPERT, D_MODEL), wmap),
        ],
        out_specs=pl.BlockSpec((MOE_TILE, D_MODEL), lambda p, pj, *_: (pj[p], 0)),
        scratch_shapes=[pltpu.VMEM((MOE_TILE, D_MODEL), F32), pltpu.VMEM((MOE_TILE, LANES), F32)],
    )
    return pl.pallas_call(
        _moe_experts_kernel,
        grid_spec=grid_spec,
        out_shape=jax.ShapeDtypeStruct((n_tiles * MOE_TILE, D_MODEL), BF16),
        compiler_params=_cparams(("arbitrary",)),
        name="moe_experts",
    )(tj, tb, tf, tl, tv, plan.tile_group, h2, plan.dest_row, route, w1, w3, w2)


def _moe_combine_kernel(cj_ref, cb_ref, cf_ref, cl_ref, cv_ref,
                        y_ref, dest_ref, x1_ref, mod_ref, g2_ref, b2_ref, o_ref, acc_scr):
    p = pl.program_id(0)

    @pl.when(cf_ref[p] == 1)
    def _():
        acc_scr[...] = jnp.zeros(acc_scr.shape, F32)

    @pl.when(cv_ref[p] == 1)
    def _():
        col = lax.broadcasted_iota(jnp.int32, (MOE_TILE, MOE_TILE), 1) + cj_ref[p] * MOE_TILE
        sel = jnp.where(dest_ref[:, 0:1] == col, 1.0, 0.0).astype(BF16)
        acc_scr[...] += jnp.dot(sel, y_ref[...], preferred_element_type=F32)

    @pl.when(cl_ref[p] == 1)
    def _():
        gate2 = mod_ref[0, 5:6, :]
        o_ref[...] = _ln(DN_ALPHA * x1_ref[...] + gate2 * acc_scr[...]) * g2_ref[...] + b2_ref[...]


def _moe_combine(y, plan: MoePlan, x1, mod_l, g2, b2, rows: Rows):
    nt = rows.total
    cj, cb, cf, cl, cv = plan.by_block
    n_pairs = cj.shape[0]
    blk = lambda p, cj, cb, *_: (cb[p], 0)
    const = lambda p, *_: (0, 0)
    grid_spec = pltpu.PrefetchScalarGridSpec(
        num_scalar_prefetch=5,
        grid=(n_pairs,),
        in_specs=[
            pl.BlockSpec((MOE_TILE, D_MODEL), lambda p, cj, *_: (cj[p], 0)),
            pl.BlockSpec((MOE_TILE, LANES), blk),
            pl.BlockSpec((MOE_TILE, D_MODEL), blk),
            pl.BlockSpec((1, 6, D_MODEL), lambda p, cj, cb, *_: (_cond_index(cb[p] * MOE_TILE, rows), 0, 0)),
            pl.BlockSpec((1, D_MODEL), const),
            pl.BlockSpec((1, D_MODEL), const),
        ],
        out_specs=pl.BlockSpec((MOE_TILE, D_MODEL), blk),
        scratch_shapes=[pltpu.VMEM((MOE_TILE, D_MODEL), F32)],
    )
    return pl.pallas_call(
        _moe_combine_kernel,
        grid_spec=grid_spec,
        out_shape=jax.ShapeDtypeStruct((nt, D_MODEL), F32),
        compiler_params=_cparams(("arbitrary",)),
        name="moe_combine",
    )(cj, cb, cf, cl, cv, y, plan.dest_col, x1, mod_l, g2, b2)


def _grid_pos_embed(n_tokens):
    rows = n_tokens // GRID_W
    quarter = D_MODEL // 4
    freqs = jnp.exp(-math.log(POS_BASE) * jnp.arange(quarter, dtype=F32) / quarter)
    ar = jnp.arange(rows, dtype=F32).reshape(-1, 1) * freqs
    ac = jnp.arange(GRID_W, dtype=F32).reshape(-1, 1) * freqs
    row_half = jnp.concatenate([jnp.sin(ar), jnp.cos(ar)], axis=-1)
    col_half = jnp.concatenate([jnp.sin(ac), jnp.cos(ac)], axis=-1)
    return jnp.concatenate([jnp.repeat(row_half, GRID_W, axis=0), jnp.tile(col_half, (rows, 1))], axis=-1)


def _pack_w_in(w_in_l):
    sizes = (3 * W_MIX, W_B, W_MIX, 8, 8, W_B, W_MIX, W_MIX, W_MIX, W_MIX, 8, 8, 3 * D_MODEL)
    parts, start = [], 0
    for s in sizes:
        parts.append(w_in_l[:, start:start + s])
        start += s
    qkv_a, x_b, z_a, beta, alpha, g_b, q_c, k_c, v_c, o_c, i_c, f_c, gates = parts
    big = jnp.concatenate([gates, qkv_a, q_c, k_c, v_c, z_a, g_b, o_c], axis=1).astype(BF16)
    pad = jnp.zeros((D_MODEL, LANES - 32), F32)
    small = jnp.concatenate([x_b, beta, alpha, i_c, f_c, pad], axis=1).astype(BF16)
    return big, small


def _lane_row(vals, off):
    out = jnp.zeros((8, LANES), F32)
    for r, (v, o) in enumerate(zip(vals, off)):
        out = out.at[r, o:o + 8].set(v.reshape(-1))
    return out


def kernel(x_prompt, x_sample, state_delta, state_lru, state_mlstm_C, state_mlstm_n, state_mlstm_m, c, c_ctx,
           w_mod, b_mod, w_in, conv_a, delta_a_log, delta_dt_bias, delta_norm, conv_b_w, conv_b_b,
           lru_wa, lru_ba, lru_wx, lru_bx, lru_lambda, mlstm_bi, mlstm_bf, mlstm_norm,
           w_pa, w_pb, w_pc, w_out, ln1_g, ln1_b, ln2_g, ln2_b, w_rg, b_rg, w_re, b_re, w_e1, w_e3, w_e2):
    n_ctx, ctx_len, _ = x_prompt.shape
    n_lat, lat_len, _ = x_sample.shape
    rows = Rows(n_ctx, ctx_len, n_lat, lat_len)
    assert rows.ctx_rows % UNIT == 0 and lat_len % UNIT == 0 and UNIT % ctx_len == 0
    assert rows.ctx_rows % lat_len == 0 and ctx_len % CHUNK == 0 and n_lat <= 7
    assert rows.ctx_rows % MOE_TILE == 0 and lat_len % MOE_TILE == 0

    tm = min(1024, lat_len)
    pos = _grid_pos_embed(lat_len)
    x = _prep(x_prompt.reshape(rows.ctx_rows, D_MODEL), x_sample.reshape(n_lat * lat_len, D_MODEL), pos, rows, tm)

    cond8 = jnp.zeros((8, D_MODEL), F32).at[0].set(c_ctx).at[1:1 + n_lat].set(c)
    mod = _modulation(cond8, w_mod, b_mod).reshape(DEPTH, 8, 6, D_MODEL)

    m_bcast = jnp.broadcast_to(state_mlstm_m[..., None], state_mlstm_m.shape + (HEAD_DIM,))
    w_e1g = w_e1.astype(BF16).reshape(DEPTH * N_GROUPS, E_PER_GROUP, D_MODEL, D_EXPERT)
    w_e3g = w_e3.astype(BF16).reshape(DEPTH * N_GROUPS, E_PER_GROUP, D_MODEL, D_EXPERT)
    w_e2g = w_e2.astype(BF16).reshape(DEPTH * N_GROUPS, E_PER_GROUP, D_EXPERT, D_MODEL)

    finals = []
    for l in range(DEPTH):
        w_big, w_small = _pack_w_in(w_in[l])
        par = _lane_row([delta_a_log[l], delta_dt_bias[l], mlstm_bi[l], mlstm_bf[l]],
                        [SM_ALPHA, SM_ALPHA, SM_I, SM_F])
        proj_big, proj_f32 = _projection(x, mod[l], w_big, w_small, rows, tm, 1280)
        qkv_c = _conv_a(proj_big, conv_a, l, rows, lat_len)
        o_a, sf_a = _delta(qkv_c, proj_f32, par, state_delta, l, rows)
        y_b, hf_b = _lru(proj_f32, proj_big, conv_b_w, conv_b_b, lru_wa, lru_ba, lru_wx, lru_bx, lru_lambda,
                         state_lru, l, rows)
        h_c, cf, nf, mf = _mlstm(proj_big, proj_f32, par, state_mlstm_C, state_mlstm_n, m_bcast, l, rows)
        wr = jnp.zeros((D_MODEL, LANES), F32).at[:, :N_GROUPS].set(w_rg[l]).at[:, 64:64 + N_EXPERTS].set(w_re[l])
        br = jnp.zeros((1, LANES), F32).at[0, :N_GROUPS].set(b_rg[l]).at[0, 64:64 + N_EXPERTS].set(b_re[l])
        x1, h2, gate = _merge(x, mod[l], o_a, proj_big, y_b, h_c,
                              delta_norm[l].reshape(1, HEAD_DIM), mlstm_norm[l].reshape(1, W_MIX),
                              w_pa[l].astype(BF16), w_pb[l].astype(BF16), w_pc[l].astype(BF16),
                              w_out[l].astype(BF16), ln1_g[l].reshape(1, D_MODEL), ln1_b[l].reshape(1, D_MODEL),
                              wr, br, rows, 256)
        plan = _moe_plan(gate[:, ROUTE_GROUP_LANE].astype(jnp.int32), rows.total)
        y_moe = _moe_experts(h2, gate, plan, w_e1g, w_e3g, w_e2g, l, rows.total)
        x = _moe_combine(y_moe, plan, x1, mod[l], ln2_g[l].reshape(1, D_MODEL), ln2_b[l].reshape(1, D_MODEL), rows)
        n_ctx_blocks = rows.ctx_rows // lat_len
        lru_fin = jnp.swapaxes(hf_b[:n_ctx_blocks], 1, 2).reshape(n_ctx, 2, W_B)
        finals.append((sf_a[:n_ctx], lru_fin, cf[:n_ctx], nf[:n_ctx, :, :, 0, :], mf[:n_ctx, :, :, 0, 0]))

    new_delta, new_lru, new_mc, new_mn, new_mm = (jnp.stack([f[i] for f in finals], axis=1) for i in range(5))
    y_prompt = x[:rows.ctx_rows].reshape(n_ctx, ctx_len, D_MODEL)
    y_sample = x[rows.ctx_rows:].reshape(n_lat, lat_len, D_MODEL)
    return (y_prompt, y_sample, new_delta, new_lru, new_mc, new_mn, new_mm)
```

```python
import functools
import math
from typing import NamedTuple

import jax
import jax.numpy as jnp
from jax import lax
from jax.experimental import pallas as pl
from jax.experimental.pallas import tpu as pltpu

F32 = jnp.float32
BF16 = jnp.bfloat16

D_MODEL = 1024
DEPTH = 2
GRID_W = 64
POS_BASE = 10000.0
CONV_W = 4
LN_EPS = 1e-5
RMS_EPS = 1e-6
N_HEADS = 4
HEAD_DIM = 128
W_MIX = N_HEADS * HEAD_DIM
CHUNK = 64
W_B = 512
NB_B = 4
BW_B = W_B // NB_B
LRU_C = 8.0
N_GROUPS = 4
E_PER_GROUP = 8
N_EXPERTS = N_GROUPS * E_PER_GROUP
D_EXPERT = 256
DN_ALPHA = (2 * DEPTH) ** 0.25

LANES = 128
SUBLANES = 8
VMEM_LIMIT = 56 * 1024 * 1024

OFF_GATES, OFF_QKV, OFF_QC, OFF_KC, OFF_VC, OFF_Z, OFF_GB, OFF_OC = 0, 3072, 4608, 5120, 5632, 6144, 6656, 7168
N_BIG = OFF_OC + W_MIX
N_F32 = W_B + LANES
SM_BETA, SM_ALPHA, SM_I, SM_F = 0, 8, 16, 24
ROUTE_GROUP_LANE = 24
ROUTE_RANK_LANE = 25


class Rows(NamedTuple):
    n_ctx: int
    ctx_len: int
    n_lat: int
    lat_len: int

    @property
    def ctx_rows(self):
        return self.n_ctx * self.ctx_len

    @property
    def total(self):
        return self.ctx_rows + self.n_lat * self.lat_len


def _cparams(sem):
    return pltpu.CompilerParams(dimension_semantics=sem, vmem_limit_bytes=VMEM_LIMIT)


def _sigmoid(x):
    return 1.0 / (1.0 + jnp.exp(-x))


def _silu(x):
    return x * _sigmoid(x)


def _softplus(x):
    return jnp.maximum(x, 0.0) + jnp.log1p(jnp.exp(-jnp.abs(x)))


def _ln(x):
    mu = jnp.mean(x, axis=-1, keepdims=True)
    xc = x - mu
    var = jnp.mean(xc * xc, axis=-1, keepdims=True)
    return xc * lax.rsqrt(var + LN_EPS)


def _bdot(a, b):
    return jnp.dot(a.astype(BF16), b.astype(BF16), preferred_element_type=F32)


def _bdot_nt(a, b):
    return lax.dot_general(a.astype(BF16), b.astype(BF16), (((1,), (1,)), ((), ())), preferred_element_type=F32)


def _bdot_tn(a, b):
    return lax.dot_general(a.astype(BF16), b.astype(BF16), (((0,), (0,)), ((), ())), preferred_element_type=F32)


def _split2(a):
    hi = a.astype(BF16)
    lo = (a - hi.astype(F32)).astype(BF16)
    return hi, lo


def _split3(a):
    hi = a.astype(BF16)
    r = a - hi.astype(F32)
    mid = r.astype(BF16)
    lo = (r - mid.astype(F32)).astype(BF16)
    return hi, mid, lo


def _dot3(a, b):
    ah, al = _split2(a)
    bh, bl = _split2(b)
    d = functools.partial(jnp.dot, preferred_element_type=F32)
    return d(ah, bh) + (d(ah, bl) + d(al, bh))


def _dot_exact_lhs(m_bf16, x):
    xh, xm, xl = _split3(x)
    d = functools.partial(jnp.dot, preferred_element_type=F32)
    return d(m_bf16, xh) + (d(m_bf16, xm) + d(m_bf16, xl))


def _cond_index(row0, rows: Rows):
    return jnp.maximum(0, (row0 - rows.ctx_rows + rows.lat_len) // rows.lat_len)


def _prep_kernel(xp_ref, xs_ref, pos_ref, o_ref, *, n_ctx_tiles):
    i = pl.program_id(0)

    @pl.when(i < n_ctx_tiles)
    def _():
        o_ref[...] = xp_ref[...]

    @pl.when(i >= n_ctx_tiles)
    def _():
        o_ref[...] = xs_ref[...] + pos_ref[...]


def _prep(xp2, xs2, pos, rows: Rows, tm):
    n_ctx_tiles = rows.ctx_rows // tm
    n_tiles = rows.total // tm
    pos_tiles = rows.lat_len // tm
    return pl.pallas_call(
        functools.partial(_prep_kernel, n_ctx_tiles=n_ctx_tiles),
        grid=(n_tiles,),
        in_specs=[
            pl.BlockSpec((tm, D_MODEL), lambda i: (jnp.minimum(i, n_ctx_tiles - 1), 0)),
            pl.BlockSpec((tm, D_MODEL), lambda i: (jnp.maximum(i - n_ctx_tiles, 0), 0)),
            pl.BlockSpec((tm, D_MODEL), lambda i: (jnp.maximum(i - n_ctx_tiles, 0) % pos_tiles, 0)),
        ],
        out_specs=pl.BlockSpec((tm, D_MODEL), lambda i: (i, 0)),
        out_shape=jax.ShapeDtypeStruct((rows.total, D_MODEL), F32),
        compiler_params=_cparams(("arbitrary",)),
        name="prep",
    )(xp2, xs2, pos)


def _mod_kernel(c_ref, w_ref, b_ref, o_ref):
    o_ref[0] = _bdot(_silu(c_ref[...]), w_ref[0]) + b_ref[0]


def _modulation(cond8, w_mod, b_mod):
    tn = 1024
    n6 = 6 * D_MODEL
    return pl.pallas_call(
        _mod_kernel,
        grid=(DEPTH, n6 // tn),
        in_specs=[
            pl.BlockSpec((8, D_MODEL), lambda l, n: (0, 0)),
            pl.BlockSpec((1, D_MODEL, tn), lambda l, n: (l, 0, n)),
            pl.BlockSpec((1, 1, tn), lambda l, n: (l, 0, n)),
        ],
        out_specs=pl.BlockSpec((1, 8, tn), lambda l, n: (l, 0, n)),
        out_shape=jax.ShapeDtypeStruct((DEPTH, 8, n6), F32),
        compiler_params=_cparams(("arbitrary", "arbitrary")),
        name="modulation",
    )(cond8, w_mod, b_mod.reshape(DEPTH, 1, n6))


def _proj_kernel(x_ref, mod_ref, wb_ref, wf_ref, ob_ref, of_ref, h_scr):
    n = pl.program_id(1)

    @pl.when(n == 0)
    def _():
        shift1 = mod_ref[0, 0:1, :]
        scale1 = mod_ref[0, 1:2, :]
        h = (_ln(x_ref[...]) * (1.0 + scale1) + shift1).astype(BF16)
        h_scr[...] = h
        of_ref[...] = jnp.dot(h, wf_ref[...], preferred_element_type=F32)

    ob_ref[...] = jnp.dot(h_scr[...], wb_ref[...], preferred_element_type=F32).astype(BF16)


def _projection(x, mod_l, w_big, w_f32, rows: Rows, tm, tn):
    nt = rows.total
    return pl.pallas_call(
        _proj_kernel,
        grid=(nt // tm, N_BIG // tn),
        in_specs=[
            pl.BlockSpec((tm, D_MODEL), lambda i, n: (i, 0)),
            pl.BlockSpec((1, 6, D_MODEL), lambda i, n: (_cond_index(i * tm, rows), 0, 0)),
            pl.BlockSpec((D_MODEL, tn), lambda i, n: (0, n)),
            pl.BlockSpec((D_MODEL, N_F32), lambda i, n: (0, 0)),
        ],
        out_specs=[
            pl.BlockSpec((tm, tn), lambda i, n: (i, n)),
            pl.BlockSpec((tm, N_F32), lambda i, n: (i, 0)),
        ],
        out_shape=[
            jax.ShapeDtypeStruct((nt, N_BIG), BF16),
            jax.ShapeDtypeStruct((nt, N_F32), F32),
        ],
        scratch_shapes=[pltpu.VMEM((tm, D_MODEL), BF16)],
        compiler_params=_cparams(("arbitrary", "arbitrary")),
        name="projection",
    )(x, mod_l, w_big, w_f32)


def _conv_taps(x, w_ref, seq_len):
    n = x.shape[0]
    pos = lax.broadcasted_iota(jnp.int32, x.shape, 0) & (seq_len - 1)
    xm1 = jnp.where(pos >= 1, pltpu.roll(x, 1, 0), 0.0)
    xp1 = jnp.where(pos <= seq_len - 2, pltpu.roll(x, n - 1, 0), 0.0)
    xp2 = jnp.where(pos <= seq_len - 3, pltpu.roll(x, n - 2, 0), 0.0)
    return xm1 * w_ref[0:1, :] + x * w_ref[1:2, :] + xp1 * w_ref[2:3, :] + xp2 * w_ref[3:4, :]


def _conv_a_kernel(x_ref, w_ref, o_ref, *, rows: Rows, br):
    b = pl.program_id(0)
    j = pl.program_id(1)
    seq_len = jnp.where(b * br < rows.ctx_rows, rows.ctx_len, rows.lat_len)
    y = _silu(_conv_taps(x_ref[...].astype(F32), w_ref[0], seq_len))
    nrm = lax.rsqrt(jnp.sum(y * y, axis=-1, keepdims=True) + RMS_EPS)
    fac = jnp.where(j < N_HEADS, nrm * HEAD_DIM ** -0.5, jnp.where(j < 2 * N_HEADS, nrm, 1.0))
    o_ref[...] = (y * fac).astype(BF16)


def _conv_a(proj_big, conv_w, l, rows: Rows, br):
    nt = rows.total
    ncol = 3 * N_HEADS
    return pl.pallas_call(
        functools.partial(_conv_a_kernel, rows=rows, br=br),
        grid=(nt // br, ncol),
        in_specs=[
            pl.BlockSpec((br, LANES), lambda b, j: (b, OFF_QKV // LANES + j)),
            pl.BlockSpec((1, CONV_W, LANES), lambda b, j: (l, 0, j)),
        ],
        out_specs=pl.BlockSpec((br, LANES), lambda b, j: (b, j)),
        out_shape=jax.ShapeDtypeStruct((nt, ncol * LANES), BF16),
        compiler_params=_cparams(("arbitrary", "arbitrary")),
        name="conv_a",
    )(proj_big, conv_w)


UNIT = 1024
NCH = UNIT // CHUNK


def _unit_layout(rows: Rows):
    n_ctx_units = rows.ctx_rows // UNIT
    lat_units = rows.lat_len // UNIT
    return n_ctx_units, lat_units, n_ctx_units + rows.n_lat * lat_units


def _unit_rowblock(u, d, rows: Rows):
    n_ctx_units, lat_units, _ = _unit_layout(rows)
    v = jnp.maximum(u - n_ctx_units, 0)
    b = v // lat_units
    j = v % lat_units
    jj = j + d * (lat_units - 1 - 2 * j)
    return jnp.where(u < n_ctx_units, u, n_ctx_units + b * lat_units + jj)


def _unit_lat_seq(u, rows: Rows):
    n_ctx_units, lat_units, _ = _unit_layout(rows)
    return jnp.maximum(u - n_ctx_units, 0) // lat_units


def _dir_masks(fwd):
    ii = lax.broadcasted_iota(jnp.int32, (CHUNK, CHUNK), 0)
    jj = lax.broadcasted_iota(jnp.int32, (CHUNK, CHUNK), 1)
    s = jnp.where(fwd, ii - jj, jj - ii)
    return s >= 0, s > 0, ii == jj


def _bmm(a, b):
    return jnp.einsum('gik,gkj->gij', a.astype(BF16), b.astype(BF16), preferred_element_type=F32)


def _bmm_nt(a, b):
    return jnp.einsum('gik,gjk->gij', a.astype(BF16), b.astype(BF16), preferred_element_type=F32)


def _bmm_tn(a, b):
    return jnp.einsum('gik,gij->gkj', a.astype(BF16), b.astype(BF16), preferred_element_type=F32)


def _chunk_cumsum(x, reverse):
    n = x.shape[0]
    pos = lax.broadcasted_iota(jnp.int32, x.shape, 0) & (CHUNK - 1)
    s = 1
    while s < CHUNK:
        if reverse:
            x = x + jnp.where(pos < CHUNK - s, pltpu.roll(x, n - s, 0), 0.0)
        else:
            x = x + jnp.where(pos >= s, pltpu.roll(x, s, 0), 0.0)
        s *= 2
    return x


def _dir_select(fwd, x):
    return jnp.where(fwd, x, pltpu.roll(x, LANES - N_HEADS, 1))


GROUP_CHUNKS = 8
GROUP_ROWS = GROUP_CHUNKS * CHUNK


def _group_columns(col_arr, row_arr, lane0):
    cols, rws = [], []
    for cc in range(GROUP_CHUNKS):
        for h in range(N_HEADS):
            cols.append(jnp.broadcast_to(col_arr[cc * CHUNK:(cc + 1) * CHUNK, lane0 + h:lane0 + h + 1],
                                         (CHUNK, HEAD_DIM)))
            if row_arr is not None:
                rws.append(jnp.broadcast_to(row_arr[lane0 + h:lane0 + h + 1, cc * CHUNK:(cc + 1) * CHUNK],
                                            (CHUNK, CHUNK)))
    return jnp.stack(cols), (jnp.stack(rws) if rws else None)


def _group_heads(ref, r0, col0):
    return jnp.stack([ref[pl.ds(r0 + cc * CHUNK, CHUNK), col0 + h * HEAD_DIM:col0 + (h + 1) * HEAD_DIM]
                      for cc in range(GROUP_CHUNKS) for h in range(N_HEADS)])


def _delta_kernel(qkv_ref, sm_ref, par_ref, s0_ref, o_ref, sf_ref,
                  s_scr, sa_scr, sb_scr, oq_scr, ov_scr, ge_scr, *, rows: Rows):
    d = pl.program_id(0)
    u = pl.program_id(1)
    n_ctx_units, lat_units, _ = _unit_layout(rows)
    seq_chunks = rows.ctx_len // CHUNK
    seq_per_unit = UNIT // rows.ctx_len
    fwd = d == 0
    is_ctx = u < n_ctx_units
    causal, strict, eye = _dir_masks(fwd)
    eye_f = jnp.where(eye, 1.0, 0.0)
    ii = lax.broadcasted_iota(jnp.int32, (CHUNK, CHUNK), 0)
    jj = lax.broadcasted_iota(jnp.int32, (CHUNK, CHUNK), 1)
    pair_masks = [jnp.logical_and((ii >> (s + 1)) == (jj >> (s + 1)), (ii >> s) != (jj >> s))
                  for s in range(CHUNK.bit_length() - 1)]
    ng = GROUP_CHUNKS * N_HEADS

    def pre(it, carry):
        r0 = pl.multiple_of(it * GROUP_ROWS, GROUP_ROWS)
        g0 = pl.multiple_of(it * ng, ng)
        sm = sm_ref[pl.ds(r0, GROUP_ROWS), :]
        g_all = -jnp.exp(par_ref[0:1, :]) * _softplus(sm + par_ref[1:2, :])
        gcum = _dir_select(fwd, jnp.where(fwd, _chunk_cumsum(g_all, False), _chunk_cumsum(g_all, True)))
        bsel = _dir_select(fwd, _sigmoid(sm))
        gi, grow = _group_columns(gcum, gcum.T, SM_ALPHA)
        beta = jnp.stack([bsel[cc * CHUNK:(cc + 1) * CHUNK, SM_BETA + h:SM_BETA + h + 1]
                          for cc in range(GROUP_CHUNKS) for h in range(N_HEADS)])
        q = _group_heads(qkv_ref, r0, 0)
        k = _group_heads(qkv_ref, r0, W_MIX)
        v = _group_heads(qkv_ref, r0, 2 * W_MIX)
        kf = k.astype(F32)
        decay = jnp.exp(jnp.where(causal, gi[:, :, :CHUNK] - grow, -jnp.inf))
        lmat = jnp.where(strict, beta * _bmm_nt(k, k) * decay, 0.0)
        eg = jnp.exp(gi)
        rhs = jnp.concatenate([v.astype(F32) * beta, kf * (beta * eg)], axis=-1)
        t = eye_f - jnp.where(pair_masks[0], lmat, 0.0)
        for pm in pair_masks[1:]:
            tb = t.astype(BF16)
            t = t - _bmm(_bmm(tb, jnp.where(pm, lmat, 0.0)), tb)
        t0 = t.astype(BF16)
        mh, ml = _split2(eye_f + lmat)
        resid = eye_f - (_bmm(mh, t0) + _bmm(ml, t0))
        t1 = t0.astype(F32) + _bmm(t0, resid)
        uu = _bmm(t1, rhs)
        qk = jnp.where(causal, _bmm_nt(q, k) * decay, 0.0)
        g_end = jnp.where(fwd, gi[:, CHUNK - 1:CHUNK, :], gi[:, 0:1, :])
        u_v = uu[:, :, :HEAD_DIM]
        u_k = uu[:, :, HEAD_DIM:]
        k_dec = kf * jnp.exp(g_end - gi)
        sa_scr[pl.ds(g0, ng)] = (-_bmm_tn(k_dec, u_k)).astype(BF16)
        sb_scr[pl.ds(g0, ng)] = _bmm_tn(k_dec, u_v)
        oq_scr[pl.ds(g0, ng)] = (q.astype(F32) * eg - _bmm(qk, u_k)).astype(BF16)
        ov_scr[pl.ds(g0, ng)] = _bmm(qk, u_v)
        ge_scr[pl.ds(g0, ng)] = jnp.broadcast_to(jnp.exp(g_end), (ng, 8, HEAD_DIM))
        return carry

    lax.fori_loop(0, NCH // GROUP_CHUNKS, pre, 0)

    first_lat = jnp.logical_and(u >= n_ctx_units, (u - n_ctx_units) % lat_units == 0)

    @pl.when(first_lat)
    def _():
        s_scr[...] = s0_ref[0, 0, 0]

    @pl.when(u == n_ctx_units)
    def _():
        sf_ref[...] = jnp.zeros(sf_ref.shape, F32)

    def step(n, carry):
        c = jnp.where(fwd, n, NCH - 1 - n)
        r0 = pl.multiple_of(c * CHUNK, CHUNK)
        hs = pl.ds(pl.multiple_of(c * N_HEADS, N_HEADS), N_HEADS)

        @pl.when(jnp.logical_and(is_ctx, n % seq_chunks == 0))
        def _():
            s_scr[...] = jnp.zeros(s_scr.shape, F32)

        s = s_scr[...]
        sb = s.astype(BF16)
        s_scr[...] = s * ge_scr[hs][:, 0:1, :] + (_bmm(sa_scr[hs], sb) + sb_scr[hs])
        o = _bmm(oq_scr[hs], sb) + ov_scr[hs]
        for h in range(N_HEADS):
            o_ref[0, pl.ds(r0, CHUNK), h * HEAD_DIM:(h + 1) * HEAD_DIM] = o[h]

        @pl.when(jnp.logical_and(is_ctx, n % seq_chunks == seq_chunks - 1))
        def _():
            sf_ref[c // seq_chunks, 0] = s_scr[...]

        return carry

    lax.fori_loop(0, NCH, step, 0)


def _delta(qkv_c, proj_f32, par, state, l, rows: Rows):
    n_ctx_units, lat_units, n_units = _unit_layout(rows)
    seq_per_unit = UNIT // rows.ctx_len
    nt = rows.total
    sm_col = W_B // LANES
    kern = functools.partial(_delta_kernel, rows=rows)
    hshape = (NCH * N_HEADS, CHUNK, HEAD_DIM)
    return pl.pallas_call(
        kern,
        grid=(2, n_units),
        in_specs=[
            pl.BlockSpec((UNIT, 3 * W_MIX), lambda d, u: (_unit_rowblock(u, d, rows), 0)),
            pl.BlockSpec((UNIT, LANES), lambda d, u: (_unit_rowblock(u, d, rows), sm_col)),
            pl.BlockSpec((8, LANES), lambda d, u: (0, 0)),
            pl.BlockSpec((1, 1, 1, N_HEADS, HEAD_DIM, HEAD_DIM),
                         lambda d, u: (_unit_lat_seq(u, rows), l, d, 0, 0, 0)),
        ],
        out_specs=[
            pl.BlockSpec((1, UNIT, W_MIX), lambda d, u: (d, _unit_rowblock(u, d, rows), 0)),
            pl.BlockSpec((seq_per_unit, 1, N_HEADS, HEAD_DIM, HEAD_DIM),
                         lambda d, u: (jnp.minimum(u, n_ctx_units), d, 0, 0, 0)),
        ],
        out_shape=[
            jax.ShapeDtypeStruct((2, nt, W_MIX), F32),
            jax.ShapeDtypeStruct((rows.n_ctx + seq_per_unit, 2, N_HEADS, HEAD_DIM, HEAD_DIM), F32),
        ],
        scratch_shapes=[
            pltpu.VMEM((N_HEADS, HEAD_DIM, HEAD_DIM), F32),
            pltpu.VMEM((NCH * N_HEADS, HEAD_DIM, HEAD_DIM), BF16),
            pltpu.VMEM((NCH * N_HEADS, HEAD_DIM, HEAD_DIM), F32),
            pltpu.VMEM(hshape, BF16),
            pltpu.VMEM(hshape, F32),
            pltpu.VMEM((NCH * N_HEADS, 8, HEAD_DIM), F32),
        ],
        compiler_params=_cparams(("arbitrary", "arbitrary")),
        name="delta",
    )(qkv_c, proj_f32, par, state)


def _mlstm_kernel(big_ref, sm_ref, par_ref, c0_ref, n0_ref, m0_ref, o_ref, cf_ref, nf_ref, mf_ref,
                  c_scr, n_scr, m_scr, fi_scr, mi_scr, pv_scr, ps_scr, kv_scr, ks_scr, me_scr, *, rows: Rows):
    d = pl.program_id(0)
    u = pl.program_id(1)
    n_ctx_units, lat_units, _ = _unit_layout(rows)
    seq_chunks = rows.ctx_len // CHUNK
    fwd = d == 0
    is_ctx = u < n_ctx_units
    causal, _, _ = _dir_masks(fwd)
    qo, ko, vo = 0, W_MIX, 2 * W_MIX
    ng = GROUP_CHUNKS * N_HEADS

    def pre(it, carry):
        r0 = pl.multiple_of(it * GROUP_ROWS, GROUP_ROWS)
        gs = pl.ds(pl.multiple_of(it * ng, ng), ng)
        sm = sm_ref[pl.ds(r0, GROUP_ROWS), :]
        f_all = -_softplus(-(sm + par_ref[3:4, :]))
        fcum = _dir_select(fwd, jnp.where(fwd, _chunk_cumsum(f_all, False), _chunk_cumsum(f_all, True)))
        i_al = pltpu.roll(_dir_select(fwd, sm + par_ref[2:3, :]), SM_F - SM_I, 1)
        fi, hrow = _group_columns(fcum, (fcum - i_al).T, SM_F)
        it_b, _ = _group_columns(i_al, None, SM_F)
        q = _group_heads(big_ref, r0, qo)
        v = _group_heads(big_ref, r0, vo)
        ks = _group_heads(big_ref, r0, ko).astype(F32) * HEAD_DIM ** -0.5
        dmat = jnp.where(causal, fi[:, :, :CHUNK] - hrow, -jnp.inf)
        f_end = jnp.where(fwd, fi[:, CHUNK - 1:CHUNK, :], fi[:, 0:1, :])
        d_end = f_end - fi + it_b
        m_intra = jnp.max(dmat, axis=-1, keepdims=True)
        m_end = jnp.max(d_end, axis=1, keepdims=True)
        p_loc = jnp.exp(dmat - m_intra) * _bmm_nt(q, ks)
        kw = ks * jnp.exp(d_end - m_end)
        fi_scr[gs] = fi
        mi_scr[gs] = jnp.broadcast_to(m_intra, (ng, CHUNK, HEAD_DIM))
        pv_scr[gs] = _bmm(p_loc, v)
        ps_scr[gs] = jnp.broadcast_to(jnp.sum(p_loc, axis=-1, keepdims=True), (ng, CHUNK, HEAD_DIM))
        kv_scr[gs] = _bmm_tn(kw, v)
        ks_scr[gs] = jnp.broadcast_to(jnp.sum(kw, axis=1, keepdims=True), (ng, 8, HEAD_DIM))
        me_scr[gs] = jnp.broadcast_to(m_end, (ng, 8, HEAD_DIM))
        return carry

    lax.fori_loop(0, NCH // GROUP_CHUNKS, pre, 0)

    first_lat = jnp.logical_and(u >= n_ctx_units, (u - n_ctx_units) % lat_units == 0)

    @pl.when(first_lat)
    def _():
        c_scr[...] = c0_ref[0, 0, 0]
        for h in range(N_HEADS):
            n_scr[h] = jnp.broadcast_to(n0_ref[0, 0, 0, h:h + 1, :], (8, HEAD_DIM))
            m_scr[h] = jnp.broadcast_to(m0_ref[0, 0, 0, h:h + 1, :], (8, HEAD_DIM))

    @pl.when(u == n_ctx_units)
    def _():
        cf_ref[...] = jnp.zeros(cf_ref.shape, F32)
        nf_ref[...] = jnp.zeros(nf_ref.shape, F32)
        mf_ref[...] = jnp.zeros(mf_ref.shape, F32)

    def step(n, carry):
        c = jnp.where(fwd, n, NCH - 1 - n)
        r0 = pl.multiple_of(c * CHUNK, CHUNK)

        @pl.when(jnp.logical_and(is_ctx, n % seq_chunks == 0))
        def _():
            c_scr[...] = jnp.zeros(c_scr.shape, F32)
            n_scr[...] = jnp.zeros(n_scr.shape, F32)
            m_scr[...] = jnp.zeros(m_scr.shape, F32)

        hs = pl.ds(pl.multiple_of(c * N_HEADS, N_HEADS), N_HEADS)
        q = jnp.stack([big_ref[pl.ds(r0, CHUNK), qo + h * HEAD_DIM:qo + (h + 1) * HEAD_DIM]
                       for h in range(N_HEADS)])
        cs = c_scr[...]
        ns = n_scr[...][:, 0:1, :]
        ms = m_scr[...][:, 0:1, :]
        fi = fi_scr[hs]
        mi = mi_scr[hs]
        inter = fi + ms
        mt = jnp.maximum(inter, mi)
        w_int = jnp.exp(inter - mt)
        w_loc = jnp.exp(mi - mt)
        num = w_int * _bmm(q, cs) + w_loc * pv_scr[hs]
        qn = jnp.sum(q.astype(F32) * ns, axis=-1, keepdims=True)
        den = w_int * qn + w_loc * ps_scr[hs]
        hh = num / jnp.maximum(jnp.abs(den), jnp.exp(-mt))
        f_end = jnp.where(fwd, fi[:, CHUNK - 1:CHUNK, :], fi[:, 0:1, :])
        inter_end = f_end + ms
        m_end = me_scr[hs][:, 0:1, :]
        m_new = jnp.maximum(inter_end, m_end)
        s_int = jnp.exp(inter_end - m_new)
        s_loc = jnp.exp(m_end - m_new)
        c_scr[...] = cs * s_int[:, :, 0:1] + kv_scr[hs] * s_loc[:, :, 0:1]
        n_scr[...] = jnp.broadcast_to(ns * s_int + ks_scr[hs][:, 0:1, :] * s_loc, n_scr.shape)
        m_scr[...] = jnp.broadcast_to(m_new, m_scr.shape)
        for h in range(N_HEADS):
            o_ref[0, pl.ds(r0, CHUNK), h * HEAD_DIM:(h + 1) * HEAD_DIM] = hh[h]

        @pl.when(jnp.logical_and(is_ctx, n % seq_chunks == seq_chunks - 1))
        def _():
            sq = c // seq_chunks
            cf_ref[sq, 0] = c_scr[...]
            nf_ref[sq, 0] = n_scr[...]
            mf_ref[sq, 0] = m_scr[...]

        return carry

    lax.fori_loop(0, NCH, step, 0)


def _mlstm(proj_big, proj_f32, par, st_c, st_n8, st_m8, l, rows: Rows):
    n_ctx_units, lat_units, n_units = _unit_layout(rows)
    seq_per_unit = UNIT // rows.ctx_len
    nt = rows.total
    sm_col = W_B // LANES
    qkv_blk = OFF_QC // (3 * W_MIX)
    assert OFF_QC % (3 * W_MIX) == 0
    kern = functools.partial(_mlstm_kernel, rows=rows)
    hshape = (NCH * N_HEADS, CHUNK, HEAD_DIM)
    n_out = rows.n_ctx + seq_per_unit
    return pl.pallas_call(
        kern,
        grid=(2, n_units),
        in_specs=[
            pl.BlockSpec((UNIT, 3 * W_MIX), lambda d, u: (_unit_rowblock(u, d, rows), qkv_blk)),
            pl.BlockSpec((UNIT, LANES), lambda d, u: (_unit_rowblock(u, d, rows), sm_col)),
            pl.BlockSpec((8, LANES), lambda d, u: (0, 0)),
            pl.BlockSpec((1, 1, 1, N_HEADS, HEAD_DIM, HEAD_DIM),
                         lambda d, u: (_unit_lat_seq(u, rows), l, d, 0, 0, 0)),
            pl.BlockSpec((1, 1, 1, N_HEADS, HEAD_DIM), lambda d, u: (_unit_lat_seq(u, rows), l, d, 0, 0)),
            pl.BlockSpec((1, 1, 1, N_HEADS, HEAD_DIM), lambda d, u: (_unit_lat_seq(u, rows), l, d, 0, 0)),
        ],
        out_specs=[
            pl.BlockSpec((1, UNIT, W_MIX), lambda d, u: (d, _unit_rowblock(u, d, rows), 0)),
            pl.BlockSpec((seq_per_unit, 1, N_HEADS, HEAD_DIM, HEAD_DIM),
                         lambda d, u: (jnp.minimum(u, n_ctx_units), d, 0, 0, 0)),
            pl.BlockSpec((seq_per_unit, 1, N_HEADS, 8, HEAD_DIM),
                         lambda d, u: (jnp.minimum(u, n_ctx_units), d, 0, 0, 0)),
            pl.BlockSpec((seq_per_unit, 1, N_HEADS, 8, HEAD_DIM),
                         lambda d, u: (jnp.minimum(u, n_ctx_units), d, 0, 0, 0)),
        ],
        out_shape=[
            jax.ShapeDtypeStruct((2, nt, W_MIX), F32),
            jax.ShapeDtypeStruct((n_out, 2, N_HEADS, HEAD_DIM, HEAD_DIM), F32),
            jax.ShapeDtypeStruct((n_out, 2, N_HEADS, 8, HEAD_DIM), F32),
            jax.ShapeDtypeStruct((n_out, 2, N_HEADS, 8, HEAD_DIM), F32),
        ],
        scratch_shapes=[
            pltpu.VMEM((N_HEADS, HEAD_DIM, HEAD_DIM), F32),
            pltpu.VMEM((N_HEADS, 8, HEAD_DIM), F32),
            pltpu.VMEM((N_HEADS, 8, HEAD_DIM), F32),
            pltpu.VMEM(hshape, F32),
            pltpu.VMEM(hshape, F32),
            pltpu.VMEM(hshape, F32),
            pltpu.VMEM(hshape, F32),
            pltpu.VMEM((NCH * N_HEADS, HEAD_DIM, HEAD_DIM), F32),
            pltpu.VMEM((NCH * N_HEADS, 8, HEAD_DIM), F32),
            pltpu.VMEM((NCH * N_HEADS, 8, HEAD_DIM), F32),
        ],
        compiler_params=_cparams(("arbitrary", "arbitrary")),
        name="mlstm",
    )(proj_big, proj_f32, par, st_c, st_n8, st_m8)


def _gelu_tanh(x):
    return 0.5 * x * (1.0 + jnp.tanh(math.sqrt(2.0 / math.pi) * (x + 0.044715 * (x * x * x))))


def _tile_scan(a, b, reverse, span=None):
    rows_n = a.shape[0]
    n = rows_n if span is None else span
    row = lax.broadcasted_iota(jnp.int32, a.shape, 0) & (n - 1)
    s = 1
    while s < n:
        if reverse:
            ok = row < n - s
            a_sh = jnp.where(ok, pltpu.roll(a, rows_n - s, 0), 1.0)
            b_sh = jnp.where(ok, pltpu.roll(b, rows_n - s, 0), 0.0)
        else:
            ok = row >= s
            a_sh = jnp.where(ok, pltpu.roll(a, s, 0), 1.0)
            b_sh = jnp.where(ok, pltpu.roll(b, s, 0), 0.0)
        b = a * b_sh + b
        a = a * a_sh
        s *= 2
    return a, b


def _lru_kernel(x_ref, g_ref, cw_ref, cb_ref, wa_ref, ba_ref, wx_ref, bx_ref, lam_ref, h0_ref,
                y_ref, hf_ref, xc_scr, hfw_scr, *, rows: Rows, br):
    blk = pl.program_id(0)
    is_ctx = blk * br < rows.ctx_rows
    ts = rows.ctx_len
    n_tiles = br // ts
    seq_len = jnp.where(is_ctx, rows.ctx_len, rows.lat_len)
    xc_scr[...] = _conv_taps(x_ref[...], cw_ref[0], seq_len) + cb_ref[0]

    for dd in range(2):
        reverse = dd == 1
        sp_lam = _softplus(-lam_ref[0, dd])

        def tile(n, carry, dd=dd, reverse=reverse, sp_lam=sp_lam):
            t = (n_tiles - 1 - n) if reverse else n
            r0 = pl.multiple_of(t * ts, ts)
            xc = xc_scr[pl.ds(r0, ts), :]
            r = _sigmoid(_bdot(xc, wa_ref[0, dd, 0]) + ba_ref[0, dd])
            gi = _sigmoid(_bdot(xc, wx_ref[0, dd, 0]) + bx_ref[0, dd])
            log_a = -LRU_C * r * sp_lam
            a = jnp.exp(log_a)
            b = jnp.sqrt(-jnp.tanh(log_a) * (a * a + 1.0)) * (gi * xc)
            a_cum, h = _tile_scan(a, b, reverse)
            carry = jnp.where(is_ctx, 0.0, carry)
            h = h + a_cum * carry
            last = h[0:1, :] if reverse else h[ts - 1:ts, :]
            hf_ref[0, dd, pl.ds(t, 1), :] = last
            if reverse:
                g = g_ref[pl.ds(r0, ts), :].astype(F32)
                y_ref[pl.ds(r0, ts), :] = ((hfw_scr[pl.ds(r0, ts), :] + h) * _gelu_tanh(g)).astype(BF16)
            else:
                hfw_scr[pl.ds(r0, ts), :] = h
            return last

        lax.fori_loop(0, n_tiles, tile, h0_ref[0, 0, dd])


def _lru(proj_f32, proj_big, conv_w, conv_b, wa, ba, wx, bx, lam, h0, l, rows: Rows):
    br = rows.lat_len
    nt = rows.total
    n_blocks = nt // br
    n_ctx_blocks = rows.ctx_rows // br
    n_tiles = br // rows.ctx_len
    gcol = OFF_GB // BW_B

    def lat_seq(b):
        return jnp.maximum(b - n_ctx_blocks, 0)

    vec = lambda a: a.reshape(DEPTH, 2, 1, W_B)
    return pl.pallas_call(
        functools.partial(_lru_kernel, rows=rows, br=br),
        grid=(n_blocks, NB_B),
        in_specs=[
            pl.BlockSpec((br, BW_B), lambda b, j: (b, j)),
            pl.BlockSpec((br, BW_B), lambda b, j: (b, gcol + j)),
            pl.BlockSpec((1, CONV_W, BW_B), lambda b, j: (l, 0, j)),
            pl.BlockSpec((1, 1, BW_B), lambda b, j: (l, 0, j)),
            pl.BlockSpec((1, 2, 1, BW_B, BW_B), lambda b, j: (l, 0, j, 0, 0)),
            pl.BlockSpec((1, 2, 1, BW_B), lambda b, j: (l, 0, 0, j)),
            pl.BlockSpec((1, 2, 1, BW_B, BW_B), lambda b, j: (l, 0, j, 0, 0)),
            pl.BlockSpec((1, 2, 1, BW_B), lambda b, j: (l, 0, 0, j)),
            pl.BlockSpec((1, 2, 1, BW_B), lambda b, j: (l, 0, 0, j)),
            pl.BlockSpec((1, 1, 2, 1, BW_B), lambda b, j: (lat_seq(b), l, 0, 0, j)),
        ],
        out_specs=[
            pl.BlockSpec((br, BW_B), lambda b, j: (b, j)),
            pl.BlockSpec((1, 2, n_tiles, BW_B), lambda b, j: (b, 0, 0, j)),
        ],
        out_shape=[
            jax.ShapeDtypeStruct((nt, W_B), BF16),
            jax.ShapeDtypeStruct((n_blocks, 2, n_tiles, W_B), F32),
        ],
        scratch_shapes=[pltpu.VMEM((br, BW_B), F32), pltpu.VMEM((br, BW_B), F32)],
        compiler_params=_cparams(("arbitrary", "arbitrary")),
        name="lru",
    )(proj_f32, proj_big, conv_w, conv_b.reshape(DEPTH, 1, W_B), wa, vec(ba), wx, vec(bx), vec(lam),
      h0.reshape(h0.shape[0], DEPTH, 2, 1, W_B))


def _route(lg, le):
    lane = lax.broadcasted_iota(jnp.int32, lg.shape, 1)
    neg = -jnp.inf
    lgm = jnp.where(lane < N_GROUPS, lg, neg)
    gmax = jnp.max(lgm, axis=-1, keepdims=True)
    p_grp = 1.0 / jnp.sum(jnp.exp(lgm - gmax), axis=-1, keepdims=True)
    g_sel = jnp.min(jnp.where(lgm == gmax, lane, LANES), axis=-1, keepdims=True)
    in_grp = jnp.logical_and(lane >= g_sel * E_PER_GROUP, lane < (g_sel + 1) * E_PER_GROUP)
    lem = jnp.where(in_grp, le, neg)
    v1 = jnp.max(lem, axis=-1, keepdims=True)
    i1 = jnp.min(jnp.where(lem == v1, lane, LANES), axis=-1, keepdims=True)
    lem2 = jnp.where(lane == i1, neg, lem)
    v2 = jnp.max(lem2, axis=-1, keepdims=True)
    i2 = jnp.min(jnp.where(lem2 == v2, lane, LANES), axis=-1, keepdims=True)
    e2 = jnp.exp(v2 - v1)
    w1 = p_grp / (1.0 + e2)
    w2 = p_grp * e2 / (1.0 + e2)
    gate = jnp.where(lane == i1, w1, 0.0) + jnp.where(lane == i2, w2, 0.0)
    local = gate
    for g in range(1, N_GROUPS):
        local = jnp.where(g_sel == g, pltpu.roll(gate, LANES - g * E_PER_GROUP, 1), local)
    local = jnp.where(lane < E_PER_GROUP, local, 0.0)
    hi = local.astype(BF16).astype(F32)
    rest = local - hi
    mid = rest.astype(BF16).astype(F32)
    lo = (rest - mid).astype(BF16).astype(F32)
    packed = hi + pltpu.roll(mid, E_PER_GROUP, 1) + pltpu.roll(lo, 2 * E_PER_GROUP, 1)
    return packed, g_sel


def _group_rank(g_sel, cnt_scr):
    tm = g_sel.shape[0]
    lane = lax.broadcasted_iota(jnp.int32, (tm, LANES), 1)
    onehot = jnp.where(lane == g_sel, 1.0, 0.0)
    ii = lax.broadcasted_iota(jnp.int32, (tm, tm), 0)
    jj = lax.broadcasted_iota(jnp.int32, (tm, tm), 1)
    before = jnp.where(ii > jj, 1.0, 0.0).astype(BF16)
    seen = jnp.dot(before, onehot.astype(BF16), preferred_element_type=F32) + cnt_scr[0:1, :]
    cnt_scr[...] = cnt_scr[...] + jnp.sum(onehot, axis=0, keepdims=True)
    return jnp.sum(seen * onehot, axis=-1, keepdims=True)


def _merge_kernel(x_ref, mod_ref, oa_ref, z_ref, yb_ref, hc_ref, oc_ref, gt_ref, dn_ref, mn_ref,
                  wpa_ref, wpb_ref, wpc_ref, wout_ref, g1_ref, b1_ref, wrh_ref, wrl_ref, br_ref,
                  x1_ref, h2_ref, gate_ref, cnt_scr):
    oa = oa_ref[0] + oa_ref[1]
    hc = hc_ref[0] + hc_ref[1]
    ya, yc = [], []
    for h in range(N_HEADS):
        sl = slice(h * HEAD_DIM, (h + 1) * HEAD_DIM)
        o_h = oa[:, sl]
        o_h = o_h * lax.rsqrt(jnp.mean(o_h * o_h, axis=-1, keepdims=True) + RMS_EPS) * dn_ref[...]
        ya.append((o_h * _silu(z_ref[:, sl].astype(F32))).astype(BF16))
        c_h = _ln(hc[:, sl]) * mn_ref[:, sl]
        yc.append((_sigmoid(oc_ref[:, sl].astype(F32)) * c_h).astype(BF16))
    ya = jnp.concatenate(ya, axis=-1)
    yc = jnp.concatenate(yc, axis=-1)
    ga = _sigmoid(gt_ref[:, 0:D_MODEL].astype(F32))
    gb = _sigmoid(gt_ref[:, D_MODEL:2 * D_MODEL].astype(F32))
    gc = _sigmoid(gt_ref[:, 2 * D_MODEL:3 * D_MODEL].astype(F32))
    d = functools.partial(jnp.dot, preferred_element_type=F32)
    merged = ga * d(ya, wpa_ref[...]) + gb * d(yb_ref[...], wpb_ref[...]) + gc * d(yc, wpc_ref[...])
    mixed = d(merged.astype(BF16), wout_ref[...])
    gate1 = mod_ref[0, 2:3, :]
    shift2 = mod_ref[0, 3:4, :]
    scale2 = mod_ref[0, 4:5, :]
    x1 = _ln(DN_ALPHA * x_ref[...] + gate1 * mixed) * g1_ref[...] + b1_ref[...]
    x1_ref[...] = x1
    h2 = _ln(x1) * (1.0 + scale2) + shift2
    h2_ref[...] = h2.astype(BF16)
    hh, hl = _split2(h2)
    d = functools.partial(jnp.dot, preferred_element_type=F32)
    logits = d(hh, wrh_ref[...]) + (d(hh, wrl_ref[...]) + d(hl, wrh_ref[...])) + br_ref[...]
    packed, g_sel = _route(logits, pltpu.roll(logits, LANES - 64, 1))

    @pl.when(pl.program_id(0) == 0)
    def _():
        cnt_scr[...] = jnp.zeros(cnt_scr.shape, F32)

    rank = _group_rank(g_sel, cnt_scr)
    lane = lax.broadcasted_iota(jnp.int32, packed.shape, 1)
    gate_ref[...] = jnp.where(lane == ROUTE_GROUP_LANE, g_sel.astype(F32),
                              jnp.where(lane == ROUTE_RANK_LANE, rank, packed))


def _merge(x, mod_l, o_a, proj_big, y_b, h_c, dn, mn, wpa, wpb, wpc, wout, g1, b1, wr, br, rows: Rows, tm):
    nt = rows.total
    row = lambda i: (i, 0)
    const = lambda i: (0, 0)
    cw = W_MIX
    return pl.pallas_call(
        _merge_kernel,
        grid=(nt // tm,),
        in_specs=[
            pl.BlockSpec((tm, D_MODEL), row),
            pl.BlockSpec((1, 6, D_MODEL), lambda i: (_cond_index(i * tm, rows), 0, 0)),
            pl.BlockSpec((2, tm, cw), lambda i: (0, i, 0)),
            pl.BlockSpec((tm, cw), lambda i: (i, OFF_Z // cw)),
            pl.BlockSpec((tm, cw), row),
            pl.BlockSpec((2, tm, cw), lambda i: (0, i, 0)),
            pl.BlockSpec((tm, cw), lambda i: (i, OFF_OC // cw)),
            pl.BlockSpec((tm, 3 * D_MODEL), lambda i: (i, OFF_GATES // (3 * D_MODEL))),
            pl.BlockSpec((1, HEAD_DIM), const),
            pl.BlockSpec((1, cw), const),
            pl.BlockSpec((cw, D_MODEL), const),
            pl.BlockSpec((cw, D_MODEL), const),
            pl.BlockSpec((cw, D_MODEL), const),
            pl.BlockSpec((D_MODEL, D_MODEL), const),
            pl.BlockSpec((1, D_MODEL), const),
            pl.BlockSpec((1, D_MODEL), const),
            pl.BlockSpec((D_MODEL, LANES), const),
            pl.BlockSpec((D_MODEL, LANES), const),
            pl.BlockSpec((1, LANES), const),
        ],
        out_specs=[
            pl.BlockSpec((tm, D_MODEL), row),
            pl.BlockSpec((tm, D_MODEL), row),
            pl.BlockSpec((tm, LANES), row),
        ],
        out_shape=[
            jax.ShapeDtypeStruct((nt, D_MODEL), F32),
            jax.ShapeDtypeStruct((nt, D_MODEL), BF16),
            jax.ShapeDtypeStruct((nt, LANES), F32),
        ],
        scratch_shapes=[pltpu.VMEM((SUBLANES, LANES), F32)],
        compiler_params=_cparams(("arbitrary",)),
        name="merge",
    )(x, mod_l, o_a, proj_big, y_b, h_c, proj_big, proj_big, dn, mn, wpa, wpb, wpc, wout, g1, b1, *_split2(wr), br)


MOE_TILE = 512


class MoePlan(NamedTuple):
    dest_row: jax.Array
    dest_col: jax.Array
    tile_group: jax.Array
    by_tile: tuple
    by_block: tuple


def _pair_list(mask, n_pairs, minor):
    flat = mask.reshape(-1)
    cnt = jnp.sum(flat.astype(jnp.int32))
    idx = jnp.nonzero(flat, size=n_pairs, fill_value=0)[0].astype(jnp.int32)
    pos = jnp.arange(n_pairs, dtype=jnp.int32)
    valid = pos < cnt
    idx = jnp.where(valid, idx, idx[jnp.maximum(cnt - 1, 0)])
    major, mnr = idx // minor, idx % minor
    prev = jnp.concatenate([jnp.full((1,), -1, jnp.int32), major[:-1]])
    nxt = jnp.concatenate([major[1:], jnp.full((1,), -1, jnp.int32)])
    first = jnp.logical_and(valid, major != prev)
    last = jnp.logical_and(valid, jnp.logical_or(major != nxt, pos == cnt - 1))
    i32 = lambda a: a.astype(jnp.int32)
    return major, mnr, i32(first), i32(last), i32(valid)


def _moe_plan(g_sel, rank, nt):
    n_blocks = nt // MOE_TILE
    n_tiles = n_blocks + N_GROUPS
    n_pairs = n_tiles + N_GROUPS * n_blocks
    oh = (g_sel[:, None] == jnp.arange(N_GROUPS, dtype=jnp.int32)[None, :]).astype(jnp.int32)
    counts = jnp.sum(oh, axis=0)
    tiles_g = (counts + MOE_TILE - 1) // MOE_TILE
    tile_end = jnp.cumsum(tiles_g)
    tile_start = tile_end - tiles_g
    dest = jnp.sum(oh * tile_start[None, :], axis=1) * MOE_TILE + rank
    tile_ids = jnp.arange(n_tiles, dtype=jnp.int32)
    tile_group = jnp.minimum(jnp.sum((tile_ids[:, None] >= tile_end[None, :]).astype(jnp.int32), axis=1),
                             N_GROUPS - 1)
    t_oh = ((dest // MOE_TILE)[:, None] == tile_ids[None, :]).astype(F32)
    b_oh = ((jnp.arange(nt, dtype=jnp.int32) // MOE_TILE)[:, None]
            == jnp.arange(n_blocks, dtype=jnp.int32)[None, :]).astype(F32)
    mask = jnp.einsum('tj,tb->jb', t_oh, b_oh) > 0.5
    mask = mask.at[:, 0].set(jnp.logical_or(mask[:, 0], tile_ids >= tile_end[-1]))
    tj, tb, tf, tl, tv = _pair_list(mask, n_pairs, n_blocks)
    cb, cj, cf, cl, cv = _pair_list(mask.T, n_pairs, n_tiles)
    return MoePlan(dest.reshape(n_blocks, 1, MOE_TILE),
                   jnp.broadcast_to(dest[:, None], (nt, LANES)),
                   tile_group, (tj, tb, tf, tl, tv), (cj, cb, cf, cl, cv))


def _moe_experts_kernel(pj_ref, pb_ref, pf_ref, pl_ref, pv_ref, tg_ref,
                        h_ref, dest_ref, route_ref, w1_ref, w3_ref, w2_ref, y_ref, x_scr, g_scr):
    p = pl.program_id(0)

    @pl.when(pf_ref[p] == 1)
    def _():
        x_scr[...] = jnp.zeros(x_scr.shape, F32)
        g_scr[...] = jnp.zeros(g_scr.shape, F32)

    @pl.when(pv_ref[p] == 1)
    def _():
        row = lax.broadcasted_iota(jnp.int32, (MOE_TILE, MOE_TILE), 0) + pj_ref[p] * MOE_TILE
        sel = jnp.where(dest_ref[0] == row, 1.0, 0.0).astype(BF16)
        d = functools.partial(jnp.dot, preferred_element_type=F32)
        x_scr[...] += d(sel, h_ref[...])
        g_scr[...] += d(sel, route_ref[...].astype(BF16))

    @pl.when(pl_ref[p] == 1)
    def _():
        x = x_scr[...].astype(BF16)
        g = g_scr[...]
        gate = g + (pltpu.roll(g, LANES - E_PER_GROUP, 1) + pltpu.roll(g, LANES - 2 * E_PER_GROUP, 1))
        d = functools.partial(jnp.dot, preferred_element_type=F32)
        acc = jnp.zeros((MOE_TILE, D_MODEL), F32)
        for e in range(E_PER_GROUP):
            hid = _silu(d(x, w1_ref[0, e])) * d(x, w3_ref[0, e]) * gate[:, e:e + 1]
            acc = acc + d(hid.astype(BF16), w2_ref[0, e])
        y_ref[...] = acc.astype(BF16)


def _moe_experts(h2, route, plan: MoePlan, w1, w3, w2, l, nt):
    n_blocks = nt // MOE_TILE
    n_tiles = n_blocks + N_GROUPS
    tj, tb, tf, tl, tv = plan.by_tile
    n_pairs = tj.shape[0]
    wmap = lambda p, pj, pb, pf, pl_, pv, tg: (l * N_GROUPS + tg[pj[p]], 0, 0, 0)
    grid_spec = pltpu.PrefetchScalarGridSpec(
        num_scalar_prefetch=6,
        grid=(n_pairs,),
        in_specs=[
            pl.BlockSpec((MOE_TILE, D_MODEL), lambda p, pj, pb, *_: (pb[p], 0)),
            pl.BlockSpec((1, 1, MOE_TILE), lambda p, pj, pb, *_: (pb[p], 0, 0)),
            pl.BlockSpec((MOE_TILE, LANES), lambda p, pj, pb, *_: (pb[p], 0)),
            pl.BlockSpec((1, E_PER_GROUP, D_MODEL, D_EXPERT), wmap),
            pl.BlockSpec((1, E_PER_GROUP, D_MODEL, D_EXPERT), wmap),
            pl.BlockSpec((1, E_PER_GROUP, D_EXPERT, D_MODEL), wmap),
        ],
        out_specs=pl.BlockSpec((MOE_TILE, D_MODEL), lambda p, pj, *_: (pj[p], 0)),
        scratch_shapes=[pltpu.VMEM((MOE_TILE, D_MODEL), F32), pltpu.VMEM((MOE_TILE, LANES), F32)],
    )
    return pl.pallas_call(
        _moe_experts_kernel,
        grid_spec=grid_spec,
        out_shape=jax.ShapeDtypeStruct((n_tiles * MOE_TILE, D_MODEL), BF16),
        compiler_params=_cparams(("arbitrary",)),
        name="moe_experts",
    )(tj, tb, tf, tl, tv, plan.tile_group, h2, plan.dest_row, route, w1, w3, w2)


def _moe_combine_kernel(cj_ref, cb_ref, cf_ref, cl_ref, cv_ref,
                        y_ref, dest_ref, x1_ref, mod_ref, g2_ref, b2_ref, o_ref, acc_scr):
    p = pl.program_id(0)

    @pl.when(cf_ref[p] == 1)
    def _():
        acc_scr[...] = jnp.zeros(acc_scr.shape, F32)

    @pl.when(cv_ref[p] == 1)
    def _():
        col = lax.broadcasted_iota(jnp.int32, (MOE_TILE, MOE_TILE), 1) + cj_ref[p] * MOE_TILE
        sel = jnp.where(dest_ref[:, 0:1] == col, 1.0, 0.0).astype(BF16)
        acc_scr[...] += jnp.dot(sel, y_ref[...], preferred_element_type=F32)

    @pl.when(cl_ref[p] == 1)
    def _():
        gate2 = mod_ref[0, 5:6, :]
        o_ref[...] = _ln(DN_ALPHA * x1_ref[...] + gate2 * acc_scr[...]) * g2_ref[...] + b2_ref[...]


def _moe_combine(y, plan: MoePlan, x1, mod_l, g2, b2, rows: Rows):
    nt = rows.total
    cj, cb, cf, cl, cv = plan.by_block
    n_pairs = cj.shape[0]
    blk = lambda p, cj, cb, *_: (cb[p], 0)
    const = lambda p, *_: (0, 0)
    grid_spec = pltpu.PrefetchScalarGridSpec(
        num_scalar_prefetch=5,
        grid=(n_pairs,),
        in_specs=[
            pl.BlockSpec((MOE_TILE, D_MODEL), lambda p, cj, *_: (cj[p], 0)),
            pl.BlockSpec((MOE_TILE, LANES), blk),
            pl.BlockSpec((MOE_TILE, D_MODEL), blk),
            pl.BlockSpec((1, 6, D_MODEL), lambda p, cj, cb, *_: (_cond_index(cb[p] * MOE_TILE, rows), 0, 0)),
            pl.BlockSpec((1, D_MODEL), const),
            pl.BlockSpec((1, D_MODEL), const),
        ],
        out_specs=pl.BlockSpec((MOE_TILE, D_MODEL), blk),
        scratch_shapes=[pltpu.VMEM((MOE_TILE, D_MODEL), F32)],
    )
    return pl.pallas_call(
        _moe_combine_kernel,
        grid_spec=grid_spec,
        out_shape=jax.ShapeDtypeStruct((nt, D_MODEL), F32),
        compiler_params=_cparams(("arbitrary",)),
        name="moe_combine",
    )(cj, cb, cf, cl, cv, y, plan.dest_col, x1, mod_l, g2, b2)


def _grid_pos_embed(n_tokens):
    rows = n_tokens // GRID_W
    quarter = D_MODEL // 4
    freqs = jnp.exp(-math.log(POS_BASE) * jnp.arange(quarter, dtype=F32) / quarter)
    ar = jnp.arange(rows, dtype=F32).reshape(-1, 1) * freqs
    ac = jnp.arange(GRID_W, dtype=F32).reshape(-1, 1) * freqs
    row_half = jnp.concatenate([jnp.sin(ar), jnp.cos(ar)], axis=-1)
    col_half = jnp.concatenate([jnp.sin(ac), jnp.cos(ac)], axis=-1)
    return jnp.concatenate([jnp.repeat(row_half, GRID_W, axis=0), jnp.tile(col_half, (rows, 1))], axis=-1)


def _pack_w_in(w_in_l):
    sizes = (3 * W_MIX, W_B, W_MIX, 8, 8, W_B, W_MIX, W_MIX, W_MIX, W_MIX, 8, 8, 3 * D_MODEL)
    parts, start = [], 0
    for s in sizes:
        parts.append(w_in_l[:, start:start + s])
        start += s
    qkv_a, x_b, z_a, beta, alpha, g_b, q_c, k_c, v_c, o_c, i_c, f_c, gates = parts
    big = jnp.concatenate([gates, qkv_a, q_c, k_c, v_c, z_a, g_b, o_c], axis=1).astype(BF16)
    pad = jnp.zeros((D_MODEL, LANES - 32), F32)
    small = jnp.concatenate([x_b, beta, alpha, i_c, f_c, pad], axis=1).astype(BF16)
    return big, small


def _lane_row(vals, off):
    out = jnp.zeros((8, LANES), F32)
    for r, (v, o) in enumerate(zip(vals, off)):
        out = out.at[r, o:o + 8].set(v.reshape(-1))
    return out


def kernel(x_prompt, x_sample, state_delta, state_lru, state_mlstm_C, state_mlstm_n, state_mlstm_m, c, c_ctx,
           w_mod, b_mod, w_in, conv_a, delta_a_log, delta_dt_bias, delta_norm, conv_b_w, conv_b_b,
           lru_wa, lru_ba, lru_wx, lru_bx, lru_lambda, mlstm_bi, mlstm_bf, mlstm_norm,
           w_pa, w_pb, w_pc, w_out, ln1_g, ln1_b, ln2_g, ln2_b, w_rg, b_rg, w_re, b_re, w_e1, w_e3, w_e2):
    n_ctx, ctx_len, _ = x_prompt.shape
    n_lat, lat_len, _ = x_sample.shape
    rows = Rows(n_ctx, ctx_len, n_lat, lat_len)
    assert rows.ctx_rows % UNIT == 0 and lat_len % UNIT == 0 and UNIT % ctx_len == 0
    assert rows.ctx_rows % lat_len == 0 and ctx_len % CHUNK == 0 and n_lat <= 7
    assert rows.ctx_rows % MOE_TILE == 0 and lat_len % MOE_TILE == 0

    tm = min(1024, lat_len)
    pos = _grid_pos_embed(lat_len)
    x = _prep(x_prompt.reshape(rows.ctx_rows, D_MODEL), x_sample.reshape(n_lat * lat_len, D_MODEL), pos, rows, tm)

    cond8 = jnp.zeros((8, D_MODEL), F32).at[0].set(c_ctx).at[1:1 + n_lat].set(c)
    mod = _modulation(cond8, w_mod, b_mod).reshape(DEPTH, 8, 6, D_MODEL)

    m_bcast = jnp.broadcast_to(state_mlstm_m[..., None], state_mlstm_m.shape + (HEAD_DIM,))
    w_e1g = w_e1.astype(BF16).reshape(DEPTH * N_GROUPS, E_PER_GROUP, D_MODEL, D_EXPERT)
    w_e3g = w_e3.astype(BF16).reshape(DEPTH * N_GROUPS, E_PER_GROUP, D_MODEL, D_EXPERT)
    w_e2g = w_e2.astype(BF16).reshape(DEPTH * N_GROUPS, E_PER_GROUP, D_EXPERT, D_MODEL)

    finals = []
    for l in range(DEPTH):
        w_big, w_small = _pack_w_in(w_in[l])
        par = _lane_row([delta_a_log[l], delta_dt_bias[l], mlstm_bi[l], mlstm_bf[l]],
                        [SM_ALPHA, SM_ALPHA, SM_I, SM_F])
        proj_big, proj_f32 = _projection(x, mod[l], w_big, w_small, rows, tm, 1280)
        qkv_c = _conv_a(proj_big, conv_a, l, rows, lat_len)
        o_a, sf_a = _delta(qkv_c, proj_f32, par, state_delta, l, rows)
        y_b, hf_b = _lru(proj_f32, proj_big, conv_b_w, conv_b_b, lru_wa, lru_ba, lru_wx, lru_bx, lru_lambda,
                         state_lru, l, rows)
        h_c, cf, nf, mf = _mlstm(proj_big, proj_f32, par, state_mlstm_C, state_mlstm_n, m_bcast, l, rows)
        wr = jnp.zeros((D_MODEL, LANES), F32).at[:, :N_GROUPS].set(w_rg[l]).at[:, 64:64 + N_EXPERTS].set(w_re[l])
        br = jnp.zeros((1, LANES), F32).at[0, :N_GROUPS].set(b_rg[l]).at[0, 64:64 + N_EXPERTS].set(b_re[l])
        x1, h2, gate = _merge(x, mod[l], o_a, proj_big, y_b, h_c,
                              delta_norm[l].reshape(1, HEAD_DIM), mlstm_norm[l].reshape(1, W_MIX),
                              w_pa[l].astype(BF16), w_pb[l].astype(BF16), w_pc[l].astype(BF16),
                              w_out[l].astype(BF16), ln1_g[l].reshape(1, D_MODEL), ln1_b[l].reshape(1, D_MODEL),
                              wr, br, rows, 256)
        plan = _moe_plan(gate[:, ROUTE_GROUP_LANE].astype(jnp.int32), gate[:, ROUTE_RANK_LANE].astype(jnp.int32),
                         rows.total)
        y_moe = _moe_experts(h2, gate, plan, w_e1g, w_e3g, w_e2g, l, rows.total)
        x = _moe_combine(y_moe, plan, x1, mod[l], ln2_g[l].reshape(1, D_MODEL), ln2_b[l].reshape(1, D_MODEL), rows)
        n_ctx_blocks = rows.ctx_rows // lat_len
        lru_fin = jnp.swapaxes(hf_b[:n_ctx_blocks], 1, 2).reshape(n_ctx, 2, W_B)
        finals.append((sf_a[:n_ctx], lru_fin, cf[:n_ctx], nf[:n_ctx, :, :, 0, :], mf[:n_ctx, :, :, 0, 0]))

    new_delta, new_lru, new_mc, new_mn, new_mm = (jnp.stack([f[i] for f in finals], axis=1) for i in range(5))
    y_prompt = x[:rows.ctx_rows].reshape(n_ctx, ctx_len, D_MODEL)
    y_sample = x[rows.ctx_rows:].reshape(n_lat, lat_len, D_MODEL)
    return (y_prompt, y_sample, new_delta, new_lru, new_mc, new_mn, new_mm)
```

```python
import functools
import math
from typing import NamedTuple

import jax
import jax.numpy as jnp
from jax import lax
from jax.experimental import pallas as pl
from jax.experimental.pallas import tpu as pltpu

F32 = jnp.float32
BF16 = jnp.bfloat16

D_MODEL = 1024
DEPTH = 2
GRID_W = 64
POS_BASE = 10000.0
CONV_W = 4
LN_EPS = 1e-5
RMS_EPS = 1e-6
N_HEADS = 4
HEAD_DIM = 128
W_MIX = N_HEADS * HEAD_DIM
CHUNK = 64
W_B = 512
NB_B = 4
BW_B = W_B // NB_B
LRU_C = 8.0
N_GROUPS = 4
E_PER_GROUP = 8
N_EXPERTS = N_GROUPS * E_PER_GROUP
D_EXPERT = 256
DN_ALPHA = (2 * DEPTH) ** 0.25

LANES = 128
SUBLANES = 8
VMEM_LIMIT = 56 * 1024 * 1024

OFF_GATES, OFF_QKV, OFF_QC, OFF_KC, OFF_VC, OFF_Z, OFF_GB, OFF_OC = 0, 3072, 4608, 5120, 5632, 6144, 6656, 7168
N_BIG = OFF_OC + W_MIX
N_F32 = W_B + LANES
SM_BETA, SM_ALPHA, SM_I, SM_F = 0, 8, 16, 24
ROUTE_GROUP_LANE = 24
ROUTE_RANK_LANE = 25


class Rows(NamedTuple):
    n_ctx: int
    ctx_len: int
    n_lat: int
    lat_len: int

    @property
    def ctx_rows(self):
        return self.n_ctx * self.ctx_len

    @property
    def total(self):
        return self.ctx_rows + self.n_lat * self.lat_len


def _cparams(sem):
    return pltpu.CompilerParams(dimension_semantics=sem, vmem_limit_bytes=VMEM_LIMIT)


def _sigmoid(x):
    return 1.0 / (1.0 + jnp.exp(-x))


def _silu(x):
    return x * _sigmoid(x)


def _softplus(x):
    return jnp.maximum(x, 0.0) + jnp.log1p(jnp.exp(-jnp.abs(x)))


def _ln(x):
    mu = jnp.mean(x, axis=-1, keepdims=True)
    xc = x - mu
    var = jnp.mean(xc * xc, axis=-1, keepdims=True)
    return xc * lax.rsqrt(var + LN_EPS)


def _bdot(a, b):
    return jnp.dot(a.astype(BF16), b.astype(BF16), preferred_element_type=F32)


def _bdot_nt(a, b):
    return lax.dot_general(a.astype(BF16), b.astype(BF16), (((1,), (1,)), ((), ())), preferred_element_type=F32)


def _bdot_tn(a, b):
    return lax.dot_general(a.astype(BF16), b.astype(BF16), (((0,), (0,)), ((), ())), preferred_element_type=F32)


def _split2(a):
    hi = a.astype(BF16)
    lo = (a - hi.astype(F32)).astype(BF16)
    return hi, lo


def _split3(a):
    hi = a.astype(BF16)
    r = a - hi.astype(F32)
    mid = r.astype(BF16)
    lo = (r - mid.astype(F32)).astype(BF16)
    return hi, mid, lo


def _dot3(a, b):
    ah, al = _split2(a)
    bh, bl = _split2(b)
    d = functools.partial(jnp.dot, preferred_element_type=F32)
    return d(ah, bh) + (d(ah, bl) + d(al, bh))


def _dot_exact_lhs(m_bf16, x):
    xh, xm, xl = _split3(x)
    d = functools.partial(jnp.dot, preferred_element_type=F32)
    return d(m_bf16, xh) + (d(m_bf16, xm) + d(m_bf16, xl))


def _cond_index(row0, rows: Rows):
    return jnp.maximum(0, (row0 - rows.ctx_rows + rows.lat_len) // rows.lat_len)


def _prep_kernel(xp_ref, xs_ref, pos_ref, o_ref, *, n_ctx_tiles):
    i = pl.program_id(0)

    @pl.when(i < n_ctx_tiles)
    def _():
        o_ref[...] = xp_ref[...]

    @pl.when(i >= n_ctx_tiles)
    def _():
        o_ref[...] = xs_ref[...] + pos_ref[...]


def _prep(xp2, xs2, pos, rows: Rows, tm):
    n_ctx_tiles = rows.ctx_rows // tm
    n_tiles = rows.total // tm
    pos_tiles = rows.lat_len // tm
    return pl.pallas_call(
        functools.partial(_prep_kernel, n_ctx_tiles=n_ctx_tiles),
        grid=(n_tiles,),
        in_specs=[
            pl.BlockSpec((tm, D_MODEL), lambda i: (jnp.minimum(i, n_ctx_tiles - 1), 0)),
            pl.BlockSpec((tm, D_MODEL), lambda i: (jnp.maximum(i - n_ctx_tiles, 0), 0)),
            pl.BlockSpec((tm, D_MODEL), lambda i: (jnp.maximum(i - n_ctx_tiles, 0) % pos_tiles, 0)),
        ],
        out_specs=pl.BlockSpec((tm, D_MODEL), lambda i: (i, 0)),
        out_shape=jax.ShapeDtypeStruct((rows.total, D_MODEL), F32),
        compiler_params=_cparams(("arbitrary",)),
        name="prep",
    )(xp2, xs2, pos)


def _mod_kernel(c_ref, w_ref, b_ref, o_ref):
    o_ref[0] = _bdot(_silu(c_ref[...]), w_ref[0]) + b_ref[0]


def _modulation(cond8, w_mod, b_mod):
    tn = 1024
    n6 = 6 * D_MODEL
    return pl.pallas_call(
        _mod_kernel,
        grid=(DEPTH, n6 // tn),
        in_specs=[
            pl.BlockSpec((8, D_MODEL), lambda l, n: (0, 0)),
            pl.BlockSpec((1, D_MODEL, tn), lambda l, n: (l, 0, n)),
            pl.BlockSpec((1, 1, tn), lambda l, n: (l, 0, n)),
        ],
        out_specs=pl.BlockSpec((1, 8, tn), lambda l, n: (l, 0, n)),
        out_shape=jax.ShapeDtypeStruct((DEPTH, 8, n6), F32),
        compiler_params=_cparams(("arbitrary", "arbitrary")),
        name="modulation",
    )(cond8, w_mod, b_mod.reshape(DEPTH, 1, n6))


def _proj_kernel(x_ref, mod_ref, wb_ref, wf_ref, ob_ref, of_ref, h_scr):
    n = pl.program_id(1)

    @pl.when(n == 0)
    def _():
        shift1 = mod_ref[0, 0:1, :]
        scale1 = mod_ref[0, 1:2, :]
        h = (_ln(x_ref[...]) * (1.0 + scale1) + shift1).astype(BF16)
        h_scr[...] = h
        of_ref[...] = jnp.dot(h, wf_ref[...], preferred_element_type=F32)

    ob_ref[...] = jnp.dot(h_scr[...], wb_ref[...], preferred_element_type=F32).astype(BF16)


def _projection(x, mod_l, w_big, w_f32, rows: Rows, tm, tn):
    nt = rows.total
    return pl.pallas_call(
        _proj_kernel,
        grid=(nt // tm, N_BIG // tn),
        in_specs=[
            pl.BlockSpec((tm, D_MODEL), lambda i, n: (i, 0)),
            pl.BlockSpec((1, 6, D_MODEL), lambda i, n: (_cond_index(i * tm, rows), 0, 0)),
            pl.BlockSpec((D_MODEL, tn), lambda i, n: (0, n)),
            pl.BlockSpec((D_MODEL, N_F32), lambda i, n: (0, 0)),
        ],
        out_specs=[
            pl.BlockSpec((tm, tn), lambda i, n: (i, n)),
            pl.BlockSpec((tm, N_F32), lambda i, n: (i, 0)),
        ],
        out_shape=[
            jax.ShapeDtypeStruct((nt, N_BIG), BF16),
            jax.ShapeDtypeStruct((nt, N_F32), F32),
        ],
        scratch_shapes=[pltpu.VMEM((tm, D_MODEL), BF16)],
        compiler_params=_cparams(("arbitrary", "arbitrary")),
        name="projection",
    )(x, mod_l, w_big, w_f32)


def _conv_taps(x, w_ref, seq_len):
    n = x.shape[0]
    pos = lax.broadcasted_iota(jnp.int32, x.shape, 0) & (seq_len - 1)
    xm1 = jnp.where(pos >= 1, pltpu.roll(x, 1, 0), 0.0)
    xp1 = jnp.where(pos <= seq_len - 2, pltpu.roll(x, n - 1, 0), 0.0)
    xp2 = jnp.where(pos <= seq_len - 3, pltpu.roll(x, n - 2, 0), 0.0)
    return xm1 * w_ref[0:1, :] + x * w_ref[1:2, :] + xp1 * w_ref[2:3, :] + xp2 * w_ref[3:4, :]


def _conv_a_kernel(x_ref, w_ref, o_ref, *, rows: Rows, br):
    b = pl.program_id(0)
    j = pl.program_id(1)
    seq_len = jnp.where(b * br < rows.ctx_rows, rows.ctx_len, rows.lat_len)
    y = _silu(_conv_taps(x_ref[...].astype(F32), w_ref[0], seq_len))
    nrm = lax.rsqrt(jnp.sum(y * y, axis=-1, keepdims=True) + RMS_EPS)
    fac = jnp.where(j < N_HEADS, nrm * HEAD_DIM ** -0.5, jnp.where(j < 2 * N_HEADS, nrm, 1.0))
    o_ref[...] = (y * fac).astype(BF16)


def _conv_a(proj_big, conv_w, l, rows: Rows, br):
    nt = rows.total
    ncol = 3 * N_HEADS
    return pl.pallas_call(
        functools.partial(_conv_a_kernel, rows=rows, br=br),
        grid=(nt // br, ncol),
        in_specs=[
            pl.BlockSpec((br, LANES), lambda b, j: (b, OFF_QKV // LANES + j)),
            pl.BlockSpec((1, CONV_W, LANES), lambda b, j: (l, 0, j)),
        ],
        out_specs=pl.BlockSpec((br, LANES), lambda b, j: (b, j)),
        out_shape=jax.ShapeDtypeStruct((nt, ncol * LANES), BF16),
        compiler_params=_cparams(("arbitrary", "arbitrary")),
        name="conv_a",
    )(proj_big, conv_w)


UNIT = 1024
NCH = UNIT // CHUNK


def _unit_layout(rows: Rows):
    n_ctx_units = rows.ctx_rows // UNIT
    lat_units = rows.lat_len // UNIT
    return n_ctx_units, lat_units, n_ctx_units + rows.n_lat * lat_units


def _unit_rowblock(u, d, rows: Rows):
    n_ctx_units, lat_units, _ = _unit_layout(rows)
    v = jnp.maximum(u - n_ctx_units, 0)
    b = v // lat_units
    j = v % lat_units
    jj = j + d * (lat_units - 1 - 2 * j)
    return jnp.where(u < n_ctx_units, u, n_ctx_units + b * lat_units + jj)


def _unit_lat_seq(u, rows: Rows):
    n_ctx_units, lat_units, _ = _unit_layout(rows)
    return jnp.maximum(u - n_ctx_units, 0) // lat_units


def _dir_masks(fwd):
    ii = lax.broadcasted_iota(jnp.int32, (CHUNK, CHUNK), 0)
    jj = lax.broadcasted_iota(jnp.int32, (CHUNK, CHUNK), 1)
    s = jnp.where(fwd, ii - jj, jj - ii)
    return s >= 0, s > 0, ii == jj


def _bmm(a, b):
    return jnp.einsum('gik,gkj->gij', a.astype(BF16), b.astype(BF16), preferred_element_type=F32)


def _bmm_nt(a, b):
    return jnp.einsum('gik,gjk->gij', a.astype(BF16), b.astype(BF16), preferred_element_type=F32)


def _bmm_tn(a, b):
    return jnp.einsum('gik,gij->gkj', a.astype(BF16), b.astype(BF16), preferred_element_type=F32)


def _chunk_cumsum(x, reverse):
    n = x.shape[0]
    pos = lax.broadcasted_iota(jnp.int32, x.shape, 0) & (CHUNK - 1)
    s = 1
    while s < CHUNK:
        if reverse:
            x = x + jnp.where(pos < CHUNK - s, pltpu.roll(x, n - s, 0), 0.0)
        else:
            x = x + jnp.where(pos >= s, pltpu.roll(x, s, 0), 0.0)
        s *= 2
    return x


def _dir_select(fwd, x):
    return jnp.where(fwd, x, pltpu.roll(x, LANES - N_HEADS, 1))


GROUP_CHUNKS = 8
GROUP_ROWS = GROUP_CHUNKS * CHUNK


def _group_columns(col_arr, row_arr, lane0):
    cols, rws = [], []
    for cc in range(GROUP_CHUNKS):
        for h in range(N_HEADS):
            cols.append(jnp.broadcast_to(col_arr[cc * CHUNK:(cc + 1) * CHUNK, lane0 + h:lane0 + h + 1],
                                         (CHUNK, HEAD_DIM)))
            if row_arr is not None:
                rws.append(jnp.broadcast_to(row_arr[lane0 + h:lane0 + h + 1, cc * CHUNK:(cc + 1) * CHUNK],
                                            (CHUNK, CHUNK)))
    return jnp.stack(cols), (jnp.stack(rws) if rws else None)


def _group_heads(ref, r0, col0):
    return jnp.stack([ref[pl.ds(r0 + cc * CHUNK, CHUNK), col0 + h * HEAD_DIM:col0 + (h + 1) * HEAD_DIM]
                      for cc in range(GROUP_CHUNKS) for h in range(N_HEADS)])


def _delta_kernel(qkv_ref, sm_ref, par_ref, s0_ref, o_ref, sf_ref,
                  s_scr, sa_scr, sb_scr, oq_scr, ov_scr, ge_scr, *, rows: Rows):
    d = pl.program_id(0)
    u = pl.program_id(1)
    n_ctx_units, lat_units, _ = _unit_layout(rows)
    seq_chunks = rows.ctx_len // CHUNK
    seq_per_unit = UNIT // rows.ctx_len
    fwd = d == 0
    is_ctx = u < n_ctx_units
    causal, strict, eye = _dir_masks(fwd)
    eye_f = jnp.where(eye, 1.0, 0.0)
    ii = lax.broadcasted_iota(jnp.int32, (CHUNK, CHUNK), 0)
    jj = lax.broadcasted_iota(jnp.int32, (CHUNK, CHUNK), 1)
    pair_masks = [jnp.logical_and((ii >> (s + 1)) == (jj >> (s + 1)), (ii >> s) != (jj >> s))
                  for s in range(CHUNK.bit_length() - 1)]
    ng = GROUP_CHUNKS * N_HEADS

    def pre(it, carry):
        r0 = pl.multiple_of(it * GROUP_ROWS, GROUP_ROWS)
        g0 = pl.multiple_of(it * ng, ng)
        sm = sm_ref[pl.ds(r0, GROUP_ROWS), :]
        g_all = -jnp.exp(par_ref[0:1, :]) * _softplus(sm + par_ref[1:2, :])
        gcum = _dir_select(fwd, jnp.where(fwd, _chunk_cumsum(g_all, False), _chunk_cumsum(g_all, True)))
        bsel = _dir_select(fwd, _sigmoid(sm))
        gi, grow = _group_columns(gcum, gcum.T, SM_ALPHA)
        beta = jnp.stack([bsel[cc * CHUNK:(cc + 1) * CHUNK, SM_BETA + h:SM_BETA + h + 1]
                          for cc in range(GROUP_CHUNKS) for h in range(N_HEADS)])
        q = _group_heads(qkv_ref, r0, 0)
        k = _group_heads(qkv_ref, r0, W_MIX)
        v = _group_heads(qkv_ref, r0, 2 * W_MIX)
        kf = k.astype(F32)
        decay = jnp.exp(jnp.where(causal, gi[:, :, :CHUNK] - grow, -jnp.inf))
        lmat = jnp.where(strict, beta * _bmm_nt(k, k) * decay, 0.0)
        eg = jnp.exp(gi)
        rhs = jnp.concatenate([v.astype(F32) * beta, kf * (beta * eg)], axis=-1)
        t = eye_f - jnp.where(pair_masks[0], lmat, 0.0)
        for pm in pair_masks[1:]:
            tb = t.astype(BF16)
            t = t - _bmm(_bmm(tb, jnp.where(pm, lmat, 0.0)), tb)
        t0 = t.astype(BF16)
        mh, ml = _split2(eye_f + lmat)
        resid = eye_f - (_bmm(mh, t0) + _bmm(ml, t0))
        t1 = t0.astype(F32) + _bmm(t0, resid)
        uu = _bmm(t1, rhs)
        qk = jnp.where(causal, _bmm_nt(q, k) * decay, 0.0)
        g_end = jnp.where(fwd, gi[:, CHUNK - 1:CHUNK, :], gi[:, 0:1, :])
        u_v = uu[:, :, :HEAD_DIM]
        u_k = uu[:, :, HEAD_DIM:]
        k_dec = kf * jnp.exp(g_end - gi)
        sa_scr[pl.ds(g0, ng)] = (-_bmm_tn(k_dec, u_k)).astype(BF16)
        sb_scr[pl.ds(g0, ng)] = _bmm_tn(k_dec, u_v)
        oq_scr[pl.ds(g0, ng)] = (q.astype(F32) * eg - _bmm(qk, u_k)).astype(BF16)
        ov_scr[pl.ds(g0, ng)] = _bmm(qk, u_v)
        ge_scr[pl.ds(g0, ng)] = jnp.broadcast_to(jnp.exp(g_end), (ng, 8, HEAD_DIM))
        return carry

    lax.fori_loop(0, NCH // GROUP_CHUNKS, pre, 0)

    first_lat = jnp.logical_and(u >= n_ctx_units, (u - n_ctx_units) % lat_units == 0)

    @pl.when(first_lat)
    def _():
        s_scr[...] = s0_ref[0, 0, 0]

    def step(n, carry):
        c = jnp.where(fwd, n, NCH - 1 - n)
        r0 = pl.multiple_of(c * CHUNK, CHUNK)
        hs = pl.ds(pl.multiple_of(c * N_HEADS, N_HEADS), N_HEADS)

        @pl.when(jnp.logical_and(is_ctx, n % seq_chunks == 0))
        def _():
            s_scr[...] = jnp.zeros(s_scr.shape, F32)

        s = s_scr[...]
        sb = s.astype(BF16)
        s_scr[...] = s * ge_scr[hs][:, 0:1, :] + (_bmm(sa_scr[hs], sb) + sb_scr[hs])
        o = _bmm(oq_scr[hs], sb) + ov_scr[hs]
        for h in range(N_HEADS):
            o_ref[0, pl.ds(r0, CHUNK), h * HEAD_DIM:(h + 1) * HEAD_DIM] = o[h]

        @pl.when(jnp.logical_and(is_ctx, n % seq_chunks == seq_chunks - 1))
        def _():
            sf_ref[c // seq_chunks, 0] = s_scr[...]

        return carry

    lax.fori_loop(0, NCH, step, 0)


def _delta(qkv_c, proj_f32, par, state, l, rows: Rows):
    n_ctx_units, lat_units, n_units = _unit_layout(rows)
    seq_per_unit = UNIT // rows.ctx_len
    nt = rows.total
    sm_col = W_B // LANES
    kern = functools.partial(_delta_kernel, rows=rows)
    hshape = (NCH * N_HEADS, CHUNK, HEAD_DIM)
    return pl.pallas_call(
        kern,
        grid=(2, n_units),
        in_specs=[
            pl.BlockSpec((UNIT, 3 * W_MIX), lambda d, u: (_unit_rowblock(u, d, rows), 0)),
            pl.BlockSpec((UNIT, LANES), lambda d, u: (_unit_rowblock(u, d, rows), sm_col)),
            pl.BlockSpec((8, LANES), lambda d, u: (0, 0)),
            pl.BlockSpec((1, 1, 1, N_HEADS, HEAD_DIM, HEAD_DIM),
                         lambda d, u: (_unit_lat_seq(u, rows), l, d, 0, 0, 0)),
        ],
        out_specs=[
            pl.BlockSpec((1, UNIT, W_MIX), lambda d, u: (d, _unit_rowblock(u, d, rows), 0)),
            pl.BlockSpec((seq_per_unit, 1, N_HEADS, HEAD_DIM, HEAD_DIM),
                         lambda d, u: (jnp.minimum(u, n_ctx_units - 1), d, 0, 0, 0)),
        ],
        out_shape=[
            jax.ShapeDtypeStruct((2, nt, W_MIX), F32),
            jax.ShapeDtypeStruct((rows.n_ctx, 2, N_HEADS, HEAD_DIM, HEAD_DIM), F32),
        ],
        scratch_shapes=[
            pltpu.VMEM((N_HEADS, HEAD_DIM, HEAD_DIM), F32),
            pltpu.VMEM((NCH * N_HEADS, HEAD_DIM, HEAD_DIM), BF16),
            pltpu.VMEM((NCH * N_HEADS, HEAD_DIM, HEAD_DIM), F32),
            pltpu.VMEM(hshape, BF16),
            pltpu.VMEM(hshape, F32),
            pltpu.VMEM((NCH * N_HEADS, 8, HEAD_DIM), F32),
        ],
        compiler_params=_cparams(("arbitrary", "arbitrary")),
        name="delta",
    )(qkv_c, proj_f32, par, state)


def _mlstm_kernel(big_ref, sm_ref, par_ref, c0_ref, n0_ref, m0_ref, o_ref, cf_ref, nf_ref, mf_ref,
                  c_scr, n_scr, m_scr, fi_scr, mi_scr, pv_scr, ps_scr, kv_scr, ks_scr, me_scr, *, rows: Rows):
    d = pl.program_id(0)
    u = pl.program_id(1)
    n_ctx_units, lat_units, _ = _unit_layout(rows)
    seq_chunks = rows.ctx_len // CHUNK
    fwd = d == 0
    is_ctx = u < n_ctx_units
    causal, _, _ = _dir_masks(fwd)
    qo, ko, vo = 0, W_MIX, 2 * W_MIX
    ng = GROUP_CHUNKS * N_HEADS

    def pre(it, carry):
        r0 = pl.multiple_of(it * GROUP_ROWS, GROUP_ROWS)
        gs = pl.ds(pl.multiple_of(it * ng, ng), ng)
        sm = sm_ref[pl.ds(r0, GROUP_ROWS), :]
        f_all = -_softplus(-(sm + par_ref[3:4, :]))
        fcum = _dir_select(fwd, jnp.where(fwd, _chunk_cumsum(f_all, False), _chunk_cumsum(f_all, True)))
        i_al = pltpu.roll(_dir_select(fwd, sm + par_ref[2:3, :]), SM_F - SM_I, 1)
        fi, hrow = _group_columns(fcum, (fcum - i_al).T, SM_F)
        it_b, _ = _group_columns(i_al, None, SM_F)
        q = _group_heads(big_ref, r0, qo)
        v = _group_heads(big_ref, r0, vo)
        ks = _group_heads(big_ref, r0, ko).astype(F32) * HEAD_DIM ** -0.5
        dmat = jnp.where(causal, fi[:, :, :CHUNK] - hrow, -jnp.inf)
        f_end = jnp.where(fwd, fi[:, CHUNK - 1:CHUNK, :], fi[:, 0:1, :])
        d_end = f_end - fi + it_b
        m_intra = jnp.max(dmat, axis=-1, keepdims=True)
        m_end = jnp.max(d_end, axis=1, keepdims=True)
        p_loc = jnp.exp(dmat - m_intra) * _bmm_nt(q, ks)
        kw = ks * jnp.exp(d_end - m_end)
        fi_scr[gs] = fi
        mi_scr[gs] = jnp.broadcast_to(m_intra, (ng, CHUNK, HEAD_DIM))
        pv_scr[gs] = _bmm(p_loc, v)
        ps_scr[gs] = jnp.broadcast_to(jnp.sum(p_loc, axis=-1, keepdims=True), (ng, CHUNK, HEAD_DIM))
        kv_scr[gs] = _bmm_tn(kw, v)
        ks_scr[gs] = jnp.broadcast_to(jnp.sum(kw, axis=1, keepdims=True), (ng, 8, HEAD_DIM))
        me_scr[gs] = jnp.broadcast_to(m_end, (ng, 8, HEAD_DIM))
        return carry

    lax.fori_loop(0, NCH // GROUP_CHUNKS, pre, 0)

    first_lat = jnp.logical_and(u >= n_ctx_units, (u - n_ctx_units) % lat_units == 0)

    @pl.when(first_lat)
    def _():
        c_scr[...] = c0_ref[0, 0, 0]
        for h in range(N_HEADS):
            n_scr[h] = jnp.broadcast_to(n0_ref[0, 0, 0, h:h + 1, :], (8, HEAD_DIM))
            m_scr[h] = jnp.broadcast_to(m0_ref[0, 0, 0, h:h + 1, :], (8, HEAD_DIM))

    def step(n, carry):
        c = jnp.where(fwd, n, NCH - 1 - n)
        r0 = pl.multiple_of(c * CHUNK, CHUNK)

        @pl.when(jnp.logical_and(is_ctx, n % seq_chunks == 0))
        def _():
            c_scr[...] = jnp.zeros(c_scr.shape, F32)
            n_scr[...] = jnp.zeros(n_scr.shape, F32)
            m_scr[...] = jnp.zeros(m_scr.shape, F32)

        hs = pl.ds(pl.multiple_of(c * N_HEADS, N_HEADS), N_HEADS)
        q = jnp.stack([big_ref[pl.ds(r0, CHUNK), qo + h * HEAD_DIM:qo + (h + 1) * HEAD_DIM]
                       for h in range(N_HEADS)])
        cs = c_scr[...]
        ns = n_scr[...][:, 0:1, :]
        ms = m_scr[...][:, 0:1, :]
        fi = fi_scr[hs]
        mi = mi_scr[hs]
        inter = fi + ms
        mt = jnp.maximum(inter, mi)
        w_int = jnp.exp(inter - mt)
        w_loc = jnp.exp(mi - mt)
        num = w_int * _bmm(q, cs) + w_loc * pv_scr[hs]
        qn = jnp.sum(q.astype(F32) * ns, axis=-1, keepdims=True)
        den = w_int * qn + w_loc * ps_scr[hs]
        hh = num / jnp.maximum(jnp.abs(den), jnp.exp(-mt))
        f_end = jnp.where(fwd, fi[:, CHUNK - 1:CHUNK, :], fi[:, 0:1, :])
        inter_end = f_end + ms
        m_end = me_scr[hs][:, 0:1, :]
        m_new = jnp.maximum(inter_end, m_end)
        s_int = jnp.exp(inter_end - m_new)
        s_loc = jnp.exp(m_end - m_new)
        c_scr[...] = cs * s_int[:, :, 0:1] + kv_scr[hs] * s_loc[:, :, 0:1]
        n_scr[...] = jnp.broadcast_to(ns * s_int + ks_scr[hs][:, 0:1, :] * s_loc, n_scr.shape)
        m_scr[...] = jnp.broadcast_to(m_new, m_scr.shape)
        for h in range(N_HEADS):
            o_ref[0, pl.ds(r0, CHUNK), h * HEAD_DIM:(h + 1) * HEAD_DIM] = hh[h]

        @pl.when(jnp.logical_and(is_ctx, n % seq_chunks == seq_chunks - 1))
        def _():
            sq = c // seq_chunks
            cf_ref[sq, 0] = c_scr[...]
            nf_ref[sq, 0] = n_scr[...]
            mf_ref[sq, 0] = m_scr[...]

        return carry

    lax.fori_loop(0, NCH, step, 0)


def _mlstm(proj_big, proj_f32, par, st_c, st_n8, st_m8, l, rows: Rows):
    n_ctx_units, lat_units, n_units = _unit_layout(rows)
    seq_per_unit = UNIT // rows.ctx_len
    nt = rows.total
    sm_col = W_B // LANES
    qkv_blk = OFF_QC // (3 * W_MIX)
    assert OFF_QC % (3 * W_MIX) == 0
    kern = functools.partial(_mlstm_kernel, rows=rows)
    hshape = (NCH * N_HEADS, CHUNK, HEAD_DIM)
    n_out = rows.n_ctx
    return pl.pallas_call(
        kern,
        grid=(2, n_units),
        in_specs=[
            pl.BlockSpec((UNIT, 3 * W_MIX), lambda d, u: (_unit_rowblock(u, d, rows), qkv_blk)),
            pl.BlockSpec((UNIT, LANES), lambda d, u: (_unit_rowblock(u, d, rows), sm_col)),
            pl.BlockSpec((8, LANES), lambda d, u: (0, 0)),
            pl.BlockSpec((1, 1, 1, N_HEADS, HEAD_DIM, HEAD_DIM),
                         lambda d, u: (_unit_lat_seq(u, rows), l, d, 0, 0, 0)),
            pl.BlockSpec((1, 1, 1, N_HEADS, HEAD_DIM), lambda d, u: (_unit_lat_seq(u, rows), l, d, 0, 0)),
            pl.BlockSpec((1, 1, 1, N_HEADS, HEAD_DIM), lambda d, u: (_unit_lat_seq(u, rows), l, d, 0, 0)),
        ],
        out_specs=[
            pl.BlockSpec((1, UNIT, W_MIX), lambda d, u: (d, _unit_rowblock(u, d, rows), 0)),
            pl.BlockSpec((seq_per_unit, 1, N_HEADS, HEAD_DIM, HEAD_DIM),
                         lambda d, u: (jnp.minimum(u, n_ctx_units - 1), d, 0, 0, 0)),
            pl.BlockSpec((seq_per_unit, 1, N_HEADS, 8, HEAD_DIM),
                         lambda d, u: (jnp.minimum(u, n_ctx_units - 1), d, 0, 0, 0)),
            pl.BlockSpec((seq_per_unit, 1, N_HEADS, 8, HEAD_DIM),
                         lambda d, u: (jnp.minimum(u, n_ctx_units - 1), d, 0, 0, 0)),
        ],
        out_shape=[
            jax.ShapeDtypeStruct((2, nt, W_MIX), F32),
            jax.ShapeDtypeStruct((n_out, 2, N_HEADS, HEAD_DIM, HEAD_DIM), F32),
            jax.ShapeDtypeStruct((n_out, 2, N_HEADS, 8, HEAD_DIM), F32),
            jax.ShapeDtypeStruct((n_out, 2, N_HEADS, 8, HEAD_DIM), F32),
        ],
        scratch_shapes=[
            pltpu.VMEM((N_HEADS, HEAD_DIM, HEAD_DIM), F32),
            pltpu.VMEM((N_HEADS, 8, HEAD_DIM), F32),
            pltpu.VMEM((N_HEADS, 8, HEAD_DIM), F32),
            pltpu.VMEM(hshape, F32),
            pltpu.VMEM(hshape, F32),
            pltpu.VMEM(hshape, F32),
            pltpu.VMEM(hshape, F32),
            pltpu.VMEM((NCH * N_HEADS, HEAD_DIM, HEAD_DIM), F32),
            pltpu.VMEM((NCH * N_HEADS, 8, HEAD_DIM), F32),
            pltpu.VMEM((NCH * N_HEADS, 8, HEAD_DIM), F32),
        ],
        compiler_params=_cparams(("arbitrary", "arbitrary")),
        name="mlstm",
    )(proj_big, proj_f32, par, st_c, st_n8, st_m8)


def _gelu_tanh(x):
    return 0.5 * x * (1.0 + jnp.tanh(math.sqrt(2.0 / math.pi) * (x + 0.044715 * (x * x * x))))


def _tile_scan(a, b, reverse, span=None):
    rows_n = a.shape[0]
    n = rows_n if span is None else span
    row = lax.broadcasted_iota(jnp.int32, a.shape, 0) & (n - 1)
    s = 1
    while s < n:
        if reverse:
            ok = row < n - s
            a_sh = jnp.where(ok, pltpu.roll(a, rows_n - s, 0), 1.0)
            b_sh = jnp.where(ok, pltpu.roll(b, rows_n - s, 0), 0.0)
        else:
            ok = row >= s
            a_sh = jnp.where(ok, pltpu.roll(a, s, 0), 1.0)
            b_sh = jnp.where(ok, pltpu.roll(b, s, 0), 0.0)
        b = a * b_sh + b
        a = a * a_sh
        s *= 2
    return a, b


def _lru_kernel(x_ref, g_ref, cw_ref, cb_ref, wa_ref, ba_ref, wx_ref, bx_ref, lam_ref, h0_ref,
                y_ref, hf_ref, xc_scr, hfw_scr, *, rows: Rows, br):
    blk = pl.program_id(0)
    is_ctx = blk * br < rows.ctx_rows
    ts = rows.ctx_len
    n_tiles = br // ts
    seq_len = jnp.where(is_ctx, rows.ctx_len, rows.lat_len)
    xc_scr[...] = _conv_taps(x_ref[...], cw_ref[0], seq_len) + cb_ref[0]

    for dd in range(2):
        reverse = dd == 1
        sp_lam = _softplus(-lam_ref[0, dd])

        def tile(n, carry, dd=dd, reverse=reverse, sp_lam=sp_lam):
            t = (n_tiles - 1 - n) if reverse else n
            r0 = pl.multiple_of(t * ts, ts)
            xc = xc_scr[pl.ds(r0, ts), :]
            r = _sigmoid(_bdot(xc, wa_ref[0, dd, 0]) + ba_ref[0, dd])
            gi = _sigmoid(_bdot(xc, wx_ref[0, dd, 0]) + bx_ref[0, dd])
            log_a = -LRU_C * r * sp_lam
            a = jnp.exp(log_a)
            b = jnp.sqrt(-jnp.tanh(log_a) * (a * a + 1.0)) * (gi * xc)
            a_cum, h = _tile_scan(a, b, reverse)
            carry = jnp.where(is_ctx, 0.0, carry)
            h = h + a_cum * carry
            last = h[0:1, :] if reverse else h[ts - 1:ts, :]
            hf_ref[0, dd, pl.ds(t, 1), :] = last
            if reverse:
                g = g_ref[pl.ds(r0, ts), :].astype(F32)
                y_ref[pl.ds(r0, ts), :] = ((hfw_scr[pl.ds(r0, ts), :] + h) * _gelu_tanh(g)).astype(BF16)
            else:
                hfw_scr[pl.ds(r0, ts), :] = h
            return last

        lax.fori_loop(0, n_tiles, tile, h0_ref[0, 0, dd])


def _lru(proj_f32, proj_big, conv_w, conv_b, wa, ba, wx, bx, lam, h0, l, rows: Rows):
    br = rows.lat_len
    nt = rows.total
    n_blocks = nt // br
    n_ctx_blocks = rows.ctx_rows // br
    n_tiles = br // rows.ctx_len
    gcol = OFF_GB // BW_B

    def lat_seq(b):
        return jnp.maximum(b - n_ctx_blocks, 0)

    vec = lambda a: a.reshape(DEPTH, 2, 1, W_B)
    return pl.pallas_call(
        functools.partial(_lru_kernel, rows=rows, br=br),
        grid=(n_blocks, NB_B),
        in_specs=[
            pl.BlockSpec((br, BW_B), lambda b, j: (b, j)),
            pl.BlockSpec((br, BW_B), lambda b, j: (b, gcol + j)),
            pl.BlockSpec((1, CONV_W, BW_B), lambda b, j: (l, 0, j)),
            pl.BlockSpec((1, 1, BW_B), lambda b, j: (l, 0, j)),
            pl.BlockSpec((1, 2, 1, BW_B, BW_B), lambda b, j: (l, 0, j, 0, 0)),
            pl.BlockSpec((1, 2, 1, BW_B), lambda b, j: (l, 0, 0, j)),
            pl.BlockSpec((1, 2, 1, BW_B, BW_B), lambda b, j: (l, 0, j, 0, 0)),
            pl.BlockSpec((1, 2, 1, BW_B), lambda b, j: (l, 0, 0, j)),
            pl.BlockSpec((1, 2, 1, BW_B), lambda b, j: (l, 0, 0, j)),
            pl.BlockSpec((1, 1, 2, 1, BW_B), lambda b, j: (lat_seq(b), l, 0, 0, j)),
        ],
        out_specs=[
            pl.BlockSpec((br, BW_B), lambda b, j: (b, j)),
            pl.BlockSpec((1, 2, n_tiles, BW_B), lambda b, j: (b, 0, 0, j)),
        ],
        out_shape=[
            jax.ShapeDtypeStruct((nt, W_B), BF16),
            jax.ShapeDtypeStruct((n_blocks, 2, n_tiles, W_B), F32),
        ],
        scratch_shapes=[pltpu.VMEM((br, BW_B), F32), pltpu.VMEM((br, BW_B), F32)],
        compiler_params=_cparams(("arbitrary", "arbitrary")),
        name="lru",
    )(proj_f32, proj_big, conv_w, conv_b.reshape(DEPTH, 1, W_B), wa, vec(ba), wx, vec(bx), vec(lam),
      h0.reshape(h0.shape[0], DEPTH, 2, 1, W_B))


def _route(lg, le):
    lane = lax.broadcasted_iota(jnp.int32, lg.shape, 1)
    neg = -jnp.inf
    lgm = jnp.where(lane < N_GROUPS, lg, neg)
    gmax = jnp.max(lgm, axis=-1, keepdims=True)
    p_grp = 1.0 / jnp.sum(jnp.exp(lgm - gmax), axis=-1, keepdims=True)
    g_sel = jnp.min(jnp.where(lgm == gmax, lane, LANES), axis=-1, keepdims=True)
    in_grp = jnp.logical_and(lane >= g_sel * E_PER_GROUP, lane < (g_sel + 1) * E_PER_GROUP)
    lem = jnp.where(in_grp, le, neg)
    v1 = jnp.max(lem, axis=-1, keepdims=True)
    i1 = jnp.min(jnp.where(lem == v1, lane, LANES), axis=-1, keepdims=True)
    lem2 = jnp.where(lane == i1, neg, lem)
    v2 = jnp.max(lem2, axis=-1, keepdims=True)
    i2 = jnp.min(jnp.where(lem2 == v2, lane, LANES), axis=-1, keepdims=True)
    e2 = jnp.exp(v2 - v1)
    w1 = p_grp / (1.0 + e2)
    w2 = p_grp * e2 / (1.0 + e2)
    gate = jnp.where(lane == i1, w1, 0.0) + jnp.where(lane == i2, w2, 0.0)
    local = gate
    for g in range(1, N_GROUPS):
        local = jnp.where(g_sel == g, pltpu.roll(gate, LANES - g * E_PER_GROUP, 1), local)
    local = jnp.where(lane < E_PER_GROUP, local, 0.0)
    hi = local.astype(BF16).astype(F32)
    rest = local - hi
    mid = rest.astype(BF16).astype(F32)
    lo = (rest - mid).astype(BF16).astype(F32)
    packed = hi + pltpu.roll(mid, E_PER_GROUP, 1) + pltpu.roll(lo, 2 * E_PER_GROUP, 1)
    return packed, g_sel


def _group_rank(g_sel, cnt_scr):
    tm = g_sel.shape[0]
    lane = lax.broadcasted_iota(jnp.int32, (tm, LANES), 1)
    onehot = jnp.where(lane == g_sel, 1.0, 0.0)
    ii = lax.broadcasted_iota(jnp.int32, (tm, tm), 0)
    jj = lax.broadcasted_iota(jnp.int32, (tm, tm), 1)
    before = jnp.where(ii > jj, 1.0, 0.0).astype(BF16)
    seen = jnp.dot(before, onehot.astype(BF16), preferred_element_type=F32) + cnt_scr[0:1, :]
    cnt_scr[...] = cnt_scr[...] + jnp.sum(onehot, axis=0, keepdims=True)
    return jnp.sum(seen * onehot, axis=-1, keepdims=True)


def _merge_kernel(x_ref, mod_ref, oa_ref, z_ref, yb_ref, hc_ref, oc_ref, gt_ref, dn_ref, mn_ref,
                  wpa_ref, wpb_ref, wpc_ref, wout_ref, g1_ref, b1_ref, wrh_ref, wrl_ref, br_ref,
                  x1_ref, h2_ref, gate_ref, meta_ref, cnt_scr):
    oa = oa_ref[0] + oa_ref[1]
    hc = hc_ref[0] + hc_ref[1]
    ya, yc = [], []
    for h in range(N_HEADS):
        sl = slice(h * HEAD_DIM, (h + 1) * HEAD_DIM)
        o_h = oa[:, sl]
        o_h = o_h * lax.rsqrt(jnp.mean(o_h * o_h, axis=-1, keepdims=True) + RMS_EPS) * dn_ref[...]
        ya.append((o_h * _silu(z_ref[:, sl].astype(F32))).astype(BF16))
        c_h = _ln(hc[:, sl]) * mn_ref[:, sl]
        yc.append((_sigmoid(oc_ref[:, sl].astype(F32)) * c_h).astype(BF16))
    ya = jnp.concatenate(ya, axis=-1)
    yc = jnp.concatenate(yc, axis=-1)
    ga = _sigmoid(gt_ref[:, 0:D_MODEL].astype(F32))
    gb = _sigmoid(gt_ref[:, D_MODEL:2 * D_MODEL].astype(F32))
    gc = _sigmoid(gt_ref[:, 2 * D_MODEL:3 * D_MODEL].astype(F32))
    d = functools.partial(jnp.dot, preferred_element_type=F32)
    merged = ga * d(ya, wpa_ref[...]) + gb * d(yb_ref[...], wpb_ref[...]) + gc * d(yc, wpc_ref[...])
    mixed = d(merged.astype(BF16), wout_ref[...])
    gate1 = mod_ref[0, 2:3, :]
    shift2 = mod_ref[0, 3:4, :]
    scale2 = mod_ref[0, 4:5, :]
    x1 = _ln(DN_ALPHA * x_ref[...] + gate1 * mixed) * g1_ref[...] + b1_ref[...]
    x1_ref[...] = x1
    h2 = _ln(x1) * (1.0 + scale2) + shift2
    h2_ref[...] = h2.astype(BF16)
    hh, hl = _split2(h2)
    d = functools.partial(jnp.dot, preferred_element_type=F32)
    logits = d(hh, wrh_ref[...]) + (d(hh, wrl_ref[...]) + d(hl, wrh_ref[...])) + br_ref[...]
    packed, g_sel = _route(logits, pltpu.roll(logits, LANES - 64, 1))

    @pl.when(pl.program_id(0) == 0)
    def _():
        cnt_scr[...] = jnp.zeros(cnt_scr.shape, F32)

    rank = _group_rank(g_sel, cnt_scr)
    lane = lax.broadcasted_iota(jnp.int32, packed.shape, 1)
    route = jnp.where(lane == ROUTE_GROUP_LANE, g_sel.astype(F32), jnp.where(lane == ROUTE_RANK_LANE, rank, packed))
    gate_ref[...] = route
    meta_ref[0] = route.T[ROUTE_GROUP_LANE:ROUTE_GROUP_LANE + SUBLANES, :]


def _merge(x, mod_l, o_a, proj_big, y_b, h_c, dn, mn, wpa, wpb, wpc, wout, g1, b1, wr, br, rows: Rows, tm):
    nt = rows.total
    row = lambda i: (i, 0)
    const = lambda i: (0, 0)
    cw = W_MIX
    return pl.pallas_call(
        _merge_kernel,
        grid=(nt // tm,),
        in_specs=[
            pl.BlockSpec((tm, D_MODEL), row),
            pl.BlockSpec((1, 6, D_MODEL), lambda i: (_cond_index(i * tm, rows), 0, 0)),
            pl.BlockSpec((2, tm, cw), lambda i: (0, i, 0)),
            pl.BlockSpec((tm, cw), lambda i: (i, OFF_Z // cw)),
            pl.BlockSpec((tm, cw), row),
            pl.BlockSpec((2, tm, cw), lambda i: (0, i, 0)),
            pl.BlockSpec((tm, cw), lambda i: (i, OFF_OC // cw)),
            pl.BlockSpec((tm, 3 * D_MODEL), lambda i: (i, OFF_GATES // (3 * D_MODEL))),
            pl.BlockSpec((1, HEAD_DIM), const),
            pl.BlockSpec((1, cw), const),
            pl.BlockSpec((cw, D_MODEL), const),
            pl.BlockSpec((cw, D_MODEL), const),
            pl.BlockSpec((cw, D_MODEL), const),
            pl.BlockSpec((D_MODEL, D_MODEL), const),
            pl.BlockSpec((1, D_MODEL), const),
            pl.BlockSpec((1, D_MODEL), const),
            pl.BlockSpec((D_MODEL, LANES), const),
            pl.BlockSpec((D_MODEL, LANES), const),
            pl.BlockSpec((1, LANES), const),
        ],
        out_specs=[
            pl.BlockSpec((tm, D_MODEL), row),
            pl.BlockSpec((tm, D_MODEL), row),
            pl.BlockSpec((tm, LANES), row),
            pl.BlockSpec((1, SUBLANES, tm), lambda i: (i, 0, 0)),
        ],
        out_shape=[
            jax.ShapeDtypeStruct((nt, D_MODEL), F32),
            jax.ShapeDtypeStruct((nt, D_MODEL), BF16),
            jax.ShapeDtypeStruct((nt, LANES), F32),
            jax.ShapeDtypeStruct((nt // tm, SUBLANES, tm), F32),
        ],
        scratch_shapes=[pltpu.VMEM((SUBLANES, LANES), F32)],
        compiler_params=_cparams(("arbitrary",)),
        name="merge",
    )(x, mod_l, o_a, proj_big, y_b, h_c, proj_big, proj_big, dn, mn, wpa, wpb, wpc, wout, g1, b1, *_split2(wr), br)


MOE_TILE = 512


class MoePlan(NamedTuple):
    dest_row: jax.Array
    dest_col: jax.Array
    tile_group: jax.Array
    by_tile: tuple
    by_block: tuple


def _pair_list(mask, n_pairs, minor):
    flat = mask.reshape(-1)
    cnt = jnp.sum(flat.astype(jnp.int32))
    idx = jnp.nonzero(flat, size=n_pairs, fill_value=0)[0].astype(jnp.int32)
    pos = jnp.arange(n_pairs, dtype=jnp.int32)
    valid = pos < cnt
    idx = jnp.where(valid, idx, idx[jnp.maximum(cnt - 1, 0)])
    major, mnr = idx // minor, idx % minor
    prev = jnp.concatenate([jnp.full((1,), -1, jnp.int32), major[:-1]])
    nxt = jnp.concatenate([major[1:], jnp.full((1,), -1, jnp.int32)])
    first = jnp.logical_and(valid, major != prev)
    last = jnp.logical_and(valid, jnp.logical_or(major != nxt, pos == cnt - 1))
    i32 = lambda a: a.astype(jnp.int32)
    return major, mnr, i32(first), i32(last), i32(valid)


def _moe_plan(g_sel, rank, nt):
    n_blocks = nt // MOE_TILE
    n_tiles = n_blocks + N_GROUPS
    n_pairs = n_tiles + N_GROUPS * n_blocks
    oh = (g_sel[:, None] == jnp.arange(N_GROUPS, dtype=jnp.int32)[None, :]).astype(jnp.int32)
    counts = jnp.sum(oh, axis=0)
    tiles_g = (counts + MOE_TILE - 1) // MOE_TILE
    tile_end = jnp.cumsum(tiles_g)
    tile_start = tile_end - tiles_g
    dest = jnp.sum(oh * tile_start[None, :], axis=1) * MOE_TILE + rank
    tile_ids = jnp.arange(n_tiles, dtype=jnp.int32)
    tile_group = jnp.minimum(jnp.sum((tile_ids[:, None] >= tile_end[None, :]).astype(jnp.int32), axis=1),
                             N_GROUPS - 1)
    t_oh = ((dest // MOE_TILE)[:, None] == tile_ids[None, :]).astype(F32)
    b_oh = ((jnp.arange(nt, dtype=jnp.int32) // MOE_TILE)[:, None]
            == jnp.arange(n_blocks, dtype=jnp.int32)[None, :]).astype(F32)
    mask = jnp.einsum('tj,tb->jb', t_oh, b_oh) > 0.5
    mask = mask.at[:, 0].set(jnp.logical_or(mask[:, 0], tile_ids >= tile_end[-1]))
    tj, tb, tf, tl, tv = _pair_list(mask, n_pairs, n_blocks)
    cb, cj, cf, cl, cv = _pair_list(mask.T, n_pairs, n_tiles)
    return MoePlan(dest.reshape(n_blocks, 1, MOE_TILE),
                   jnp.broadcast_to(dest[:, None], (nt, LANES)),
                   tile_group, (tj, tb, tf, tl, tv), (cj, cb, cf, cl, cv))


def _moe_experts_kernel(pj_ref, pb_ref, pf_ref, pl_ref, pv_ref, tg_ref,
                        h_ref, dest_ref, route_ref, w1_ref, w3_ref, w2_ref, y_ref, x_scr, g_scr):
    p = pl.program_id(0)

    @pl.when(pf_ref[p] == 1)
    def _():
        x_scr[...] = jnp.zeros(x_scr.shape, F32)
        g_scr[...] = jnp.zeros(g_scr.shape, F32)

    @pl.when(pv_ref[p] == 1)
    def _():
        row = lax.broadcasted_iota(jnp.int32, (MOE_TILE, MOE_TILE), 0) + pj_ref[p] * MOE_TILE
        sel = jnp.where(dest_ref[0] == row, 1.0, 0.0).astype(BF16)
        d = functools.partial(jnp.dot, preferred_element_type=F32)
        x_scr[...] += d(sel, h_ref[...])
        g_scr[...] += d(sel, route_ref[...].astype(BF16))

    @pl.when(pl_ref[p] == 1)
    def _():
        x = x_scr[...].astype(BF16)
        g = g_scr[...]
        gate = g + (pltpu.roll(g, LANES - E_PER_GROUP, 1) + pltpu.roll(g, LANES - 2 * E_PER_GROUP, 1))
        d = functools.partial(jnp.dot, preferred_element_type=F32)
        acc = jnp.zeros((MOE_TILE, D_MODEL), F32)
        for e in range(E_PER_GROUP):
            hid = _silu(d(x, w1_ref[0, e])) * d(x, w3_ref[0, e]) * gate[:, e:e + 1]
            acc = acc + d(hid.astype(BF16), w2_ref[0, e])
        y_ref[...] = acc.astype(BF16)


def _moe_experts(h2, route, plan: MoePlan, w1, w3, w2, l, nt):
    n_blocks = nt // MOE_TILE
    n_tiles = n_blocks + N_GROUPS
    tj, tb, tf, tl, tv = plan.by_tile
    n_pairs = tj.shape[0]
    wmap = lambda p, pj, pb, pf, pl_, pv, tg: (l * N_GROUPS + tg[pj[p]], 0, 0, 0)
    grid_spec = pltpu.PrefetchScalarGridSpec(
        num_scalar_prefetch=6,
        grid=(n_pairs,),
        in_specs=[
            pl.BlockSpec((MOE_TILE, D_MODEL), lambda p, pj, pb, *_: (pb[p], 0)),
            pl.BlockSpec((1, 1, MOE_TILE), lambda p, pj, pb, *_: (pb[p], 0, 0)),
            pl.BlockSpec((MOE_TILE, LANES), lambda p, pj, pb, *_: (pb[p], 0)),
            pl.BlockSpec((1, E_PER_GROUP, D_MODEL, D_EXPERT), wmap),
            pl.BlockSpec((1, E_PER_GROUP, D_MODEL, D_EXPERT), wmap),
            pl.BlockSpec((1, E_PER_GROUP, D_EXPERT, D_MODEL), wmap),
        ],
        out_specs=pl.BlockSpec((MOE_TILE, D_MODEL), lambda p, pj, *_: (pj[p], 0)),
        scratch_shapes=[pltpu.VMEM((MOE_TILE, D_MODEL), F32), pltpu.VMEM((MOE_TILE, LANES), F32)],
    )
    return pl.pallas_call(
        _moe_experts_kernel,
        grid_spec=grid_spec,
        out_shape=jax.ShapeDtypeStruct((n_tiles * MOE_TILE, D_MODEL), BF16),
        compiler_params=_cparams(("arbitrary",)),
        name="moe_experts",
    )(tj, tb, tf, tl, tv, plan.tile_group, h2, plan.dest_row, route, w1, w3, w2)


def _moe_combine_kernel(cj_ref, cb_ref, cf_ref, cl_ref, cv_ref,
                        y_ref, dest_ref, x1_ref, mod_ref, g2_ref, b2_ref, o_ref, acc_scr):
    p = pl.program_id(0)

    @pl.when(cf_ref[p] == 1)
    def _():
        acc_scr[...] = jnp.zeros(acc_scr.shape, F32)

    @pl.when(cv_ref[p] == 1)
    def _():
        col = lax.broadcasted_iota(jnp.int32, (MOE_TILE, MOE_TILE), 1) + cj_ref[p] * MOE_TILE
        sel = jnp.where(dest_ref[:, 0:1] == col, 1.0, 0.0).astype(BF16)
        acc_scr[...] += jnp.dot(sel, y_ref[...], preferred_element_type=F32)

    @pl.when(cl_ref[p] == 1)
    def _():
        gate2 = mod_ref[0, 5:6, :]
        o_ref[...] = _ln(DN_ALPHA * x1_ref[...] + gate2 * acc_scr[...]) * g2_ref[...] + b2_ref[...]


def _moe_combine(y, plan: MoePlan, x1, mod_l, g2, b2, rows: Rows):
    nt = rows.total
    cj, cb, cf, cl, cv = plan.by_block
    n_pairs = cj.shape[0]
    blk = lambda p, cj, cb, *_: (cb[p], 0)
    const = lambda p, *_: (0, 0)
    grid_spec = pltpu.PrefetchScalarGridSpec(
        num_scalar_prefetch=5,
        grid=(n_pairs,),
        in_specs=[
            pl.BlockSpec((MOE_TILE, D_MODEL), lambda p, cj, *_: (cj[p], 0)),
            pl.BlockSpec((MOE_TILE, LANES), blk),
            pl.BlockSpec((MOE_TILE, D_MODEL), blk),
            pl.BlockSpec((1, 6, D_MODEL), lambda p, cj, cb, *_: (_cond_index(cb[p] * MOE_TILE, rows), 0, 0)),
            pl.BlockSpec((1, D_MODEL), const),
            pl.BlockSpec((1, D_MODEL), const),
        ],
        out_specs=pl.BlockSpec((MOE_TILE, D_MODEL), blk),
        scratch_shapes=[pltpu.VMEM((MOE_TILE, D_MODEL), F32)],
    )
    return pl.pallas_call(
        _moe_combine_kernel,
        grid_spec=grid_spec,
        out_shape=jax.ShapeDtypeStruct((nt, D_MODEL), F32),
        compiler_params=_cparams(("arbitrary",)),
        name="moe_combine",
    )(cj, cb, cf, cl, cv, y, plan.dest_col, x1, mod_l, g2, b2)


class Tiles(NamedTuple):
    token_rows: int
    proj_cols: int
    merge_rows: int


def _tile_config(rows: Rows) -> Tiles:
    token_rows = min(1024, rows.lat_len)
    proj_cols = 1280
    assert N_BIG % proj_cols == 0 and proj_cols % (2 * LANES) == 0
    return Tiles(token_rows, proj_cols, min(256, rows.lat_len))


def _grid_pos_embed(n_tokens):
    rows = n_tokens // GRID_W
    quarter = D_MODEL // 4
    freqs = jnp.exp(-math.log(POS_BASE) * jnp.arange(quarter, dtype=F32) / quarter)
    ar = jnp.arange(rows, dtype=F32).reshape(-1, 1) * freqs
    ac = jnp.arange(GRID_W, dtype=F32).reshape(-1, 1) * freqs
    row_half = jnp.concatenate([jnp.sin(ar), jnp.cos(ar)], axis=-1)
    col_half = jnp.concatenate([jnp.sin(ac), jnp.cos(ac)], axis=-1)
    return jnp.concatenate([jnp.repeat(row_half, GRID_W, axis=0), jnp.tile(col_half, (rows, 1))], axis=-1)


def _pack_w_in(w_in_l):
    sizes = (3 * W_MIX, W_B, W_MIX, 8, 8, W_B, W_MIX, W_MIX, W_MIX, W_MIX, 8, 8, 3 * D_MODEL)
    parts, start = [], 0
    for s in sizes:
        parts.append(w_in_l[:, start:start + s])
        start += s
    qkv_a, x_b, z_a, beta, alpha, g_b, q_c, k_c, v_c, o_c, i_c, f_c, gates = parts
    big = jnp.concatenate([gates, qkv_a, q_c, k_c, v_c, z_a, g_b, o_c], axis=1).astype(BF16)
    pad = jnp.zeros((D_MODEL, LANES - 32), F32)
    small = jnp.concatenate([x_b, beta, alpha, i_c, f_c, pad], axis=1).astype(BF16)
    return big, small


def _lane_row(vals, off):
    out = jnp.zeros((8, LANES), F32)
    for r, (v, o) in enumerate(zip(vals, off)):
        out = out.at[r, o:o + 8].set(v.reshape(-1))
    return out


def kernel(x_prompt, x_sample, state_delta, state_lru, state_mlstm_C, state_mlstm_n, state_mlstm_m, c, c_ctx,
           w_mod, b_mod, w_in, conv_a, delta_a_log, delta_dt_bias, delta_norm, conv_b_w, conv_b_b,
           lru_wa, lru_ba, lru_wx, lru_bx, lru_lambda, mlstm_bi, mlstm_bf, mlstm_norm,
           w_pa, w_pb, w_pc, w_out, ln1_g, ln1_b, ln2_g, ln2_b, w_rg, b_rg, w_re, b_re, w_e1, w_e3, w_e2):
    n_ctx, ctx_len, _ = x_prompt.shape
    n_lat, lat_len, _ = x_sample.shape
    rows = Rows(n_ctx, ctx_len, n_lat, lat_len)
    assert rows.ctx_rows % UNIT == 0 and lat_len % UNIT == 0 and UNIT % ctx_len == 0
    assert rows.ctx_rows % lat_len == 0 and ctx_len % CHUNK == 0 and n_lat <= 7
    assert rows.ctx_rows % MOE_TILE == 0 and lat_len % MOE_TILE == 0

    tiles = _tile_config(rows)
    pos = _grid_pos_embed(lat_len)
    x = _prep(x_prompt.reshape(rows.ctx_rows, D_MODEL), x_sample.reshape(n_lat * lat_len, D_MODEL), pos, rows,
              tiles.token_rows)

    cond8 = jnp.zeros((8, D_MODEL), F32).at[0].set(c_ctx).at[1:1 + n_lat].set(c)
    mod = _modulation(cond8, w_mod, b_mod).reshape(DEPTH, 8, 6, D_MODEL)

    m_bcast = jnp.broadcast_to(state_mlstm_m[..., None], state_mlstm_m.shape + (HEAD_DIM,))
    w_e1g = w_e1.astype(BF16).reshape(DEPTH * N_GROUPS, E_PER_GROUP, D_MODEL, D_EXPERT)
    w_e3g = w_e3.astype(BF16).reshape(DEPTH * N_GROUPS, E_PER_GROUP, D_MODEL, D_EXPERT)
    w_e2g = w_e2.astype(BF16).reshape(DEPTH * N_GROUPS, E_PER_GROUP, D_EXPERT, D_MODEL)

    finals = []
    for l in range(DEPTH):
        w_big, w_small = _pack_w_in(w_in[l])
        par = _lane_row([delta_a_log[l], delta_dt_bias[l], mlstm_bi[l], mlstm_bf[l]],
                        [SM_ALPHA, SM_ALPHA, SM_I, SM_F])
        proj_big, proj_f32 = _projection(x, mod[l], w_big, w_small, rows, tiles.token_rows, tiles.proj_cols)
        qkv_c = _conv_a(proj_big, conv_a, l, rows, lat_len)
        o_a, sf_a = _delta(qkv_c, proj_f32, par, state_delta, l, rows)
        y_b, hf_b = _lru(proj_f32, proj_big, conv_b_w, conv_b_b, lru_wa, lru_ba, lru_wx, lru_bx, lru_lambda,
                         state_lru, l, rows)
        h_c, cf, nf, mf = _mlstm(proj_big, proj_f32, par, state_mlstm_C, state_mlstm_n, m_bcast, l, rows)
        wr = jnp.zeros((D_MODEL, LANES), F32).at[:, :N_GROUPS].set(w_rg[l]).at[:, 64:64 + N_EXPERTS].set(w_re[l])
        br = jnp.zeros((1, LANES), F32).at[0, :N_GROUPS].set(b_rg[l]).at[0, 64:64 + N_EXPERTS].set(b_re[l])
        x1, h2, gate, meta = _merge(x, mod[l], o_a, proj_big, y_b, h_c,
                                    delta_norm[l].reshape(1, HEAD_DIM), mlstm_norm[l].reshape(1, W_MIX),
                                    w_pa[l].astype(BF16), w_pb[l].astype(BF16), w_pc[l].astype(BF16),
                                    w_out[l].astype(BF16), ln1_g[l].reshape(1, D_MODEL),
                                    ln1_b[l].reshape(1, D_MODEL), wr, br, rows, tiles.merge_rows)
        meta_row = lambda r: meta[:, r, :].reshape(rows.total).astype(jnp.int32)
        plan = _moe_plan(meta_row(0), meta_row(ROUTE_RANK_LANE - ROUTE_GROUP_LANE), rows.total)
        y_moe = _moe_experts(h2, gate, plan, w_e1g, w_e3g, w_e2g, l, rows.total)
        x = _moe_combine(y_moe, plan, x1, mod[l], ln2_g[l].reshape(1, D_MODEL), ln2_b[l].reshape(1, D_MODEL), rows)
        n_ctx_blocks = rows.ctx_rows // lat_len
        lru_fin = jnp.swapaxes(hf_b[:n_ctx_blocks], 1, 2).reshape(n_ctx, 2, W_B)
        finals.append((sf_a, lru_fin, cf, nf[:, :, :, 0, :], mf[:, :, :, 0, 0]))

    new_delta, new_lru, new_mc, new_mn, new_mm = (jnp.stack([f[i] for f in finals], axis=1) for i in range(5))
    y_prompt = x[:rows.ctx_rows].reshape(n_ctx, ctx_len, D_MODEL)
    y_sample = x[rows.ctx_rows:].reshape(n_lat, lat_len, D_MODEL)
    return (y_prompt, y_sample, new_delta, new_lru, new_mc, new_mn, new_mm)
```

```python
import functools
import math
from typing import NamedTuple

import jax
import jax.numpy as jnp
from jax import lax
from jax.experimental import pallas as pl
from jax.experimental.pallas import tpu as pltpu

F32 = jnp.float32
BF16 = jnp.bfloat16

D_MODEL = 1024
DEPTH = 2
GRID_W = 64
POS_BASE = 10000.0
CONV_W = 4
LN_EPS = 1e-5
RMS_EPS = 1e-6
N_HEADS = 4
HEAD_DIM = 128
W_MIX = N_HEADS * HEAD_DIM
CHUNK = 64
W_B = 512
NB_B = 4
BW_B = W_B // NB_B
LRU_C = 8.0
N_GROUPS = 4
E_PER_GROUP = 8
N_EXPERTS = N_GROUPS * E_PER_GROUP
D_EXPERT = 256
DN_ALPHA = (2 * DEPTH) ** 0.25

LANES = 128
SUBLANES = 8
VMEM_LIMIT = 56 * 1024 * 1024

OFF_GATES, OFF_QKV, OFF_QC, OFF_KC, OFF_VC, OFF_Z, OFF_GB, OFF_OC = 0, 3072, 4608, 5120, 5632, 6144, 6656, 7168
N_BIG = OFF_OC + W_MIX
N_F32 = W_B + LANES
SM_BETA, SM_ALPHA, SM_I, SM_F = 0, 8, 16, 24
ROUTE_GROUP_LANE = 24
ROUTE_RANK_LANE = 25


class Rows(NamedTuple):
    n_ctx: int
    ctx_len: int
    n_lat: int
    lat_len: int

    @property
    def ctx_rows(self):
        return self.n_ctx * self.ctx_len

    @property
    def total(self):
        return self.ctx_rows + self.n_lat * self.lat_len


def _cparams(sem):
    return pltpu.CompilerParams(dimension_semantics=sem, vmem_limit_bytes=VMEM_LIMIT)


def _sigmoid(x):
    return 1.0 / (1.0 + jnp.exp(-x))


def _silu(x):
    return x * _sigmoid(x)


def _softplus(x):
    return jnp.maximum(x, 0.0) + jnp.log1p(jnp.exp(-jnp.abs(x)))


def _ln(x):
    mu = jnp.mean(x, axis=-1, keepdims=True)
    xc = x - mu
    var = jnp.mean(xc * xc, axis=-1, keepdims=True)
    return xc * lax.rsqrt(var + LN_EPS)


def _bdot(a, b):
    return jnp.dot(a.astype(BF16), b.astype(BF16), preferred_element_type=F32)


def _bdot_nt(a, b):
    return lax.dot_general(a.astype(BF16), b.astype(BF16), (((1,), (1,)), ((), ())), preferred_element_type=F32)


def _bdot_tn(a, b):
    return lax.dot_general(a.astype(BF16), b.astype(BF16), (((0,), (0,)), ((), ())), preferred_element_type=F32)


def _split2(a):
    hi = a.astype(BF16)
    lo = (a - hi.astype(F32)).astype(BF16)
    return hi, lo


def _split3(a):
    hi = a.astype(BF16)
    r = a - hi.astype(F32)
    mid = r.astype(BF16)
    lo = (r - mid.astype(F32)).astype(BF16)
    return hi, mid, lo


def _dot3(a, b):
    ah, al = _split2(a)
    bh, bl = _split2(b)
    d = functools.partial(jnp.dot, preferred_element_type=F32)
    return d(ah, bh) + (d(ah, bl) + d(al, bh))


def _dot_exact_lhs(m_bf16, x):
    xh, xm, xl = _split3(x)
    d = functools.partial(jnp.dot, preferred_element_type=F32)
    return d(m_bf16, xh) + (d(m_bf16, xm) + d(m_bf16, xl))


def _cond_index(row0, rows: Rows):
    return jnp.maximum(0, (row0 - rows.ctx_rows + rows.lat_len) // rows.lat_len)


def _mod_kernel(c_ref, w_ref, b_ref, o_ref):
    o_ref[0] = _bdot(_silu(c_ref[...]), w_ref[0]) + b_ref[0]


def _modulation(cond8, w_mod, b_mod):
    tn = 1024
    n6 = 6 * D_MODEL
    return pl.pallas_call(
        _mod_kernel,
        grid=(DEPTH, n6 // tn),
        in_specs=[
            pl.BlockSpec((8, D_MODEL), lambda l, n: (0, 0)),
            pl.BlockSpec((1, D_MODEL, tn), lambda l, n: (l, 0, n)),
            pl.BlockSpec((1, 1, tn), lambda l, n: (l, 0, n)),
        ],
        out_specs=pl.BlockSpec((1, 8, tn), lambda l, n: (l, 0, n)),
        out_shape=jax.ShapeDtypeStruct((DEPTH, 8, n6), F32),
        compiler_params=_cparams(("arbitrary", "arbitrary")),
        name="modulation",
    )(cond8, w_mod, b_mod.reshape(DEPTH, 1, n6))


def _x_specs(x_parts, rows: Rows, tm):
    if len(x_parts) == 1:
        return [pl.BlockSpec((tm, D_MODEL), lambda i, *_: (i, 0))]
    n_ctx_tiles = rows.ctx_rows // tm
    pos_tiles = rows.lat_len // tm
    return [
        pl.BlockSpec((tm, D_MODEL), lambda i, *_: (jnp.minimum(i, n_ctx_tiles - 1), 0)),
        pl.BlockSpec((tm, D_MODEL), lambda i, *_: (jnp.maximum(i - n_ctx_tiles, 0), 0)),
        pl.BlockSpec((tm, D_MODEL), lambda i, *_: (jnp.maximum(i - n_ctx_tiles, 0) % pos_tiles, 0)),
    ]


def _read_x(x_refs, n_ctx_tiles):
    if len(x_refs) == 1:
        return x_refs[0][...]
    xp_ref, xs_ref, pos_ref = x_refs
    return jnp.where(pl.program_id(0) < n_ctx_tiles, xp_ref[...], xs_ref[...] + pos_ref[...])


def _proj_kernel(*refs, n_x, n_ctx_tiles):
    x_refs = refs[:n_x]
    mod_ref, wb_ref, wf_ref, ob_ref, of_ref, h_scr = refs[n_x:]
    n = pl.program_id(1)

    @pl.when(n == 0)
    def _():
        shift1 = mod_ref[0, 0:1, :]
        scale1 = mod_ref[0, 1:2, :]
        h = (_ln(_read_x(x_refs, n_ctx_tiles)) * (1.0 + scale1) + shift1).astype(BF16)
        h_scr[...] = h
        of_ref[...] = jnp.dot(h, wf_ref[...], preferred_element_type=F32)

    ob_ref[...] = jnp.dot(h_scr[...], wb_ref[...], preferred_element_type=F32).astype(BF16)


def _projection(x_parts, mod_l, w_big, w_f32, rows: Rows, tm, tn):
    nt = rows.total
    return pl.pallas_call(
        functools.partial(_proj_kernel, n_x=len(x_parts), n_ctx_tiles=rows.ctx_rows // tm),
        grid=(nt // tm, N_BIG // tn),
        in_specs=[
            *_x_specs(x_parts, rows, tm),
            pl.BlockSpec((1, 6, D_MODEL), lambda i, n: (_cond_index(i * tm, rows), 0, 0)),
            pl.BlockSpec((D_MODEL, tn), lambda i, n: (0, n)),
            pl.BlockSpec((D_MODEL, N_F32), lambda i, n: (0, 0)),
        ],
        out_specs=[
            pl.BlockSpec((tm, tn), lambda i, n: (i, n)),
            pl.BlockSpec((tm, N_F32), lambda i, n: (i, 0)),
        ],
        out_shape=[
            jax.ShapeDtypeStruct((nt, N_BIG), BF16),
            jax.ShapeDtypeStruct((nt, N_F32), F32),
        ],
        scratch_shapes=[pltpu.VMEM((tm, D_MODEL), BF16)],
        compiler_params=_cparams(("arbitrary", "arbitrary")),
        name="projection",
    )(*x_parts, mod_l, w_big, w_f32)


def _conv_taps(x, w_ref, seq_len):
    n = x.shape[0]
    pos = lax.broadcasted_iota(jnp.int32, x.shape, 0) & (seq_len - 1)
    xm1 = jnp.where(pos >= 1, pltpu.roll(x, 1, 0), 0.0)
    xp1 = jnp.where(pos <= seq_len - 2, pltpu.roll(x, n - 1, 0), 0.0)
    xp2 = jnp.where(pos <= seq_len - 3, pltpu.roll(x, n - 2, 0), 0.0)
    return xm1 * w_ref[0:1, :] + x * w_ref[1:2, :] + xp1 * w_ref[2:3, :] + xp2 * w_ref[3:4, :]


def _conv_a_kernel(x_ref, w_ref, o_ref, *, rows: Rows, br):
    b = pl.program_id(0)
    j = pl.program_id(1)
    seq_len = jnp.where(b * br < rows.ctx_rows, rows.ctx_len, rows.lat_len)
    y = _silu(_conv_taps(x_ref[...].astype(F32), w_ref[0], seq_len))
    nrm = lax.rsqrt(jnp.sum(y * y, axis=-1, keepdims=True) + RMS_EPS)
    fac = jnp.where(j < N_HEADS, nrm * HEAD_DIM ** -0.5, jnp.where(j < 2 * N_HEADS, nrm, 1.0))
    o_ref[...] = (y * fac).astype(BF16)


def _conv_a(proj_big, conv_w, l, rows: Rows, br):
    nt = rows.total
    ncol = 3 * N_HEADS
    return pl.pallas_call(
        functools.partial(_conv_a_kernel, rows=rows, br=br),
        grid=(nt // br, ncol),
        in_specs=[
            pl.BlockSpec((br, LANES), lambda b, j: (b, OFF_QKV // LANES + j)),
            pl.BlockSpec((1, CONV_W, LANES), lambda b, j: (l, 0, j)),
        ],
        out_specs=pl.BlockSpec((br, LANES), lambda b, j: (b, j)),
        out_shape=jax.ShapeDtypeStruct((nt, ncol * LANES), BF16),
        compiler_params=_cparams(("arbitrary", "arbitrary")),
        name="conv_a",
    )(proj_big, conv_w)


UNIT = 1024
NCH = UNIT // CHUNK


def _unit_layout(rows: Rows):
    n_ctx_units = rows.ctx_rows // UNIT
    lat_units = rows.lat_len // UNIT
    return n_ctx_units, lat_units, n_ctx_units + rows.n_lat * lat_units


def _unit_rowblock(u, d, rows: Rows):
    n_ctx_units, lat_units, _ = _unit_layout(rows)
    v = jnp.maximum(u - n_ctx_units, 0)
    b = v // lat_units
    j = v % lat_units
    jj = j + d * (lat_units - 1 - 2 * j)
    return jnp.where(u < n_ctx_units, u, n_ctx_units + b * lat_units + jj)


def _unit_lat_seq(u, rows: Rows):
    n_ctx_units, lat_units, _ = _unit_layout(rows)
    return jnp.maximum(u - n_ctx_units, 0) // lat_units


def _dir_masks(fwd):
    ii = lax.broadcasted_iota(jnp.int32, (CHUNK, CHUNK), 0)
    jj = lax.broadcasted_iota(jnp.int32, (CHUNK, CHUNK), 1)
    s = jnp.where(fwd, ii - jj, jj - ii)
    return s >= 0, s > 0, ii == jj


def _bmm(a, b):
    return jnp.einsum('gik,gkj->gij', a.astype(BF16), b.astype(BF16), preferred_element_type=F32)


def _bmm_nt(a, b):
    return jnp.einsum('gik,gjk->gij', a.astype(BF16), b.astype(BF16), preferred_element_type=F32)


def _bmm_tn(a, b):
    return jnp.einsum('gik,gij->gkj', a.astype(BF16), b.astype(BF16), preferred_element_type=F32)


def _chunk_cumsum(x, reverse):
    n = x.shape[0]
    pos = lax.broadcasted_iota(jnp.int32, x.shape, 0) & (CHUNK - 1)
    s = 1
    while s < CHUNK:
        if reverse:
            x = x + jnp.where(pos < CHUNK - s, pltpu.roll(x, n - s, 0), 0.0)
        else:
            x = x + jnp.where(pos >= s, pltpu.roll(x, s, 0), 0.0)
        s *= 2
    return x


def _dir_select(fwd, x):
    return jnp.where(fwd, x, pltpu.roll(x, LANES - N_HEADS, 1))


GROUP_CHUNKS = 8
GROUP_ROWS = GROUP_CHUNKS * CHUNK


def _group_columns(col_arr, row_arr, lane0):
    cols, rws = [], []
    for cc in range(GROUP_CHUNKS):
        for h in range(N_HEADS):
            cols.append(jnp.broadcast_to(col_arr[cc * CHUNK:(cc + 1) * CHUNK, lane0 + h:lane0 + h + 1],
                                         (CHUNK, HEAD_DIM)))
            if row_arr is not None:
                rws.append(jnp.broadcast_to(row_arr[lane0 + h:lane0 + h + 1, cc * CHUNK:(cc + 1) * CHUNK],
                                            (CHUNK, CHUNK)))
    return jnp.stack(cols), (jnp.stack(rws) if rws else None)


def _group_heads(ref, r0, col0):
    return jnp.stack([ref[pl.ds(r0 + cc * CHUNK, CHUNK), col0 + h * HEAD_DIM:col0 + (h + 1) * HEAD_DIM]
                      for cc in range(GROUP_CHUNKS) for h in range(N_HEADS)])


def _delta_kernel(qkv_ref, sm_ref, par_ref, s0_ref, o_ref, sf_ref,
                  s_scr, sa_scr, sb_scr, oq_scr, ov_scr, ge_scr, *, rows: Rows):
    d = pl.program_id(0)
    u = pl.program_id(1)
    n_ctx_units, lat_units, _ = _unit_layout(rows)
    seq_chunks = rows.ctx_len // CHUNK
    seq_per_unit = UNIT // rows.ctx_len
    fwd = d == 0
    is_ctx = u < n_ctx_units
    causal, strict, eye = _dir_masks(fwd)
    eye_f = jnp.where(eye, 1.0, 0.0)
    ii = lax.broadcasted_iota(jnp.int32, (CHUNK, CHUNK), 0)
    jj = lax.broadcasted_iota(jnp.int32, (CHUNK, CHUNK), 1)
    pair_masks = [jnp.logical_and((ii >> (s + 1)) == (jj >> (s + 1)), (ii >> s) != (jj >> s))
                  for s in range(CHUNK.bit_length() - 1)]
    ng = GROUP_CHUNKS * N_HEADS

    def pre(it, carry):
        r0 = pl.multiple_of(it * GROUP_ROWS, GROUP_ROWS)
        g0 = pl.multiple_of(it * ng, ng)
        sm = sm_ref[pl.ds(r0, GROUP_ROWS), :]
        g_all = -jnp.exp(par_ref[0:1, :]) * _softplus(sm + par_ref[1:2, :])
        gcum = _dir_select(fwd, jnp.where(fwd, _chunk_cumsum(g_all, False), _chunk_cumsum(g_all, True)))
        bsel = _dir_select(fwd, _sigmoid(sm))
        gi, grow = _group_columns(gcum, gcum.T, SM_ALPHA)
        beta = jnp.stack([bsel[cc * CHUNK:(cc + 1) * CHUNK, SM_BETA + h:SM_BETA + h + 1]
                          for cc in range(GROUP_CHUNKS) for h in range(N_HEADS)])
        q = _group_heads(qkv_ref, r0, 0)
        k = _group_heads(qkv_ref, r0, W_MIX)
        v = _group_heads(qkv_ref, r0, 2 * W_MIX)
        kf = k.astype(F32)
        decay = jnp.exp(jnp.where(causal, gi[:, :, :CHUNK] - grow, -jnp.inf))
        lmat = jnp.where(strict, beta * _bmm_nt(k, k) * decay, 0.0)
        eg = jnp.exp(gi)
        rhs = jnp.concatenate([v.astype(F32) * beta, kf * (beta * eg)], axis=-1)
        t = eye_f - jnp.where(pair_masks[0], lmat, 0.0)
        for pm in pair_masks[1:]:
            tb = t.astype(BF16)
            t = t - _bmm(_bmm(tb, jnp.where(pm, lmat, 0.0)), tb)
        t0 = t.astype(BF16)
        mh, ml = _split2(eye_f + lmat)
        resid = eye_f - (_bmm(mh, t0) + _bmm(ml, t0))
        t1 = t0.astype(F32) + _bmm(t0, resid)
        uu = _bmm(t1, rhs)
        qk = jnp.where(causal, _bmm_nt(q, k) * decay, 0.0)
        g_end = jnp.where(fwd, gi[:, CHUNK - 1:CHUNK, :], gi[:, 0:1, :])
        u_v = uu[:, :, :HEAD_DIM]
        u_k = uu[:, :, HEAD_DIM:]
        k_dec = kf * jnp.exp(g_end - gi)
        sa_scr[pl.ds(g0, ng)] = (-_bmm_tn(k_dec, u_k)).astype(BF16)
        sb_scr[pl.ds(g0, ng)] = _bmm_tn(k_dec, u_v)
        oq_scr[pl.ds(g0, ng)] = (q.astype(F32) * eg - _bmm(qk, u_k)).astype(BF16)
        ov_scr[pl.ds(g0, ng)] = _bmm(qk, u_v)
        ge_scr[pl.ds(g0, ng)] = jnp.broadcast_to(jnp.exp(g_end), (ng, 8, HEAD_DIM))
        return carry

    lax.fori_loop(0, NCH // GROUP_CHUNKS, pre, 0)

    first_lat = jnp.logical_and(u >= n_ctx_units, (u - n_ctx_units) % lat_units == 0)

    @pl.when(first_lat)
    def _():
        s_scr[...] = s0_ref[0, 0, 0]

    def step(n, carry):
        c = jnp.where(fwd, n, NCH - 1 - n)
        r0 = pl.multiple_of(c * CHUNK, CHUNK)
        hs = pl.ds(pl.multiple_of(c * N_HEADS, N_HEADS), N_HEADS)

        @pl.when(jnp.logical_and(is_ctx, n % seq_chunks == 0))
        def _():
            s_scr[...] = jnp.zeros(s_scr.shape, F32)

        s = s_scr[...]
        sb = s.astype(BF16)
        s_scr[...] = s * ge_scr[hs][:, 0:1, :] + (_bmm(sa_scr[hs], sb) + sb_scr[hs])
        o = _bmm(oq_scr[hs], sb) + ov_scr[hs]
        for h in range(N_HEADS):
            o_ref[0, pl.ds(r0, CHUNK), h * HEAD_DIM:(h + 1) * HEAD_DIM] = o[h]

        @pl.when(jnp.logical_and(is_ctx, n % seq_chunks == seq_chunks - 1))
        def _():
            sf_ref[c // seq_chunks, 0] = s_scr[...]

        return carry

    lax.fori_loop(0, NCH, step, 0)


def _delta(qkv_c, proj_f32, par, state, l, rows: Rows):
    n_ctx_units, lat_units, n_units = _unit_layout(rows)
    seq_per_unit = UNIT // rows.ctx_len
    nt = rows.total
    sm_col = W_B // LANES
    kern = functools.partial(_delta_kernel, rows=rows)
    hshape = (NCH * N_HEADS, CHUNK, HEAD_DIM)
    return pl.pallas_call(
        kern,
        grid=(2, n_units),
        in_specs=[
            pl.BlockSpec((UNIT, 3 * W_MIX), lambda d, u: (_unit_rowblock(u, d, rows), 0)),
            pl.BlockSpec((UNIT, LANES), lambda d, u: (_unit_rowblock(u, d, rows), sm_col)),
            pl.BlockSpec((8, LANES), lambda d, u: (0, 0)),
            pl.BlockSpec((1, 1, 1, N_HEADS, HEAD_DIM, HEAD_DIM),
                         lambda d, u: (_unit_lat_seq(u, rows), l, d, 0, 0, 0)),
        ],
        out_specs=[
            pl.BlockSpec((1, UNIT, W_MIX), lambda d, u: (d, _unit_rowblock(u, d, rows), 0)),
            pl.BlockSpec((seq_per_unit, 1, N_HEADS, HEAD_DIM, HEAD_DIM),
                         lambda d, u: (jnp.minimum(u, n_ctx_units - 1), d, 0, 0, 0)),
        ],
        out_shape=[
            jax.ShapeDtypeStruct((2, nt, W_MIX), F32),
            jax.ShapeDtypeStruct((rows.n_ctx, 2, N_HEADS, HEAD_DIM, HEAD_DIM), F32),
        ],
        scratch_shapes=[
            pltpu.VMEM((N_HEADS, HEAD_DIM, HEAD_DIM), F32),
            pltpu.VMEM((NCH * N_HEADS, HEAD_DIM, HEAD_DIM), BF16),
            pltpu.VMEM((NCH * N_HEADS, HEAD_DIM, HEAD_DIM), F32),
            pltpu.VMEM(hshape, BF16),
            pltpu.VMEM(hshape, F32),
            pltpu.VMEM((NCH * N_HEADS, 8, HEAD_DIM), F32),
        ],
        compiler_params=_cparams(("arbitrary", "arbitrary")),
        name="delta",
    )(qkv_c, proj_f32, par, state)


def _mlstm_kernel(big_ref, sm_ref, par_ref, c0_ref, n0_ref, m0_ref, o_ref, cf_ref, nf_ref, mf_ref,
                  c_scr, n_scr, m_scr, fi_scr, mi_scr, pv_scr, ps_scr, kv_scr, ks_scr, me_scr, *, rows: Rows):
    d = pl.program_id(0)
    u = pl.program_id(1)
    n_ctx_units, lat_units, _ = _unit_layout(rows)
    seq_chunks = rows.ctx_len // CHUNK
    fwd = d == 0
    is_ctx = u < n_ctx_units
    causal, _, _ = _dir_masks(fwd)
    qo, ko, vo = 0, W_MIX, 2 * W_MIX
    ng = GROUP_CHUNKS * N_HEADS

    def pre(it, carry):
        r0 = pl.multiple_of(it * GROUP_ROWS, GROUP_ROWS)
        gs = pl.ds(pl.multiple_of(it * ng, ng), ng)
        sm = sm_ref[pl.ds(r0, GROUP_ROWS), :]
        f_all = -_softplus(-(sm + par_ref[3:4, :]))
        fcum = _dir_select(fwd, jnp.where(fwd, _chunk_cumsum(f_all, False), _chunk_cumsum(f_all, True)))
        i_al = pltpu.roll(_dir_select(fwd, sm + par_ref[2:3, :]), SM_F - SM_I, 1)
        fi, hrow = _group_columns(fcum, (fcum - i_al).T, SM_F)
        it_b, _ = _group_columns(i_al, None, SM_F)
        q = _group_heads(big_ref, r0, qo)
        v = _group_heads(big_ref, r0, vo)
        ks = _group_heads(big_ref, r0, ko).astype(F32) * HEAD_DIM ** -0.5
        dmat = jnp.where(causal, fi[:, :, :CHUNK] - hrow, -jnp.inf)
        f_end = jnp.where(fwd, fi[:, CHUNK - 1:CHUNK, :], fi[:, 0:1, :])
        d_end = f_end - fi + it_b
        m_intra = jnp.max(dmat, axis=-1, keepdims=True)
        m_end = jnp.max(d_end, axis=1, keepdims=True)
        p_loc = jnp.exp(dmat - m_intra) * _bmm_nt(q, ks)
        kw = ks * jnp.exp(d_end - m_end)
        fi_scr[gs] = fi
        mi_scr[gs] = jnp.broadcast_to(m_intra, (ng, CHUNK, HEAD_DIM))
        pv_scr[gs] = _bmm(p_loc, v)
        ps_scr[gs] = jnp.broadcast_to(jnp.sum(p_loc, axis=-1, keepdims=True), (ng, CHUNK, HEAD_DIM))
        kv_scr[gs] = _bmm_tn(kw, v)
        ks_scr[gs] = jnp.broadcast_to(jnp.sum(kw, axis=1, keepdims=True), (ng, 8, HEAD_DIM))
        me_scr[gs] = jnp.broadcast_to(m_end, (ng, 8, HEAD_DIM))
        return carry

    lax.fori_loop(0, NCH // GROUP_CHUNKS, pre, 0)

    first_lat = jnp.logical_and(u >= n_ctx_units, (u - n_ctx_units) % lat_units == 0)

    @pl.when(first_lat)
    def _():
        c_scr[...] = c0_ref[0, 0, 0]
        for h in range(N_HEADS):
            n_scr[h] = jnp.broadcast_to(n0_ref[0, 0, 0, h:h + 1, :], (8, HEAD_DIM))
            m_scr[h] = jnp.broadcast_to(m0_ref[0, 0, 0, h:h + 1, :], (8, HEAD_DIM))

    def step(n, carry):
        c = jnp.where(fwd, n, NCH - 1 - n)
        r0 = pl.multiple_of(c * CHUNK, CHUNK)

        @pl.when(jnp.logical_and(is_ctx, n % seq_chunks == 0))
        def _():
            c_scr[...] = jnp.zeros(c_scr.shape, F32)
            n_scr[...] = jnp.zeros(n_scr.shape, F32)
            m_scr[...] = jnp.zeros(m_scr.shape, F32)

        hs = pl.ds(pl.multiple_of(c * N_HEADS, N_HEADS), N_HEADS)
        q = jnp.stack([big_ref[pl.ds(r0, CHUNK), qo + h * HEAD_DIM:qo + (h + 1) * HEAD_DIM]
                       for h in range(N_HEADS)])
        cs = c_scr[...]
        ns = n_scr[...][:, 0:1, :]
        ms = m_scr[...][:, 0:1, :]
        fi = fi_scr[hs]
        mi = mi_scr[hs]
        inter = fi + ms
        mt = jnp.maximum(inter, mi)
        w_int = jnp.exp(inter - mt)
        w_loc = jnp.exp(mi - mt)
        num = w_int * _bmm(q, cs) + w_loc * pv_scr[hs]
        qn = jnp.sum(q.astype(F32) * ns, axis=-1, keepdims=True)
        den = w_int * qn + w_loc * ps_scr[hs]
        hh = num / jnp.maximum(jnp.abs(den), jnp.exp(-mt))
        f_end = jnp.where(fwd, fi[:, CHUNK - 1:CHUNK, :], fi[:, 0:1, :])
        inter_end = f_end + ms
        m_end = me_scr[hs][:, 0:1, :]
        m_new = jnp.maximum(inter_end, m_end)
        s_int = jnp.exp(inter_end - m_new)
        s_loc = jnp.exp(m_end - m_new)
        c_scr[...] = cs * s_int[:, :, 0:1] + kv_scr[hs] * s_loc[:, :, 0:1]
        n_scr[...] = jnp.broadcast_to(ns * s_int + ks_scr[hs][:, 0:1, :] * s_loc, n_scr.shape)
        m_scr[...] = jnp.broadcast_to(m_new, m_scr.shape)
        for h in range(N_HEADS):
            o_ref[0, pl.ds(r0, CHUNK), h * HEAD_DIM:(h + 1) * HEAD_DIM] = hh[h]

        @pl.when(jnp.logical_and(is_ctx, n % seq_chunks == seq_chunks - 1))
        def _():
            sq = c // seq_chunks
            cf_ref[sq, 0] = c_scr[...]
            nf_ref[sq, 0] = n_scr[...]
            mf_ref[sq, 0] = m_scr[...]

        return carry

    lax.fori_loop(0, NCH, step, 0)


def _mlstm(proj_big, proj_f32, par, st_c, st_n8, st_m8, l, rows: Rows):
    n_ctx_units, lat_units, n_units = _unit_layout(rows)
    seq_per_unit = UNIT // rows.ctx_len
    nt = rows.total
    sm_col = W_B // LANES
    qkv_blk = OFF_QC // (3 * W_MIX)
    assert OFF_QC % (3 * W_MIX) == 0
    kern = functools.partial(_mlstm_kernel, rows=rows)
    hshape = (NCH * N_HEADS, CHUNK, HEAD_DIM)
    n_out = rows.n_ctx
    return pl.pallas_call(
        kern,
        grid=(2, n_units),
        in_specs=[
            pl.BlockSpec((UNIT, 3 * W_MIX), lambda d, u: (_unit_rowblock(u, d, rows), qkv_blk)),
            pl.BlockSpec((UNIT, LANES), lambda d, u: (_unit_rowblock(u, d, rows), sm_col)),
            pl.BlockSpec((8, LANES), lambda d, u: (0, 0)),
            pl.BlockSpec((1, 1, 1, N_HEADS, HEAD_DIM, HEAD_DIM),
                         lambda d, u: (_unit_lat_seq(u, rows), l, d, 0, 0, 0)),
            pl.BlockSpec((1, 1, 1, N_HEADS, HEAD_DIM), lambda d, u: (_unit_lat_seq(u, rows), l, d, 0, 0)),
            pl.BlockSpec((1, 1, 1, N_HEADS, HEAD_DIM), lambda d, u: (_unit_lat_seq(u, rows), l, d, 0, 0)),
        ],
        out_specs=[
            pl.BlockSpec((1, UNIT, W_MIX), lambda d, u: (d, _unit_rowblock(u, d, rows), 0)),
            pl.BlockSpec((seq_per_unit, 1, N_HEADS, HEAD_DIM, HEAD_DIM),
                         lambda d, u: (jnp.minimum(u, n_ctx_units - 1), d, 0, 0, 0)),
            pl.BlockSpec((seq_per_unit, 1, N_HEADS, 8, HEAD_DIM),
                         lambda d, u: (jnp.minimum(u, n_ctx_units - 1), d, 0, 0, 0)),
            pl.BlockSpec((seq_per_unit, 1, N_HEADS, 8, HEAD_DIM),
                         lambda d, u: (jnp.minimum(u, n_ctx_units - 1), d, 0, 0, 0)),
        ],
        out_shape=[
            jax.ShapeDtypeStruct((2, nt, W_MIX), F32),
            jax.ShapeDtypeStruct((n_out, 2, N_HEADS, HEAD_DIM, HEAD_DIM), F32),
            jax.ShapeDtypeStruct((n_out, 2, N_HEADS, 8, HEAD_DIM), F32),
            jax.ShapeDtypeStruct((n_out, 2, N_HEADS, 8, HEAD_DIM), F32),
        ],
        scratch_shapes=[
            pltpu.VMEM((N_HEADS, HEAD_DIM, HEAD_DIM), F32),
            pltpu.VMEM((N_HEADS, 8, HEAD_DIM), F32),
            pltpu.VMEM((N_HEADS, 8, HEAD_DIM), F32),
            pltpu.VMEM(hshape, F32),
            pltpu.VMEM(hshape, F32),
            pltpu.VMEM(hshape, F32),
            pltpu.VMEM(hshape, F32),
            pltpu.VMEM((NCH * N_HEADS, HEAD_DIM, HEAD_DIM), F32),
            pltpu.VMEM((NCH * N_HEADS, 8, HEAD_DIM), F32),
            pltpu.VMEM((NCH * N_HEADS, 8, HEAD_DIM), F32),
        ],
        compiler_params=_cparams(("arbitrary", "arbitrary")),
        name="mlstm",
    )(proj_big, proj_f32, par, st_c, st_n8, st_m8)


def _gelu_tanh(x):
    return 0.5 * x * (1.0 + jnp.tanh(math.sqrt(2.0 / math.pi) * (x + 0.044715 * (x * x * x))))


def _tile_scan(a, b, reverse, span=None):
    rows_n = a.shape[0]
    n = rows_n if span is None else span
    row = lax.broadcasted_iota(jnp.int32, a.shape, 0) & (n - 1)
    s = 1
    while s < n:
        if reverse:
            ok = row < n - s
            a_sh = jnp.where(ok, pltpu.roll(a, rows_n - s, 0), 1.0)
            b_sh = jnp.where(ok, pltpu.roll(b, rows_n - s, 0), 0.0)
        else:
            ok = row >= s
            a_sh = jnp.where(ok, pltpu.roll(a, s, 0), 1.0)
            b_sh = jnp.where(ok, pltpu.roll(b, s, 0), 0.0)
        b = a * b_sh + b
        a = a * a_sh
        s *= 2
    return a, b


def _lru_kernel(x_ref, g_ref, cw_ref, cb_ref, wa_ref, ba_ref, wx_ref, bx_ref, lam_ref, h0_ref,
                y_ref, hf_ref, xc_scr, hfw_scr, *, rows: Rows, br):
    blk = pl.program_id(0)
    is_ctx = blk * br < rows.ctx_rows
    ts = rows.ctx_len
    n_tiles = br // ts
    seq_len = jnp.where(is_ctx, rows.ctx_len, rows.lat_len)
    xc_scr[...] = _conv_taps(x_ref[...], cw_ref[0], seq_len) + cb_ref[0]

    for dd in range(2):
        reverse = dd == 1
        sp_lam = _softplus(-lam_ref[0, dd])

        def tile(n, carry, dd=dd, reverse=reverse, sp_lam=sp_lam):
            t = (n_tiles - 1 - n) if reverse else n
            r0 = pl.multiple_of(t * ts, ts)
            xc = xc_scr[pl.ds(r0, ts), :]
            r = _sigmoid(_bdot(xc, wa_ref[0, dd, 0]) + ba_ref[0, dd])
            gi = _sigmoid(_bdot(xc, wx_ref[0, dd, 0]) + bx_ref[0, dd])
            log_a = -LRU_C * r * sp_lam
            a = jnp.exp(log_a)
            b = jnp.sqrt(-jnp.tanh(log_a) * (a * a + 1.0)) * (gi * xc)
            a_cum, h = _tile_scan(a, b, reverse)
            carry = jnp.where(is_ctx, 0.0, carry)
            h = h + a_cum * carry
            last = h[0:1, :] if reverse else h[ts - 1:ts, :]
            hf_ref[0, dd, pl.ds(t, 1), :] = last
            if reverse:
                g = g_ref[pl.ds(r0, ts), :].astype(F32)
                y_ref[pl.ds(r0, ts), :] = ((hfw_scr[pl.ds(r0, ts), :] + h) * _gelu_tanh(g)).astype(BF16)
            else:
                hfw_scr[pl.ds(r0, ts), :] = h
            return last

        lax.fori_loop(0, n_tiles, tile, h0_ref[0, 0, dd])


def _lru(proj_f32, proj_big, conv_w, conv_b, wa, ba, wx, bx, lam, h0, l, rows: Rows):
    br = rows.lat_len
    nt = rows.total
    n_blocks = nt // br
    n_ctx_blocks = rows.ctx_rows // br
    n_tiles = br // rows.ctx_len
    gcol = OFF_GB // BW_B

    def lat_seq(b):
        return jnp.maximum(b - n_ctx_blocks, 0)

    vec = lambda a: a.reshape(DEPTH, 2, 1, W_B)
    return pl.pallas_call(
        functools.partial(_lru_kernel, rows=rows, br=br),
        grid=(n_blocks, NB_B),
        in_specs=[
            pl.BlockSpec((br, BW_B), lambda b, j: (b, j)),
            pl.BlockSpec((br, BW_B), lambda b, j: (b, gcol + j)),
            pl.BlockSpec((1, CONV_W, BW_B), lambda b, j: (l, 0, j)),
            pl.BlockSpec((1, 1, BW_B), lambda b, j: (l, 0, j)),
            pl.BlockSpec((1, 2, 1, BW_B, BW_B), lambda b, j: (l, 0, j, 0, 0)),
            pl.BlockSpec((1, 2, 1, BW_B), lambda b, j: (l, 0, 0, j)),
            pl.BlockSpec((1, 2, 1, BW_B, BW_B), lambda b, j: (l, 0, j, 0, 0)),
            pl.BlockSpec((1, 2, 1, BW_B), lambda b, j: (l, 0, 0, j)),
            pl.BlockSpec((1, 2, 1, BW_B), lambda b, j: (l, 0, 0, j)),
            pl.BlockSpec((1, 1, 2, 1, BW_B), lambda b, j: (lat_seq(b), l, 0, 0, j)),
        ],
        out_specs=[
            pl.BlockSpec((br, BW_B), lambda b, j: (b, j)),
            pl.BlockSpec((1, 2, n_tiles, BW_B), lambda b, j: (b, 0, 0, j)),
        ],
        out_shape=[
            jax.ShapeDtypeStruct((nt, W_B), BF16),
            jax.ShapeDtypeStruct((n_blocks, 2, n_tiles, W_B), F32),
        ],
        scratch_shapes=[pltpu.VMEM((br, BW_B), F32), pltpu.VMEM((br, BW_B), F32)],
        compiler_params=_cparams(("arbitrary", "arbitrary")),
        name="lru",
    )(proj_f32, proj_big, conv_w, conv_b.reshape(DEPTH, 1, W_B), wa, vec(ba), wx, vec(bx), vec(lam),
      h0.reshape(h0.shape[0], DEPTH, 2, 1, W_B))


def _route(lg, le):
    lane = lax.broadcasted_iota(jnp.int32, lg.shape, 1)
    neg = -jnp.inf
    lgm = jnp.where(lane < N_GROUPS, lg, neg)
    gmax = jnp.max(lgm, axis=-1, keepdims=True)
    p_grp = 1.0 / jnp.sum(jnp.exp(lgm - gmax), axis=-1, keepdims=True)
    g_sel = jnp.min(jnp.where(lgm == gmax, lane, LANES), axis=-1, keepdims=True)
    in_grp = jnp.logical_and(lane >= g_sel * E_PER_GROUP, lane < (g_sel + 1) * E_PER_GROUP)
    lem = jnp.where(in_grp, le, neg)
    v1 = jnp.max(lem, axis=-1, keepdims=True)
    i1 = jnp.min(jnp.where(lem == v1, lane, LANES), axis=-1, keepdims=True)
    lem2 = jnp.where(lane == i1, neg, lem)
    v2 = jnp.max(lem2, axis=-1, keepdims=True)
    i2 = jnp.min(jnp.where(lem2 == v2, lane, LANES), axis=-1, keepdims=True)
    e2 = jnp.exp(v2 - v1)
    w1 = p_grp / (1.0 + e2)
    w2 = p_grp * e2 / (1.0 + e2)
    gate = jnp.where(lane == i1, w1, 0.0) + jnp.where(lane == i2, w2, 0.0)
    local = gate
    for g in range(1, N_GROUPS):
        local = jnp.where(g_sel == g, pltpu.roll(gate, LANES - g * E_PER_GROUP, 1), local)
    local = jnp.where(lane < E_PER_GROUP, local, 0.0)
    hi = local.astype(BF16).astype(F32)
    rest = local - hi
    mid = rest.astype(BF16).astype(F32)
    lo = (rest - mid).astype(BF16).astype(F32)
    packed = hi + pltpu.roll(mid, E_PER_GROUP, 1) + pltpu.roll(lo, 2 * E_PER_GROUP, 1)
    return packed, g_sel


def _group_rank(g_sel, cnt_scr):
    tm = g_sel.shape[0]
    lane = lax.broadcasted_iota(jnp.int32, (tm, LANES), 1)
    onehot = jnp.where(lane == g_sel, 1.0, 0.0)
    ii = lax.broadcasted_iota(jnp.int32, (tm, tm), 0)
    jj = lax.broadcasted_iota(jnp.int32, (tm, tm), 1)
    before = jnp.where(ii > jj, 1.0, 0.0).astype(BF16)
    seen = jnp.dot(before, onehot.astype(BF16), preferred_element_type=F32) + cnt_scr[0:1, :]
    cnt_scr[...] = cnt_scr[...] + jnp.sum(onehot, axis=0, keepdims=True)
    return jnp.sum(seen * onehot, axis=-1, keepdims=True)


def _merge_kernel(*refs, n_x, n_ctx_tiles):
    x_refs = refs[:n_x]
    (mod_ref, oa_ref, z_ref, yb_ref, hc_ref, oc_ref, gt_ref, dn_ref, mn_ref,
     wpa_ref, wpb_ref, wpc_ref, wout_ref, g1_ref, b1_ref, wrh_ref, wrl_ref, br_ref,
     x1_ref, h2_ref, gate_ref, meta_ref, cnt_scr) = refs[n_x:]
    oa = oa_ref[0] + oa_ref[1]
    hc = hc_ref[0] + hc_ref[1]
    ya, yc = [], []
    for h in range(N_HEADS):
        sl = slice(h * HEAD_DIM, (h + 1) * HEAD_DIM)
        o_h = oa[:, sl]
        o_h = o_h * lax.rsqrt(jnp.mean(o_h * o_h, axis=-1, keepdims=True) + RMS_EPS) * dn_ref[...]
        ya.append((o_h * _silu(z_ref[:, sl].astype(F32))).astype(BF16))
        c_h = _ln(hc[:, sl]) * mn_ref[:, sl]
        yc.append((_sigmoid(oc_ref[:, sl].astype(F32)) * c_h).astype(BF16))
    ya = jnp.concatenate(ya, axis=-1)
    yc = jnp.concatenate(yc, axis=-1)
    ga = _sigmoid(gt_ref[:, 0:D_MODEL].astype(F32))
    gb = _sigmoid(gt_ref[:, D_MODEL:2 * D_MODEL].astype(F32))
    gc = _sigmoid(gt_ref[:, 2 * D_MODEL:3 * D_MODEL].astype(F32))
    d = functools.partial(jnp.dot, preferred_element_type=F32)
    merged = ga * d(ya, wpa_ref[...]) + gb * d(yb_ref[...], wpb_ref[...]) + gc * d(yc, wpc_ref[...])
    mixed = d(merged.astype(BF16), wout_ref[...])
    gate1 = mod_ref[0, 2:3, :]
    shift2 = mod_ref[0, 3:4, :]
    scale2 = mod_ref[0, 4:5, :]
    x1 = _ln(DN_ALPHA * _read_x(x_refs, n_ctx_tiles) + gate1 * mixed) * g1_ref[...] + b1_ref[...]
    x1_ref[...] = x1
    h2 = _ln(x1) * (1.0 + scale2) + shift2
    h2_ref[...] = h2.astype(BF16)
    hh, hl = _split2(h2)
    d = functools.partial(jnp.dot, preferred_element_type=F32)
    logits = d(hh, wrh_ref[...]) + (d(hh, wrl_ref[...]) + d(hl, wrh_ref[...])) + br_ref[...]
    packed, g_sel = _route(logits, pltpu.roll(logits, LANES - 64, 1))

    @pl.when(pl.program_id(0) == 0)
    def _():
        cnt_scr[...] = jnp.zeros(cnt_scr.shape, F32)

    rank = _group_rank(g_sel, cnt_scr)
    lane = lax.broadcasted_iota(jnp.int32, packed.shape, 1)
    route = jnp.where(lane == ROUTE_GROUP_LANE, g_sel.astype(F32), jnp.where(lane == ROUTE_RANK_LANE, rank, packed))
    gate_ref[...] = route
    meta_ref[0] = route.T[ROUTE_GROUP_LANE:ROUTE_GROUP_LANE + SUBLANES, :]


def _merge(x_parts, mod_l, o_a, proj_big, y_b, h_c, dn, mn, wpa, wpb, wpc, wout, g1, b1, wr, br, rows: Rows, tm):
    nt = rows.total
    row = lambda i: (i, 0)
    const = lambda i: (0, 0)
    cw = W_MIX
    return pl.pallas_call(
        functools.partial(_merge_kernel, n_x=len(x_parts), n_ctx_tiles=rows.ctx_rows // tm),
        grid=(nt // tm,),
        in_specs=[
            *_x_specs(x_parts, rows, tm),
            pl.BlockSpec((1, 6, D_MODEL), lambda i: (_cond_index(i * tm, rows), 0, 0)),
            pl.BlockSpec((2, tm, cw), lambda i: (0, i, 0)),
            pl.BlockSpec((tm, cw), lambda i: (i, OFF_Z // cw)),
            pl.BlockSpec((tm, cw), row),
            pl.BlockSpec((2, tm, cw), lambda i: (0, i, 0)),
            pl.BlockSpec((tm, cw), lambda i: (i, OFF_OC // cw)),
            pl.BlockSpec((tm, 3 * D_MODEL), lambda i: (i, OFF_GATES // (3 * D_MODEL))),
            pl.BlockSpec((1, HEAD_DIM), const),
            pl.BlockSpec((1, cw), const),
            pl.BlockSpec((cw, D_MODEL), const),
            pl.BlockSpec((cw, D_MODEL), const),
            pl.BlockSpec((cw, D_MODEL), const),
            pl.BlockSpec((D_MODEL, D_MODEL), const),
            pl.BlockSpec((1, D_MODEL), const),
            pl.BlockSpec((1, D_MODEL), const),
            pl.BlockSpec((D_MODEL, LANES), const),
            pl.BlockSpec((D_MODEL, LANES), const),
            pl.BlockSpec((1, LANES), const),
        ],
        out_specs=[
            pl.BlockSpec((tm, D_MODEL), row),
            pl.BlockSpec((tm, D_MODEL), row),
            pl.BlockSpec((tm, LANES), row),
            pl.BlockSpec((1, SUBLANES, tm), lambda i: (i, 0, 0)),
        ],
        out_shape=[
            jax.ShapeDtypeStruct((nt, D_MODEL), F32),
            jax.ShapeDtypeStruct((nt, D_MODEL), BF16),
            jax.ShapeDtypeStruct((nt, LANES), F32),
            jax.ShapeDtypeStruct((nt // tm, SUBLANES, tm), F32),
        ],
        scratch_shapes=[pltpu.VMEM((SUBLANES, LANES), F32)],
        compiler_params=_cparams(("arbitrary",)),
        name="merge",
    )(*x_parts, mod_l, o_a, proj_big, y_b, h_c, proj_big, proj_big, dn, mn, wpa, wpb, wpc, wout, g1, b1, *_split2(wr), br)


MOE_TILE = 512


class MoePlan(NamedTuple):
    dest_row: jax.Array
    dest_col: jax.Array
    tile_group: jax.Array
    by_tile: tuple
    by_block: tuple


def _pair_list(mask, n_pairs, minor):
    flat = mask.reshape(-1)
    cnt = jnp.sum(flat.astype(jnp.int32))
    idx = jnp.nonzero(flat, size=n_pairs, fill_value=0)[0].astype(jnp.int32)
    pos = jnp.arange(n_pairs, dtype=jnp.int32)
    valid = pos < cnt
    idx = jnp.where(valid, idx, idx[jnp.maximum(cnt - 1, 0)])
    major, mnr = idx // minor, idx % minor
    prev = jnp.concatenate([jnp.full((1,), -1, jnp.int32), major[:-1]])
    nxt = jnp.concatenate([major[1:], jnp.full((1,), -1, jnp.int32)])
    first = jnp.logical_and(valid, major != prev)
    last = jnp.logical_and(valid, jnp.logical_or(major != nxt, pos == cnt - 1))
    i32 = lambda a: a.astype(jnp.int32)
    return major, mnr, i32(first), i32(last), i32(valid)


def _moe_plan(g_sel, rank, nt):
    n_blocks = nt // MOE_TILE
    n_tiles = n_blocks + N_GROUPS
    n_pairs = n_tiles + N_GROUPS * n_blocks
    oh = (g_sel[:, None] == jnp.arange(N_GROUPS, dtype=jnp.int32)[None, :]).astype(jnp.int32)
    counts = jnp.sum(oh, axis=0)
    tiles_g = (counts + MOE_TILE - 1) // MOE_TILE
    tile_end = jnp.cumsum(tiles_g)
    tile_start = tile_end - tiles_g
    dest = jnp.sum(oh * tile_start[None, :], axis=1) * MOE_TILE + rank
    tile_ids = jnp.arange(n_tiles, dtype=jnp.int32)
    tile_group = jnp.minimum(jnp.sum((tile_ids[:, None] >= tile_end[None, :]).astype(jnp.int32), axis=1),
                             N_GROUPS - 1)
    t_oh = ((dest // MOE_TILE)[:, None] == tile_ids[None, :]).astype(F32)
    b_oh = ((jnp.arange(nt, dtype=jnp.int32) // MOE_TILE)[:, None]
            == jnp.arange(n_blocks, dtype=jnp.int32)[None, :]).astype(F32)
    mask = jnp.einsum('tj,tb->jb', t_oh, b_oh) > 0.5
    mask = mask.at[:, 0].set(jnp.logical_or(mask[:, 0], tile_ids >= tile_end[-1]))
    tj, tb, tf, tl, tv = _pair_list(mask, n_pairs, n_blocks)
    cb, cj, cf, cl, cv = _pair_list(mask.T, n_pairs, n_tiles)
    return MoePlan(dest.reshape(n_blocks, 1, MOE_TILE),
                   jnp.broadcast_to(dest[:, None], (nt, LANES)),
                   tile_group, (tj, tb, tf, tl, tv), (cj, cb, cf, cl, cv))


def _moe_experts_kernel(pj_ref, pb_ref, pf_ref, pl_ref, pv_ref, tg_ref,
                        h_ref, dest_ref, route_ref, w1_ref, w3_ref, w2_ref, y_ref, x_scr, g_scr):
    p = pl.program_id(0)

    @pl.when(pf_ref[p] == 1)
    def _():
        x_scr[...] = jnp.zeros(x_scr.shape, F32)
        g_scr[...] = jnp.zeros(g_scr.shape, F32)

    @pl.when(pv_ref[p] == 1)
    def _():
        row = lax.broadcasted_iota(jnp.int32, (MOE_TILE, MOE_TILE), 0) + pj_ref[p] * MOE_TILE
        sel = jnp.where(dest_ref[0] == row, 1.0, 0.0).astype(BF16)
        d = functools.partial(jnp.dot, preferred_element_type=F32)
        x_scr[...] += d(sel, h_ref[...])
        g_scr[...] += d(sel, route_ref[...].astype(BF16))

    @pl.when(pl_ref[p] == 1)
    def _():
        x = x_scr[...].astype(BF16)
        g = g_scr[...]
        gate = g + (pltpu.roll(g, LANES - E_PER_GROUP, 1) + pltpu.roll(g, LANES - 2 * E_PER_GROUP, 1))
        d = functools.partial(jnp.dot, preferred_element_type=F32)
        acc = jnp.zeros((MOE_TILE, D_MODEL), F32)
        for e in range(E_PER_GROUP):
            hid = _silu(d(x, w1_ref[0, e])) * d(x, w3_ref[0, e]) * gate[:, e:e + 1]
            acc = acc + d(hid.astype(BF16), w2_ref[0, e])
        y_ref[...] = acc.astype(BF16)


def _moe_experts(h2, route, plan: MoePlan, w1, w3, w2, l, nt):
    n_blocks = nt // MOE_TILE
    n_tiles = n_blocks + N_GROUPS
    tj, tb, tf, tl, tv = plan.by_tile
    n_pairs = tj.shape[0]
    wmap = lambda p, pj, pb, pf, pl_, pv, tg: (l * N_GROUPS + tg[pj[p]], 0, 0, 0)
    grid_spec = pltpu.PrefetchScalarGridSpec(
        num_scalar_prefetch=6,
        grid=(n_pairs,),
        in_specs=[
            pl.BlockSpec((MOE_TILE, D_MODEL), lambda p, pj, pb, *_: (pb[p], 0)),
            pl.BlockSpec((1, 1, MOE_TILE), lambda p, pj, pb, *_: (pb[p], 0, 0)),
            pl.BlockSpec((MOE_TILE, LANES), lambda p, pj, pb, *_: (pb[p], 0)),
            pl.BlockSpec((1, E_PER_GROUP, D_MODEL, D_EXPERT), wmap),
            pl.BlockSpec((1, E_PER_GROUP, D_MODEL, D_EXPERT), wmap),
            pl.BlockSpec((1, E_PER_GROUP, D_EXPERT, D_MODEL), wmap),
        ],
        out_specs=pl.BlockSpec((MOE_TILE, D_MODEL), lambda p, pj, *_: (pj[p], 0)),
        scratch_shapes=[pltpu.VMEM((MOE_TILE, D_MODEL), F32), pltpu.VMEM((MOE_TILE, LANES), F32)],
    )
    return pl.pallas_call(
        _moe_experts_kernel,
        grid_spec=grid_spec,
        out_shape=jax.ShapeDtypeStruct((n_tiles * MOE_TILE, D_MODEL), BF16),
        compiler_params=_cparams(("arbitrary",)),
        name="moe_experts",
    )(tj, tb, tf, tl, tv, plan.tile_group, h2, plan.dest_row, route, w1, w3, w2)


def _moe_combine_kernel(cj_ref, cb_ref, cf_ref, cl_ref, cv_ref,
                        y_ref, dest_ref, x1_ref, mod_ref, g2_ref, b2_ref, o_ref, acc_scr):
    p = pl.program_id(0)

    @pl.when(cf_ref[p] == 1)
    def _():
        acc_scr[...] = jnp.zeros(acc_scr.shape, F32)

    @pl.when(cv_ref[p] == 1)
    def _():
        col = lax.broadcasted_iota(jnp.int32, (MOE_TILE, MOE_TILE), 1) + cj_ref[p] * MOE_TILE
        sel = jnp.where(dest_ref[:, 0:1] == col, 1.0, 0.0).astype(BF16)
        acc_scr[...] += jnp.dot(sel, y_ref[...], preferred_element_type=F32)

    @pl.when(cl_ref[p] == 1)
    def _():
        gate2 = mod_ref[0, 5:6, :]
        o_ref[...] = _ln(DN_ALPHA * x1_ref[...] + gate2 * acc_scr[...]) * g2_ref[...] + b2_ref[...]


def _moe_combine(y, plan: MoePlan, x1, mod_l, g2, b2, rows: Rows):
    nt = rows.total
    cj, cb, cf, cl, cv = plan.by_block
    n_pairs = cj.shape[0]
    blk = lambda p, cj, cb, *_: (cb[p], 0)
    const = lambda p, *_: (0, 0)
    grid_spec = pltpu.PrefetchScalarGridSpec(
        num_scalar_prefetch=5,
        grid=(n_pairs,),
        in_specs=[
            pl.BlockSpec((MOE_TILE, D_MODEL), lambda p, cj, *_: (cj[p], 0)),
            pl.BlockSpec((MOE_TILE, LANES), blk),
            pl.BlockSpec((MOE_TILE, D_MODEL), blk),
            pl.BlockSpec((1, 6, D_MODEL), lambda p, cj, cb, *_: (_cond_index(cb[p] * MOE_TILE, rows), 0, 0)),
            pl.BlockSpec((1, D_MODEL), const),
            pl.BlockSpec((1, D_MODEL), const),
        ],
        out_specs=pl.BlockSpec((MOE_TILE, D_MODEL), blk),
        scratch_shapes=[pltpu.VMEM((MOE_TILE, D_MODEL), F32)],
    )
    return pl.pallas_call(
        _moe_combine_kernel,
        grid_spec=grid_spec,
        out_shape=jax.ShapeDtypeStruct((nt, D_MODEL), F32),
        compiler_params=_cparams(("arbitrary",)),
        name="moe_combine",
    )(cj, cb, cf, cl, cv, y, plan.dest_col, x1, mod_l, g2, b2)


class Tiles(NamedTuple):
    token_rows: int
    proj_cols: int
    merge_rows: int


def _tile_config(rows: Rows) -> Tiles:
    token_rows = min(1024, rows.lat_len)
    proj_cols = 1280
    assert N_BIG % proj_cols == 0 and proj_cols % (2 * LANES) == 0
    return Tiles(token_rows, proj_cols, min(256, rows.lat_len))


def _grid_pos_embed(n_tokens):
    rows = n_tokens // GRID_W
    quarter = D_MODEL // 4
    freqs = jnp.exp(-math.log(POS_BASE) * jnp.arange(quarter, dtype=F32) / quarter)
    ar = jnp.arange(rows, dtype=F32).reshape(-1, 1) * freqs
    ac = jnp.arange(GRID_W, dtype=F32).reshape(-1, 1) * freqs
    row_half = jnp.concatenate([jnp.sin(ar), jnp.cos(ar)], axis=-1)
    col_half = jnp.concatenate([jnp.sin(ac), jnp.cos(ac)], axis=-1)
    return jnp.concatenate([jnp.repeat(row_half, GRID_W, axis=0), jnp.tile(col_half, (rows, 1))], axis=-1)


def _pack_w_in(w_in_l):
    sizes = (3 * W_MIX, W_B, W_MIX, 8, 8, W_B, W_MIX, W_MIX, W_MIX, W_MIX, 8, 8, 3 * D_MODEL)
    parts, start = [], 0
    for s in sizes:
        parts.append(w_in_l[:, start:start + s])
        start += s
    qkv_a, x_b, z_a, beta, alpha, g_b, q_c, k_c, v_c, o_c, i_c, f_c, gates = parts
    big = jnp.concatenate([gates, qkv_a, q_c, k_c, v_c, z_a, g_b, o_c], axis=1).astype(BF16)
    pad = jnp.zeros((D_MODEL, LANES - 32), F32)
    small = jnp.concatenate([x_b, beta, alpha, i_c, f_c, pad], axis=1).astype(BF16)
    return big, small


def _lane_row(vals, off):
    rows_ = [jnp.concatenate([jnp.zeros((o,), F32), v.reshape(-1).astype(F32), jnp.zeros((LANES - o - 8,), F32)])
             for v, o in zip(vals, off)]
    return jnp.stack(rows_ + [jnp.zeros((LANES,), F32)] * (8 - len(rows_)))


def kernel(x_prompt, x_sample, state_delta, state_lru, state_mlstm_C, state_mlstm_n, state_mlstm_m, c, c_ctx,
           w_mod, b_mod, w_in, conv_a, delta_a_log, delta_dt_bias, delta_norm, conv_b_w, conv_b_b,
           lru_wa, lru_ba, lru_wx, lru_bx, lru_lambda, mlstm_bi, mlstm_bf, mlstm_norm,
           w_pa, w_pb, w_pc, w_out, ln1_g, ln1_b, ln2_g, ln2_b, w_rg, b_rg, w_re, b_re, w_e1, w_e3, w_e2):
    n_ctx, ctx_len, _ = x_prompt.shape
    n_lat, lat_len, _ = x_sample.shape
    rows = Rows(n_ctx, ctx_len, n_lat, lat_len)
    assert rows.ctx_rows % UNIT == 0 and lat_len % UNIT == 0 and UNIT % ctx_len == 0
    assert rows.ctx_rows % lat_len == 0 and ctx_len % CHUNK == 0 and n_lat <= 7
    assert rows.ctx_rows % MOE_TILE == 0 and lat_len % MOE_TILE == 0

    tiles = _tile_config(rows)
    pos = _grid_pos_embed(lat_len)
    x_parts = (x_prompt.reshape(rows.ctx_rows, D_MODEL), x_sample.reshape(n_lat * lat_len, D_MODEL), pos)

    cond8 = jnp.concatenate([c_ctx[None, :], c, jnp.zeros((8 - 1 - n_lat, D_MODEL), F32)], axis=0)
    mod = _modulation(cond8, w_mod, b_mod).reshape(DEPTH, 8, 6, D_MODEL)

    m_bcast = jnp.broadcast_to(state_mlstm_m[..., None], state_mlstm_m.shape + (HEAD_DIM,))
    w_e1g = w_e1.astype(BF16).reshape(DEPTH * N_GROUPS, E_PER_GROUP, D_MODEL, D_EXPERT)
    w_e3g = w_e3.astype(BF16).reshape(DEPTH * N_GROUPS, E_PER_GROUP, D_MODEL, D_EXPERT)
    w_e2g = w_e2.astype(BF16).reshape(DEPTH * N_GROUPS, E_PER_GROUP, D_EXPERT, D_MODEL)

    finals = []
    for l in range(DEPTH):
        w_big, w_small = _pack_w_in(w_in[l])
        par = _lane_row([delta_a_log[l], delta_dt_bias[l], mlstm_bi[l], mlstm_bf[l]],
                        [SM_ALPHA, SM_ALPHA, SM_I, SM_F])
        proj_big, proj_f32 = _projection(x_parts, mod[l], w_big, w_small, rows, tiles.token_rows, tiles.proj_cols)
        qkv_c = _conv_a(proj_big, conv_a, l, rows, lat_len)
        o_a, sf_a = _delta(qkv_c, proj_f32, par, state_delta, l, rows)
        y_b, hf_b = _lru(proj_f32, proj_big, conv_b_w, conv_b_b, lru_wa, lru_ba, lru_wx, lru_bx, lru_lambda,
                         state_lru, l, rows)
        h_c, cf, nf, mf = _mlstm(proj_big, proj_f32, par, state_mlstm_C, state_mlstm_n, m_bcast, l, rows)
        lane_pack = lambda g, e: jnp.concatenate(
            [g, jnp.zeros(g.shape[:-1] + (64 - N_GROUPS,), F32), e,
             jnp.zeros(g.shape[:-1] + (LANES - 64 - N_EXPERTS,), F32)], axis=-1)
        wr = lane_pack(w_rg[l], w_re[l])
        br = lane_pack(b_rg[l], b_re[l]).reshape(1, LANES)
        x1, h2, gate, meta = _merge(x_parts, mod[l], o_a, proj_big, y_b, h_c,
                                    delta_norm[l].reshape(1, HEAD_DIM), mlstm_norm[l].reshape(1, W_MIX),
                                    w_pa[l].astype(BF16), w_pb[l].astype(BF16), w_pc[l].astype(BF16),
                                    w_out[l].astype(BF16), ln1_g[l].reshape(1, D_MODEL),
                                    ln1_b[l].reshape(1, D_MODEL), wr, br, rows, tiles.merge_rows)
        meta_row = lambda r: meta[:, r, :].reshape(rows.total).astype(jnp.int32)
        plan = _moe_plan(meta_row(0), meta_row(ROUTE_RANK_LANE - ROUTE_GROUP_LANE), rows.total)
        y_moe = _moe_experts(h2, gate, plan, w_e1g, w_e3g, w_e2g, l, rows.total)
        x = _moe_combine(y_moe, plan, x1, mod[l], ln2_g[l].reshape(1, D_MODEL), ln2_b[l].reshape(1, D_MODEL), rows)
        x_parts = (x,)
        n_ctx_blocks = rows.ctx_rows // lat_len
        lru_fin = jnp.swapaxes(hf_b[:n_ctx_blocks], 1, 2).reshape(n_ctx, 2, W_B)
        finals.append((sf_a, lru_fin, cf, nf[:, :, :, 0, :], mf[:, :, :, 0, 0]))

    new_delta, new_lru, new_mc, new_mn, new_mm = (jnp.stack([f[i] for f in finals], axis=1) for i in range(5))
    y_prompt = x[:rows.ctx_rows].reshape(n_ctx, ctx_len, D_MODEL)
    y_sample = x[rows.ctx_rows:].reshape(n_lat, lat_len, D_MODEL)
    return (y_prompt, y_sample, new_delta, new_lru, new_mc, new_mn, new_mm)
```

```python
import functools
import math
from typing import NamedTuple

import jax
import jax.numpy as jnp
from jax import lax
from jax.experimental import pallas as pl
from jax.experimental.pallas import tpu as pltpu

F32 = jnp.float32
BF16 = jnp.bfloat16

D_MODEL = 1024
DEPTH = 2
GRID_W = 64
POS_BASE = 10000.0
CONV_W = 4
LN_EPS = 1e-5
RMS_EPS = 1e-6
N_HEADS = 4
HEAD_DIM = 128
W_MIX = N_HEADS * HEAD_DIM
CHUNK = 64
W_B = 512
NB_B = 4
BW_B = W_B // NB_B
LRU_C = 8.0
N_GROUPS = 4
E_PER_GROUP = 8
N_EXPERTS = N_GROUPS * E_PER_GROUP
D_EXPERT = 256
DN_ALPHA = (2 * DEPTH) ** 0.25

LANES = 128
SUBLANES = 8
VMEM_LIMIT = 56 * 1024 * 1024

OFF_GATES, OFF_QKV, OFF_QC, OFF_KC, OFF_VC, OFF_Z, OFF_GB, OFF_OC = 0, 3072, 4608, 5120, 5632, 6144, 6656, 7168
N_BIG = OFF_OC + W_MIX
N_F32 = W_B + LANES
SM_BETA, SM_ALPHA, SM_I, SM_F = 0, 8, 16, 24
ROUTE_GROUP_LANE = 24
ROUTE_RANK_LANE = 25


class Rows(NamedTuple):
    n_ctx: int
    ctx_len: int
    n_lat: int
    lat_len: int

    @property
    def ctx_rows(self):
        return self.n_ctx * self.ctx_len

    @property
    def total(self):
        return self.ctx_rows + self.n_lat * self.lat_len


def _cparams(sem):
    return pltpu.CompilerParams(dimension_semantics=sem, vmem_limit_bytes=VMEM_LIMIT)


def _sigmoid(x):
    return 1.0 / (1.0 + jnp.exp(-x))


def _silu(x):
    return x * _sigmoid(x)


def _softplus(x):
    return jnp.maximum(x, 0.0) + jnp.log1p(jnp.exp(-jnp.abs(x)))


def _ln(x):
    mu = jnp.mean(x, axis=-1, keepdims=True)
    xc = x - mu
    var = jnp.mean(xc * xc, axis=-1, keepdims=True)
    return xc * lax.rsqrt(var + LN_EPS)


def _bdot(a, b):
    return jnp.dot(a.astype(BF16), b.astype(BF16), preferred_element_type=F32)


def _bdot_nt(a, b):
    return lax.dot_general(a.astype(BF16), b.astype(BF16), (((1,), (1,)), ((), ())), preferred_element_type=F32)


def _bdot_tn(a, b):
    return lax.dot_general(a.astype(BF16), b.astype(BF16), (((0,), (0,)), ((), ())), preferred_element_type=F32)


def _split2(a):
    hi = a.astype(BF16)
    lo = (a - hi.astype(F32)).astype(BF16)
    return hi, lo


def _split3(a):
    hi = a.astype(BF16)
    r = a - hi.astype(F32)
    mid = r.astype(BF16)
    lo = (r - mid.astype(F32)).astype(BF16)
    return hi, mid, lo


def _dot3(a, b):
    ah, al = _split2(a)
    bh, bl = _split2(b)
    d = functools.partial(jnp.dot, preferred_element_type=F32)
    return d(ah, bh) + (d(ah, bl) + d(al, bh))


def _dot_exact_lhs(m_bf16, x):
    xh, xm, xl = _split3(x)
    d = functools.partial(jnp.dot, preferred_element_type=F32)
    return d(m_bf16, xh) + (d(m_bf16, xm) + d(m_bf16, xl))


def _cond_index(row0, rows: Rows):
    return jnp.maximum(0, (row0 - rows.ctx_rows + rows.lat_len) // rows.lat_len)


def _mod_kernel(c_ref, w_ref, b_ref, o_ref):
    o_ref[0] = _bdot(_silu(c_ref[...]), w_ref[0]) + b_ref[0]


def _modulation(cond8, w_mod, b_mod):
    tn = 1024
    n6 = 6 * D_MODEL
    return pl.pallas_call(
        _mod_kernel,
        grid=(DEPTH, n6 // tn),
        in_specs=[
            pl.BlockSpec((8, D_MODEL), lambda l, n: (0, 0)),
            pl.BlockSpec((1, D_MODEL, tn), lambda l, n: (l, 0, n)),
            pl.BlockSpec((1, 1, tn), lambda l, n: (l, 0, n)),
        ],
        out_specs=pl.BlockSpec((1, 8, tn), lambda l, n: (l, 0, n)),
        out_shape=jax.ShapeDtypeStruct((DEPTH, 8, n6), F32),
        compiler_params=_cparams(("arbitrary", "arbitrary")),
        name="modulation",
    )(cond8, w_mod, b_mod.reshape(DEPTH, 1, n6))


def _x_specs(x_parts, rows: Rows, tm):
    if len(x_parts) == 1:
        return [pl.BlockSpec((tm, D_MODEL), lambda i, *_: (i, 0))]
    n_ctx_tiles = rows.ctx_rows // tm
    pos_tiles = rows.lat_len // tm
    return [
        pl.BlockSpec((tm, D_MODEL), lambda i, *_: (jnp.minimum(i, n_ctx_tiles - 1), 0)),
        pl.BlockSpec((tm, D_MODEL), lambda i, *_: (jnp.maximum(i - n_ctx_tiles, 0), 0)),
        pl.BlockSpec((tm, D_MODEL), lambda i, *_: (jnp.maximum(i - n_ctx_tiles, 0) % pos_tiles, 0)),
    ]


def _read_x(x_refs, n_ctx_tiles):
    if len(x_refs) == 1:
        return x_refs[0][...]
    xp_ref, xs_ref, pos_ref = x_refs
    return jnp.where(pl.program_id(0) < n_ctx_tiles, xp_ref[...], xs_ref[...] + pos_ref[...])


def _proj_kernel(*refs, n_x, n_ctx_tiles):
    x_refs = refs[:n_x]
    mod_ref, wb_ref, wf_ref, ob_ref, of_ref, h_scr = refs[n_x:]
    n = pl.program_id(1)

    @pl.when(n == 0)
    def _():
        shift1 = mod_ref[0, 0:1, :]
        scale1 = mod_ref[0, 1:2, :]
        h = (_ln(_read_x(x_refs, n_ctx_tiles)) * (1.0 + scale1) + shift1).astype(BF16)
        h_scr[...] = h
        of_ref[...] = jnp.dot(h, wf_ref[...], preferred_element_type=F32)

    ob_ref[...] = jnp.dot(h_scr[...], wb_ref[...], preferred_element_type=F32).astype(BF16)


def _projection(x_parts, mod_l, w_big, w_f32, rows: Rows, tm, tn):
    nt = rows.total
    return pl.pallas_call(
        functools.partial(_proj_kernel, n_x=len(x_parts), n_ctx_tiles=rows.ctx_rows // tm),
        grid=(nt // tm, N_BIG // tn),
        in_specs=[
            *_x_specs(x_parts, rows, tm),
            pl.BlockSpec((1, 6, D_MODEL), lambda i, n: (_cond_index(i * tm, rows), 0, 0)),
            pl.BlockSpec((D_MODEL, tn), lambda i, n: (0, n)),
            pl.BlockSpec((D_MODEL, N_F32), lambda i, n: (0, 0)),
        ],
        out_specs=[
            pl.BlockSpec((tm, tn), lambda i, n: (i, n)),
            pl.BlockSpec((tm, N_F32), lambda i, n: (i, 0)),
        ],
        out_shape=[
            jax.ShapeDtypeStruct((nt, N_BIG), BF16),
            jax.ShapeDtypeStruct((nt, N_F32), F32),
        ],
        scratch_shapes=[pltpu.VMEM((tm, D_MODEL), BF16)],
        compiler_params=_cparams(("arbitrary", "arbitrary")),
        name="projection",
    )(*x_parts, mod_l, w_big, w_f32)


def _conv_masks(shape, seq_len):
    pos = lax.broadcasted_iota(jnp.int32, shape, 0) & (seq_len - 1)
    return (jnp.where(pos >= 1, 1.0, 0.0), jnp.where(pos <= seq_len - 2, 1.0, 0.0),
            jnp.where(pos <= seq_len - 3, 1.0, 0.0))


def _conv_taps(x, w_ref, masks):
    n = x.shape[0]
    m_prev, m_next, m_next2 = masks
    return (pltpu.roll(x, 1, 0) * (m_prev * w_ref[0:1, :]) + x * w_ref[1:2, :]
            + pltpu.roll(x, n - 1, 0) * (m_next * w_ref[2:3, :]) + pltpu.roll(x, n - 2, 0) * (m_next2 * w_ref[3:4, :]))


def _conv_a_kernel(x_ref, w_ref, o_ref, m_scr, *, rows: Rows, br):
    b = pl.program_id(0)
    j = pl.program_id(1)

    @pl.when(j == 0)
    def _():
        seq_len = jnp.where(b * br < rows.ctx_rows, rows.ctx_len, rows.lat_len)
        for t, m in enumerate(_conv_masks(m_scr.shape[1:], seq_len)):
            m_scr[t] = m

    y = _silu(_conv_taps(x_ref[...].astype(F32), w_ref[0], (m_scr[0], m_scr[1], m_scr[2])))
    nrm = lax.rsqrt(jnp.sum(y * y, axis=-1, keepdims=True) + RMS_EPS)
    fac = jnp.where(j < N_HEADS, nrm * HEAD_DIM ** -0.5, jnp.where(j < 2 * N_HEADS, nrm, 1.0))
    o_ref[...] = (y * fac).astype(BF16)


def _conv_a(proj_big, conv_w, l, rows: Rows, br):
    nt = rows.total
    ncol = 3 * N_HEADS
    return pl.pallas_call(
        functools.partial(_conv_a_kernel, rows=rows, br=br),
        grid=(nt // br, ncol),
        in_specs=[
            pl.BlockSpec((br, LANES), lambda b, j: (b, OFF_QKV // LANES + j)),
            pl.BlockSpec((1, CONV_W, LANES), lambda b, j: (l, 0, j)),
        ],
        out_specs=pl.BlockSpec((br, LANES), lambda b, j: (b, j)),
        out_shape=jax.ShapeDtypeStruct((nt, ncol * LANES), BF16),
        scratch_shapes=[pltpu.VMEM((CONV_W - 1, br, LANES), F32)],
        compiler_params=_cparams(("arbitrary", "arbitrary")),
        name="conv_a",
    )(proj_big, conv_w)


UNIT = 1024
NCH = UNIT // CHUNK


def _unit_layout(rows: Rows):
    n_ctx_units = rows.ctx_rows // UNIT
    lat_units = rows.lat_len // UNIT
    return n_ctx_units, lat_units, n_ctx_units + rows.n_lat * lat_units


def _unit_rowblock(u, d, rows: Rows):
    n_ctx_units, lat_units, _ = _unit_layout(rows)
    v = jnp.maximum(u - n_ctx_units, 0)
    b = v // lat_units
    j = v % lat_units
    jj = j + d * (lat_units - 1 - 2 * j)
    return jnp.where(u < n_ctx_units, u, n_ctx_units + b * lat_units + jj)


def _unit_lat_seq(u, rows: Rows):
    n_ctx_units, lat_units, _ = _unit_layout(rows)
    return jnp.maximum(u - n_ctx_units, 0) // lat_units


def _dir_masks(fwd):
    ii = lax.broadcasted_iota(jnp.int32, (CHUNK, CHUNK), 0)
    jj = lax.broadcasted_iota(jnp.int32, (CHUNK, CHUNK), 1)
    s = jnp.where(fwd, ii - jj, jj - ii)
    return s >= 0, s > 0, ii == jj


def _bmm(a, b):
    return jnp.einsum('gik,gkj->gij', a.astype(BF16), b.astype(BF16), preferred_element_type=F32)


def _bmm_nt(a, b):
    return jnp.einsum('gik,gjk->gij', a.astype(BF16), b.astype(BF16), preferred_element_type=F32)


def _bmm_tn(a, b):
    return jnp.einsum('gik,gij->gkj', a.astype(BF16), b.astype(BF16), preferred_element_type=F32)


def _chunk_cumsum(x, reverse):
    n = x.shape[0]
    pos = lax.broadcasted_iota(jnp.int32, x.shape, 0) & (CHUNK - 1)
    s = 1
    while s < CHUNK:
        if reverse:
            x = x + jnp.where(pos < CHUNK - s, pltpu.roll(x, n - s, 0), 0.0)
        else:
            x = x + jnp.where(pos >= s, pltpu.roll(x, s, 0), 0.0)
        s *= 2
    return x


def _dir_select(fwd, x):
    return jnp.where(fwd, x, pltpu.roll(x, LANES - N_HEADS, 1))


GROUP_CHUNKS = 8
GROUP_ROWS = GROUP_CHUNKS * CHUNK


def _group_columns(col_arr, row_arr, lane0):
    cols, rws = [], []
    for cc in range(GROUP_CHUNKS):
        for h in range(N_HEADS):
            cols.append(jnp.broadcast_to(col_arr[cc * CHUNK:(cc + 1) * CHUNK, lane0 + h:lane0 + h + 1],
                                         (CHUNK, HEAD_DIM)))
            if row_arr is not None:
                rws.append(jnp.broadcast_to(row_arr[lane0 + h:lane0 + h + 1, cc * CHUNK:(cc + 1) * CHUNK],
                                            (CHUNK, CHUNK)))
    return jnp.stack(cols), (jnp.stack(rws) if rws else None)


def _group_heads(ref, r0, col0):
    return jnp.stack([ref[pl.ds(r0 + cc * CHUNK, CHUNK), col0 + h * HEAD_DIM:col0 + (h + 1) * HEAD_DIM]
                      for cc in range(GROUP_CHUNKS) for h in range(N_HEADS)])


def _delta_kernel(qkv_ref, sm_ref, par_ref, s0_ref, o_ref, sf_ref,
                  s_scr, sa_scr, sb_scr, oq_scr, ov_scr, ge_scr, *, rows: Rows):
    d = pl.program_id(0)
    u = pl.program_id(1)
    n_ctx_units, lat_units, _ = _unit_layout(rows)
    seq_chunks = rows.ctx_len // CHUNK
    seq_per_unit = UNIT // rows.ctx_len
    fwd = d == 0
    is_ctx = u < n_ctx_units
    causal, strict, eye = _dir_masks(fwd)
    eye_f = jnp.where(eye, 1.0, 0.0)
    ii = lax.broadcasted_iota(jnp.int32, (CHUNK, CHUNK), 0)
    jj = lax.broadcasted_iota(jnp.int32, (CHUNK, CHUNK), 1)
    pair_masks = [jnp.logical_and((ii >> (s + 1)) == (jj >> (s + 1)), (ii >> s) != (jj >> s))
                  for s in range(CHUNK.bit_length() - 1)]
    ng = GROUP_CHUNKS * N_HEADS

    def pre(it, carry):
        r0 = pl.multiple_of(it * GROUP_ROWS, GROUP_ROWS)
        g0 = pl.multiple_of(it * ng, ng)
        sm = sm_ref[pl.ds(r0, GROUP_ROWS), :]
        g_all = -jnp.exp(par_ref[0:1, :]) * _softplus(sm + par_ref[1:2, :])
        gcum = _dir_select(fwd, jnp.where(fwd, _chunk_cumsum(g_all, False), _chunk_cumsum(g_all, True)))
        bsel = _dir_select(fwd, _sigmoid(sm))
        gi, grow = _group_columns(gcum, gcum.T, SM_ALPHA)
        beta = jnp.stack([bsel[cc * CHUNK:(cc + 1) * CHUNK, SM_BETA + h:SM_BETA + h + 1]
                          for cc in range(GROUP_CHUNKS) for h in range(N_HEADS)])
        q = _group_heads(qkv_ref, r0, 0)
        k = _group_heads(qkv_ref, r0, W_MIX)
        v = _group_heads(qkv_ref, r0, 2 * W_MIX)
        kf = k.astype(F32)
        decay = jnp.exp(jnp.where(causal, gi[:, :, :CHUNK] - grow, -jnp.inf))
        lmat = jnp.where(strict, beta * _bmm_nt(k, k) * decay, 0.0)
        eg = jnp.exp(gi)
        rhs = jnp.concatenate([v.astype(F32) * beta, kf * (beta * eg)], axis=-1)
        t = eye_f - jnp.where(pair_masks[0], lmat, 0.0)
        for pm in pair_masks[1:]:
            tb = t.astype(BF16)
            t = t - _bmm(_bmm(tb, jnp.where(pm, lmat, 0.0)), tb)
        t0 = t.astype(BF16)
        mh, ml = _split2(eye_f + lmat)
        resid = eye_f - (_bmm(mh, t0) + _bmm(ml, t0))
        t1 = t0.astype(F32) + _bmm(t0, resid)
        uu = _bmm(t1, rhs)
        qk = jnp.where(causal, _bmm_nt(q, k) * decay, 0.0)
        g_end = jnp.where(fwd, gi[:, CHUNK - 1:CHUNK, :], gi[:, 0:1, :])
        u_v = uu[:, :, :HEAD_DIM]
        u_k = uu[:, :, HEAD_DIM:]
        k_dec = kf * jnp.exp(g_end - gi)
        sa_scr[pl.ds(g0, ng)] = (-_bmm_tn(k_dec, u_k)).astype(BF16)
        sb_scr[pl.ds(g0, ng)] = _bmm_tn(k_dec, u_v)
        oq_scr[pl.ds(g0, ng)] = (q.astype(F32) * eg - _bmm(qk, u_k)).astype(BF16)
        ov_scr[pl.ds(g0, ng)] = _bmm(qk, u_v)
        ge_scr[pl.ds(g0, ng)] = jnp.broadcast_to(jnp.exp(g_end), (ng, 8, HEAD_DIM))
        return carry

    lax.fori_loop(0, NCH // GROUP_CHUNKS, pre, 0)

    first_lat = jnp.logical_and(u >= n_ctx_units, (u - n_ctx_units) % lat_units == 0)

    @pl.when(first_lat)
    def _():
        s_scr[...] = s0_ref[0, 0, 0]

    def step(n, carry):
        c = jnp.where(fwd, n, NCH - 1 - n)
        r0 = pl.multiple_of(c * CHUNK, CHUNK)
        hs = pl.ds(pl.multiple_of(c * N_HEADS, N_HEADS), N_HEADS)

        @pl.when(jnp.logical_and(is_ctx, n % seq_chunks == 0))
        def _():
            s_scr[...] = jnp.zeros(s_scr.shape, F32)

        s = s_scr[...]
        sb = s.astype(BF16)
        s_scr[...] = s * ge_scr[hs][:, 0:1, :] + (_bmm(sa_scr[hs], sb) + sb_scr[hs])
        o = _bmm(oq_scr[hs], sb) + ov_scr[hs]
        for h in range(N_HEADS):
            o_ref[0, pl.ds(r0, CHUNK), h * HEAD_DIM:(h + 1) * HEAD_DIM] = o[h]

        @pl.when(jnp.logical_and(is_ctx, n % seq_chunks == seq_chunks - 1))
        def _():
            sf_ref[c // seq_chunks, 0] = s_scr[...]

        return carry

    lax.fori_loop(0, NCH, step, 0)


def _delta(qkv_c, proj_f32, par, state, l, rows: Rows):
    n_ctx_units, lat_units, n_units = _unit_layout(rows)
    seq_per_unit = UNIT // rows.ctx_len
    nt = rows.total
    sm_col = W_B // LANES
    kern = functools.partial(_delta_kernel, rows=rows)
    hshape = (NCH * N_HEADS, CHUNK, HEAD_DIM)
    return pl.pallas_call(
        kern,
        grid=(2, n_units),
        in_specs=[
            pl.BlockSpec((UNIT, 3 * W_MIX), lambda d, u: (_unit_rowblock(u, d, rows), 0)),
            pl.BlockSpec((UNIT, LANES), lambda d, u: (_unit_rowblock(u, d, rows), sm_col)),
            pl.BlockSpec((8, LANES), lambda d, u: (0, 0)),
            pl.BlockSpec((1, 1, 1, N_HEADS, HEAD_DIM, HEAD_DIM),
                         lambda d, u: (_unit_lat_seq(u, rows), l, d, 0, 0, 0)),
        ],
        out_specs=[
            pl.BlockSpec((1, UNIT, W_MIX), lambda d, u: (d, _unit_rowblock(u, d, rows), 0)),
            pl.BlockSpec((seq_per_unit, 1, N_HEADS, HEAD_DIM, HEAD_DIM),
                         lambda d, u: (jnp.minimum(u, n_ctx_units - 1), d, 0, 0, 0)),
        ],
        out_shape=[
            jax.ShapeDtypeStruct((2, nt, W_MIX), F32),
            jax.ShapeDtypeStruct((rows.n_ctx, 2, N_HEADS, HEAD_DIM, HEAD_DIM), F32),
        ],
        scratch_shapes=[
            pltpu.VMEM((N_HEADS, HEAD_DIM, HEAD_DIM), F32),
            pltpu.VMEM((NCH * N_HEADS, HEAD_DIM, HEAD_DIM), BF16),
            pltpu.VMEM((NCH * N_HEADS, HEAD_DIM, HEAD_DIM), F32),
            pltpu.VMEM(hshape, BF16),
            pltpu.VMEM(hshape, F32),
            pltpu.VMEM((NCH * N_HEADS, 8, HEAD_DIM), F32),
        ],
        compiler_params=_cparams(("arbitrary", "arbitrary")),
        name="delta",
    )(qkv_c, proj_f32, par, state)


def _mlstm_kernel(big_ref, sm_ref, par_ref, c0_ref, n0_ref, m0_ref, o_ref, cf_ref, nf_ref, mf_ref,
                  c_scr, n_scr, m_scr, fi_scr, mi_scr, pv_scr, ps_scr, kv_scr, ks_scr, me_scr, *, rows: Rows):
    d = pl.program_id(0)
    u = pl.program_id(1)
    n_ctx_units, lat_units, _ = _unit_layout(rows)
    seq_chunks = rows.ctx_len // CHUNK
    fwd = d == 0
    is_ctx = u < n_ctx_units
    causal, _, _ = _dir_masks(fwd)
    qo, ko, vo = 0, W_MIX, 2 * W_MIX
    ng = GROUP_CHUNKS * N_HEADS

    def pre(it, carry):
        r0 = pl.multiple_of(it * GROUP_ROWS, GROUP_ROWS)
        gs = pl.ds(pl.multiple_of(it * ng, ng), ng)
        sm = sm_ref[pl.ds(r0, GROUP_ROWS), :]
        f_all = -_softplus(-(sm + par_ref[3:4, :]))
        fcum = _dir_select(fwd, jnp.where(fwd, _chunk_cumsum(f_all, False), _chunk_cumsum(f_all, True)))
        i_al = pltpu.roll(_dir_select(fwd, sm + par_ref[2:3, :]), SM_F - SM_I, 1)
        fi, hrow = _group_columns(fcum, (fcum - i_al).T, SM_F)
        it_b, _ = _group_columns(i_al, None, SM_F)
        q = _group_heads(big_ref, r0, qo)
        v = _group_heads(big_ref, r0, vo)
        ks = _group_heads(big_ref, r0, ko).astype(F32) * HEAD_DIM ** -0.5
        dmat = jnp.where(causal, fi[:, :, :CHUNK] - hrow, -jnp.inf)
        f_end = jnp.where(fwd, fi[:, CHUNK - 1:CHUNK, :], fi[:, 0:1, :])
        d_end = f_end - fi + it_b
        m_intra = jnp.max(dmat, axis=-1, keepdims=True)
        m_end = jnp.max(d_end, axis=1, keepdims=True)
        p_loc = jnp.exp(dmat - m_intra) * _bmm_nt(q, ks)
        kw = ks * jnp.exp(d_end - m_end)
        fi_scr[gs] = fi
        mi_scr[gs] = jnp.broadcast_to(m_intra, (ng, CHUNK, HEAD_DIM))
        pv_scr[gs] = _bmm(p_loc, v)
        ps_scr[gs] = jnp.broadcast_to(jnp.sum(p_loc, axis=-1, keepdims=True), (ng, CHUNK, HEAD_DIM))
        kv_scr[gs] = _bmm_tn(kw, v)
        ks_scr[gs] = jnp.broadcast_to(jnp.sum(kw, axis=1, keepdims=True), (ng, 8, HEAD_DIM))
        me_scr[gs] = jnp.broadcast_to(m_end, (ng, 8, HEAD_DIM))
        return carry

    lax.fori_loop(0, NCH // GROUP_CHUNKS, pre, 0)

    first_lat = jnp.logical_and(u >= n_ctx_units, (u - n_ctx_units) % lat_units == 0)

    @pl.when(first_lat)
    def _():
        c_scr[...] = c0_ref[0, 0, 0]
        for h in range(N_HEADS):
            n_scr[h] = jnp.broadcast_to(n0_ref[0, 0, 0, h:h + 1, :], (8, HEAD_DIM))
            m_scr[h] = jnp.broadcast_to(m0_ref[0, 0, 0, h:h + 1, :], (8, HEAD_DIM))

    def step(n, carry):
        c = jnp.where(fwd, n, NCH - 1 - n)
        r0 = pl.multiple_of(c * CHUNK, CHUNK)

        @pl.when(jnp.logical_and(is_ctx, n % seq_chunks == 0))
        def _():
            c_scr[...] = jnp.zeros(c_scr.shape, F32)
            n_scr[...] = jnp.zeros(n_scr.shape, F32)
            m_scr[...] = jnp.zeros(m_scr.shape, F32)

        hs = pl.ds(pl.multiple_of(c * N_HEADS, N_HEADS), N_HEADS)
        q = jnp.stack([big_ref[pl.ds(r0, CHUNK), qo + h * HEAD_DIM:qo + (h + 1) * HEAD_DIM]
                       for h in range(N_HEADS)])
        cs = c_scr[...]
        ns = n_scr[...][:, 0:1, :]
        ms = m_scr[...][:, 0:1, :]
        fi = fi_scr[hs]
        mi = mi_scr[hs]
        inter = fi + ms
        mt = jnp.maximum(inter, mi)
        w_int = jnp.exp(inter - mt)
        w_loc = jnp.exp(mi - mt)
        num = w_int * _bmm(q, cs) + w_loc * pv_scr[hs]
        qn = jnp.sum(q.astype(F32) * ns, axis=-1, keepdims=True)
        den = w_int * qn + w_loc * ps_scr[hs]
        hh = num / jnp.maximum(jnp.abs(den), jnp.exp(-mt))
        f_end = jnp.where(fwd, fi[:, CHUNK - 1:CHUNK, :], fi[:, 0:1, :])
        inter_end = f_end + ms
        m_end = me_scr[hs][:, 0:1, :]
        m_new = jnp.maximum(inter_end, m_end)
        s_int = jnp.exp(inter_end - m_new)
        s_loc = jnp.exp(m_end - m_new)
        c_scr[...] = cs * s_int[:, :, 0:1] + kv_scr[hs] * s_loc[:, :, 0:1]
        n_scr[...] = jnp.broadcast_to(ns * s_int + ks_scr[hs][:, 0:1, :] * s_loc, n_scr.shape)
        m_scr[...] = jnp.broadcast_to(m_new, m_scr.shape)
        for h in range(N_HEADS):
            o_ref[0, pl.ds(r0, CHUNK), h * HEAD_DIM:(h + 1) * HEAD_DIM] = hh[h]

        @pl.when(jnp.logical_and(is_ctx, n % seq_chunks == seq_chunks - 1))
        def _():
            sq = c // seq_chunks
            cf_ref[sq, 0] = c_scr[...]
            nf_ref[sq, 0] = n_scr[...]
            mf_ref[sq, 0] = m_scr[...]

        return carry

    lax.fori_loop(0, NCH, step, 0)


def _mlstm(proj_big, proj_f32, par, st_c, st_n8, st_m8, l, rows: Rows):
    n_ctx_units, lat_units, n_units = _unit_layout(rows)
    seq_per_unit = UNIT // rows.ctx_len
    nt = rows.total
    sm_col = W_B // LANES
    qkv_blk = OFF_QC // (3 * W_MIX)
    assert OFF_QC % (3 * W_MIX) == 0
    kern = functools.partial(_mlstm_kernel, rows=rows)
    hshape = (NCH * N_HEADS, CHUNK, HEAD_DIM)
    n_out = rows.n_ctx
    return pl.pallas_call(
        kern,
        grid=(2, n_units),
        in_specs=[
            pl.BlockSpec((UNIT, 3 * W_MIX), lambda d, u: (_unit_rowblock(u, d, rows), qkv_blk)),
            pl.BlockSpec((UNIT, LANES), lambda d, u: (_unit_rowblock(u, d, rows), sm_col)),
            pl.BlockSpec((8, LANES), lambda d, u: (0, 0)),
            pl.BlockSpec((1, 1, 1, N_HEADS, HEAD_DIM, HEAD_DIM),
                         lambda d, u: (_unit_lat_seq(u, rows), l, d, 0, 0, 0)),
            pl.BlockSpec((1, 1, 1, N_HEADS, HEAD_DIM), lambda d, u: (_unit_lat_seq(u, rows), l, d, 0, 0)),
            pl.BlockSpec((1, 1, 1, N_HEADS, HEAD_DIM), lambda d, u: (_unit_lat_seq(u, rows), l, d, 0, 0)),
        ],
        out_specs=[
            pl.BlockSpec((1, UNIT, W_MIX), lambda d, u: (d, _unit_rowblock(u, d, rows), 0)),
            pl.BlockSpec((seq_per_unit, 1, N_HEADS, HEAD_DIM, HEAD_DIM),
                         lambda d, u: (jnp.minimum(u, n_ctx_units - 1), d, 0, 0, 0)),
            pl.BlockSpec((seq_per_unit, 1, N_HEADS, 8, HEAD_DIM),
                         lambda d, u: (jnp.minimum(u, n_ctx_units - 1), d, 0, 0, 0)),
            pl.BlockSpec((seq_per_unit, 1, N_HEADS, 8, HEAD_DIM),
                         lambda d, u: (jnp.minimum(u, n_ctx_units - 1), d, 0, 0, 0)),
        ],
        out_shape=[
            jax.ShapeDtypeStruct((2, nt, W_MIX), F32),
            jax.ShapeDtypeStruct((n_out, 2, N_HEADS, HEAD_DIM, HEAD_DIM), F32),
            jax.ShapeDtypeStruct((n_out, 2, N_HEADS, 8, HEAD_DIM), F32),
            jax.ShapeDtypeStruct((n_out, 2, N_HEADS, 8, HEAD_DIM), F32),
        ],
        scratch_shapes=[
            pltpu.VMEM((N_HEADS, HEAD_DIM, HEAD_DIM), F32),
            pltpu.VMEM((N_HEADS, 8, HEAD_DIM), F32),
            pltpu.VMEM((N_HEADS, 8, HEAD_DIM), F32),
            pltpu.VMEM(hshape, F32),
            pltpu.VMEM(hshape, F32),
            pltpu.VMEM(hshape, F32),
            pltpu.VMEM(hshape, F32),
            pltpu.VMEM((NCH * N_HEADS, HEAD_DIM, HEAD_DIM), F32),
            pltpu.VMEM((NCH * N_HEADS, 8, HEAD_DIM), F32),
            pltpu.VMEM((NCH * N_HEADS, 8, HEAD_DIM), F32),
        ],
        compiler_params=_cparams(("arbitrary", "arbitrary")),
        name="mlstm",
    )(proj_big, proj_f32, par, st_c, st_n8, st_m8)


def _gelu_tanh(x):
    return 0.5 * x * (1.0 + jnp.tanh(math.sqrt(2.0 / math.pi) * (x + 0.044715 * (x * x * x))))


def _tile_scan(a, b, reverse, span=None):
    rows_n = a.shape[0]
    n = rows_n if span is None else span
    row = lax.broadcasted_iota(jnp.int32, a.shape, 0) & (n - 1)
    s = 1
    while s < n:
        if reverse:
            ok = row < n - s
            a_sh = jnp.where(ok, pltpu.roll(a, rows_n - s, 0), 1.0)
            b_sh = jnp.where(ok, pltpu.roll(b, rows_n - s, 0), 0.0)
        else:
            ok = row >= s
            a_sh = jnp.where(ok, pltpu.roll(a, s, 0), 1.0)
            b_sh = jnp.where(ok, pltpu.roll(b, s, 0), 0.0)
        b = a * b_sh + b
        a = a * a_sh
        s *= 2
    return a, b


def _lru_kernel(x_ref, g_ref, cw_ref, cb_ref, wa_ref, ba_ref, wx_ref, bx_ref, lam_ref, h0_ref,
                y_ref, hf_ref, xc_scr, hfw_scr, *, rows: Rows, br):
    blk = pl.program_id(0)
    is_ctx = blk * br < rows.ctx_rows
    ts = rows.ctx_len
    n_tiles = br // ts
    seq_len = jnp.where(is_ctx, rows.ctx_len, rows.lat_len)
    xc_scr[...] = _conv_taps(x_ref[...], cw_ref[0], _conv_masks(x_ref.shape, seq_len)) + cb_ref[0]

    for dd in range(2):
        reverse = dd == 1
        sp_lam = _softplus(-lam_ref[0, dd])

        def tile(n, carry, dd=dd, reverse=reverse, sp_lam=sp_lam):
            t = (n_tiles - 1 - n) if reverse else n
            r0 = pl.multiple_of(t * ts, ts)
            xc = xc_scr[pl.ds(r0, ts), :]
            r = _sigmoid(_bdot(xc, wa_ref[0, dd, 0]) + ba_ref[0, dd])
            gi = _sigmoid(_bdot(xc, wx_ref[0, dd, 0]) + bx_ref[0, dd])
            log_a = -LRU_C * r * sp_lam
            a = jnp.exp(log_a)
            b = jnp.sqrt(-jnp.tanh(log_a) * (a * a + 1.0)) * (gi * xc)
            a_cum, h = _tile_scan(a, b, reverse)
            carry = jnp.where(is_ctx, 0.0, carry)
            h = h + a_cum * carry
            last = h[0:1, :] if reverse else h[ts - 1:ts, :]
            hf_ref[0, dd, pl.ds(t, 1), :] = last
            if reverse:
                g = g_ref[pl.ds(r0, ts), :].astype(F32)
                y_ref[pl.ds(r0, ts), :] = ((hfw_scr[pl.ds(r0, ts), :] + h) * _gelu_tanh(g)).astype(BF16)
            else:
                hfw_scr[pl.ds(r0, ts), :] = h
            return last

        lax.fori_loop(0, n_tiles, tile, h0_ref[0, 0, dd])


def _lru(proj_f32, proj_big, conv_w, conv_b, wa, ba, wx, bx, lam, h0, l, rows: Rows):
    br = rows.lat_len
    nt = rows.total
    n_blocks = nt // br
    n_ctx_blocks = rows.ctx_rows // br
    n_tiles = br // rows.ctx_len
    gcol = OFF_GB // BW_B

    def lat_seq(b):
        return jnp.maximum(b - n_ctx_blocks, 0)

    vec = lambda a: a.reshape(DEPTH, 2, 1, W_B)
    return pl.pallas_call(
        functools.partial(_lru_kernel, rows=rows, br=br),
        grid=(n_blocks, NB_B),
        in_specs=[
            pl.BlockSpec((br, BW_B), lambda b, j: (b, j)),
            pl.BlockSpec((br, BW_B), lambda b, j: (b, gcol + j)),
            pl.BlockSpec((1, CONV_W, BW_B), lambda b, j: (l, 0, j)),
            pl.BlockSpec((1, 1, BW_B), lambda b, j: (l, 0, j)),
            pl.BlockSpec((1, 2, 1, BW_B, BW_B), lambda b, j: (l, 0, j, 0, 0)),
            pl.BlockSpec((1, 2, 1, BW_B), lambda b, j: (l, 0, 0, j)),
            pl.BlockSpec((1, 2, 1, BW_B, BW_B), lambda b, j: (l, 0, j, 0, 0)),
            pl.BlockSpec((1, 2, 1, BW_B), lambda b, j: (l, 0, 0, j)),
            pl.BlockSpec((1, 2, 1, BW_B), lambda b, j: (l, 0, 0, j)),
            pl.BlockSpec((1, 1, 2, 1, BW_B), lambda b, j: (lat_seq(b), l, 0, 0, j)),
        ],
        out_specs=[
            pl.BlockSpec((br, BW_B), lambda b, j: (b, j)),
            pl.BlockSpec((1, 2, n_tiles, BW_B), lambda b, j: (b, 0, 0, j)),
        ],
        out_shape=[
            jax.ShapeDtypeStruct((nt, W_B), BF16),
            jax.ShapeDtypeStruct((n_blocks, 2, n_tiles, W_B), F32),
        ],
        scratch_shapes=[pltpu.VMEM((br, BW_B), F32), pltpu.VMEM((br, BW_B), F32)],
        compiler_params=_cparams(("arbitrary", "arbitrary")),
        name="lru",
    )(proj_f32, proj_big, conv_w, conv_b.reshape(DEPTH, 1, W_B), wa, vec(ba), wx, vec(bx), vec(lam),
      h0.reshape(h0.shape[0], DEPTH, 2, 1, W_B))


def _route(lg, le):
    lane = lax.broadcasted_iota(jnp.int32, lg.shape, 1)
    neg = -jnp.inf
    lgm = jnp.where(lane < N_GROUPS, lg, neg)
    gmax = jnp.max(lgm, axis=-1, keepdims=True)
    p_grp = 1.0 / jnp.sum(jnp.exp(lgm - gmax), axis=-1, keepdims=True)
    g_sel = jnp.min(jnp.where(lgm == gmax, lane, LANES), axis=-1, keepdims=True)
    in_grp = jnp.logical_and(lane >= g_sel * E_PER_GROUP, lane < (g_sel + 1) * E_PER_GROUP)
    lem = jnp.where(in_grp, le, neg)
    v1 = jnp.max(lem, axis=-1, keepdims=True)
    i1 = jnp.min(jnp.where(lem == v1, lane, LANES), axis=-1, keepdims=True)
    lem2 = jnp.where(lane == i1, neg, lem)
    v2 = jnp.max(lem2, axis=-1, keepdims=True)
    i2 = jnp.min(jnp.where(lem2 == v2, lane, LANES), axis=-1, keepdims=True)
    e2 = jnp.exp(v2 - v1)
    w1 = p_grp / (1.0 + e2)
    w2 = p_grp * e2 / (1.0 + e2)
    gate = jnp.where(lane == i1, w1, 0.0) + jnp.where(lane == i2, w2, 0.0)
    local = gate
    for g in range(1, N_GROUPS):
        local = jnp.where(g_sel == g, pltpu.roll(gate, LANES - g * E_PER_GROUP, 1), local)
    local = jnp.where(lane < E_PER_GROUP, local, 0.0)
    hi = local.astype(BF16).astype(F32)
    rest = local - hi
    mid = rest.astype(BF16).astype(F32)
    lo = (rest - mid).astype(BF16).astype(F32)
    packed = hi + pltpu.roll(mid, E_PER_GROUP, 1) + pltpu.roll(lo, 2 * E_PER_GROUP, 1)
    return packed, g_sel


def _group_rank(g_sel, cnt_scr):
    tm = g_sel.shape[0]
    lane = lax.broadcasted_iota(jnp.int32, (tm, LANES), 1)
    onehot = jnp.where(lane == g_sel, 1.0, 0.0)
    ii = lax.broadcasted_iota(jnp.int32, (tm, tm), 0)
    jj = lax.broadcasted_iota(jnp.int32, (tm, tm), 1)
    before = jnp.where(ii > jj, 1.0, 0.0).astype(BF16)
    seen = jnp.dot(before, onehot.astype(BF16), preferred_element_type=F32) + cnt_scr[0:1, :]
    cnt_scr[...] = cnt_scr[...] + jnp.sum(onehot, axis=0, keepdims=True)
    return jnp.sum(seen * onehot, axis=-1, keepdims=True)


def _merge_kernel(*refs, n_x, n_ctx_tiles):
    x_refs = refs[:n_x]
    (mod_ref, oa_ref, z_ref, yb_ref, hc_ref, oc_ref, gt_ref, dn_ref, mn_ref,
     wpa_ref, wpb_ref, wpc_ref, wout_ref, g1_ref, b1_ref, wrh_ref, wrl_ref, br_ref,
     x1_ref, h2_ref, gate_ref, meta_ref, cnt_scr) = refs[n_x:]
    oa = oa_ref[0] + oa_ref[1]
    hc = hc_ref[0] + hc_ref[1]
    ya, yc = [], []
    for h in range(N_HEADS):
        sl = slice(h * HEAD_DIM, (h + 1) * HEAD_DIM)
        o_h = oa[:, sl]
        o_h = o_h * lax.rsqrt(jnp.mean(o_h * o_h, axis=-1, keepdims=True) + RMS_EPS) * dn_ref[...]
        ya.append((o_h * _silu(z_ref[:, sl].astype(F32))).astype(BF16))
        c_h = _ln(hc[:, sl]) * mn_ref[:, sl]
        yc.append((_sigmoid(oc_ref[:, sl].astype(F32)) * c_h).astype(BF16))
    ya = jnp.concatenate(ya, axis=-1)
    yc = jnp.concatenate(yc, axis=-1)
    ga = _sigmoid(gt_ref[:, 0:D_MODEL].astype(F32))
    gb = _sigmoid(gt_ref[:, D_MODEL:2 * D_MODEL].astype(F32))
    gc = _sigmoid(gt_ref[:, 2 * D_MODEL:3 * D_MODEL].astype(F32))
    d = functools.partial(jnp.dot, preferred_element_type=F32)
    merged = ga * d(ya, wpa_ref[...]) + gb * d(yb_ref[...], wpb_ref[...]) + gc * d(yc, wpc_ref[...])
    mixed = d(merged.astype(BF16), wout_ref[...])
    gate1 = mod_ref[0, 2:3, :]
    shift2 = mod_ref[0, 3:4, :]
    scale2 = mod_ref[0, 4:5, :]
    x1 = _ln(DN_ALPHA * _read_x(x_refs, n_ctx_tiles) + gate1 * mixed) * g1_ref[...] + b1_ref[...]
    x1_ref[...] = x1
    h2 = _ln(x1) * (1.0 + scale2) + shift2
    h2_ref[...] = h2.astype(BF16)
    hh, hl = _split2(h2)
    d = functools.partial(jnp.dot, preferred_element_type=F32)
    logits = d(hh, wrh_ref[...]) + (d(hh, wrl_ref[...]) + d(hl, wrh_ref[...])) + br_ref[...]
    packed, g_sel = _route(logits, pltpu.roll(logits, LANES - 64, 1))

    @pl.when(pl.program_id(0) == 0)
    def _():
        cnt_scr[...] = jnp.zeros(cnt_scr.shape, F32)

    rank = _group_rank(g_sel, cnt_scr)
    lane = lax.broadcasted_iota(jnp.int32, packed.shape, 1)
    route = jnp.where(lane == ROUTE_GROUP_LANE, g_sel.astype(F32), jnp.where(lane == ROUTE_RANK_LANE, rank, packed))
    gate_ref[...] = route
    meta_ref[0] = route.T[ROUTE_GROUP_LANE:ROUTE_GROUP_LANE + SUBLANES, :]


def _merge(x_parts, mod_l, o_a, proj_big, y_b, h_c, dn, mn, wpa, wpb, wpc, wout, g1, b1, wr, br, rows: Rows, tm):
    nt = rows.total
    row = lambda i: (i, 0)
    const = lambda i: (0, 0)
    cw = W_MIX
    return pl.pallas_call(
        functools.partial(_merge_kernel, n_x=len(x_parts), n_ctx_tiles=rows.ctx_rows // tm),
        grid=(nt // tm,),
        in_specs=[
            *_x_specs(x_parts, rows, tm),
            pl.BlockSpec((1, 6, D_MODEL), lambda i: (_cond_index(i * tm, rows), 0, 0)),
            pl.BlockSpec((2, tm, cw), lambda i: (0, i, 0)),
            pl.BlockSpec((tm, cw), lambda i: (i, OFF_Z // cw)),
            pl.BlockSpec((tm, cw), row),
            pl.BlockSpec((2, tm, cw), lambda i: (0, i, 0)),
            pl.BlockSpec((tm, cw), lambda i: (i, OFF_OC // cw)),
            pl.BlockSpec((tm, 3 * D_MODEL), lambda i: (i, OFF_GATES // (3 * D_MODEL))),
            pl.BlockSpec((1, HEAD_DIM), const),
            pl.BlockSpec((1, cw), const),
            pl.BlockSpec((cw, D_MODEL), const),
            pl.BlockSpec((cw, D_MODEL), const),
            pl.BlockSpec((cw, D_MODEL), const),
            pl.BlockSpec((D_MODEL, D_MODEL), const),
            pl.BlockSpec((1, D_MODEL), const),
            pl.BlockSpec((1, D_MODEL), const),
            pl.BlockSpec((D_MODEL, LANES), const),
            pl.BlockSpec((D_MODEL, LANES), const),
            pl.BlockSpec((1, LANES), const),
        ],
        out_specs=[
            pl.BlockSpec((tm, D_MODEL), row),
            pl.BlockSpec((tm, D_MODEL), row),
            pl.BlockSpec((tm, LANES), row),
            pl.BlockSpec((1, SUBLANES, tm), lambda i: (i, 0, 0)),
        ],
        out_shape=[
            jax.ShapeDtypeStruct((nt, D_MODEL), F32),
            jax.ShapeDtypeStruct((nt, D_MODEL), BF16),
            jax.ShapeDtypeStruct((nt, LANES), F32),
            jax.ShapeDtypeStruct((nt // tm, SUBLANES, tm), F32),
        ],
        scratch_shapes=[pltpu.VMEM((SUBLANES, LANES), F32)],
        compiler_params=_cparams(("arbitrary",)),
        name="merge",
    )(*x_parts, mod_l, o_a, proj_big, y_b, h_c, proj_big, proj_big, dn, mn, wpa, wpb, wpc, wout, g1, b1, *_split2(wr), br)


MOE_TILE = 512


class MoePlan(NamedTuple):
    dest_row: jax.Array
    dest_col: jax.Array
    tile_group: jax.Array
    by_tile: tuple
    by_block: tuple


def _pair_list(mask, n_pairs, minor):
    flat = mask.reshape(-1)
    cnt = jnp.sum(flat.astype(jnp.int32))
    idx = jnp.nonzero(flat, size=n_pairs, fill_value=0)[0].astype(jnp.int32)
    pos = jnp.arange(n_pairs, dtype=jnp.int32)
    valid = pos < cnt
    idx = jnp.where(valid, idx, idx[jnp.maximum(cnt - 1, 0)])
    major, mnr = idx // minor, idx % minor
    prev = jnp.concatenate([jnp.full((1,), -1, jnp.int32), major[:-1]])
    nxt = jnp.concatenate([major[1:], jnp.full((1,), -1, jnp.int32)])
    first = jnp.logical_and(valid, major != prev)
    last = jnp.logical_and(valid, jnp.logical_or(major != nxt, pos == cnt - 1))
    i32 = lambda a: a.astype(jnp.int32)
    return major, mnr, i32(first), i32(last), i32(valid)


def _moe_plan(g_sel, rank, nt):
    n_blocks = nt // MOE_TILE
    n_tiles = n_blocks + N_GROUPS
    n_pairs = n_tiles + N_GROUPS * n_blocks
    oh = (g_sel[:, None] == jnp.arange(N_GROUPS, dtype=jnp.int32)[None, :]).astype(jnp.int32)
    counts = jnp.sum(oh, axis=0)
    tiles_g = (counts + MOE_TILE - 1) // MOE_TILE
    tile_end = jnp.cumsum(tiles_g)
    tile_start = tile_end - tiles_g
    dest = jnp.sum(oh * tile_start[None, :], axis=1) * MOE_TILE + rank
    tile_ids = jnp.arange(n_tiles, dtype=jnp.int32)
    tile_group = jnp.minimum(jnp.sum((tile_ids[:, None] >= tile_end[None, :]).astype(jnp.int32), axis=1),
                             N_GROUPS - 1)
    t_oh = ((dest // MOE_TILE)[:, None] == tile_ids[None, :]).astype(F32)
    b_oh = ((jnp.arange(nt, dtype=jnp.int32) // MOE_TILE)[:, None]
            == jnp.arange(n_blocks, dtype=jnp.int32)[None, :]).astype(F32)
    mask = jnp.einsum('tj,tb->jb', t_oh, b_oh) > 0.5
    mask = mask.at[:, 0].set(jnp.logical_or(mask[:, 0], tile_ids >= tile_end[-1]))
    tj, tb, tf, tl, tv = _pair_list(mask, n_pairs, n_blocks)
    cb, cj, cf, cl, cv = _pair_list(mask.T, n_pairs, n_tiles)
    return MoePlan(dest.reshape(n_blocks, 1, MOE_TILE),
                   jnp.broadcast_to(dest[:, None], (nt, LANES)),
                   tile_group, (tj, tb, tf, tl, tv), (cj, cb, cf, cl, cv))


def _moe_experts_kernel(pj_ref, pb_ref, pf_ref, pl_ref, pv_ref, tg_ref,
                        h_ref, dest_ref, route_ref, w1_ref, w3_ref, w2_ref, y_ref, x_scr, g_scr):
    p = pl.program_id(0)

    def gathered():
        row = lax.broadcasted_iota(jnp.int32, (MOE_TILE, MOE_TILE), 0) + pj_ref[p] * MOE_TILE
        sel = jnp.where(dest_ref[0] == row, 1.0, 0.0).astype(BF16)
        d = functools.partial(jnp.dot, preferred_element_type=F32)
        return d(sel, h_ref[...]), d(sel, route_ref[...].astype(BF16))

    @pl.when(pf_ref[p] == 1)
    def _():
        x_scr[...], g_scr[...] = gathered()

    @pl.when(jnp.logical_and(pv_ref[p] == 1, pf_ref[p] == 0))
    def _():
        xg, gg = gathered()
        x_scr[...] += xg
        g_scr[...] += gg

    @pl.when(pl_ref[p] == 1)
    def _():
        x = x_scr[...].astype(BF16)
        g = g_scr[...]
        gate = g + (pltpu.roll(g, LANES - E_PER_GROUP, 1) + pltpu.roll(g, LANES - 2 * E_PER_GROUP, 1))
        d = functools.partial(jnp.dot, preferred_element_type=F32)
        acc = jnp.zeros((MOE_TILE, D_MODEL), F32)
        for e in range(E_PER_GROUP):
            hid = _silu(d(x, w1_ref[0, e])) * d(x, w3_ref[0, e]) * gate[:, e:e + 1]
            acc = acc + d(hid.astype(BF16), w2_ref[0, e])
        y_ref[...] = acc.astype(BF16)


def _moe_experts(h2, route, plan: MoePlan, w1, w3, w2, l, nt):
    n_blocks = nt // MOE_TILE
    n_tiles = n_blocks + N_GROUPS
    tj, tb, tf, tl, tv = plan.by_tile
    n_pairs = tj.shape[0]
    wmap = lambda p, pj, pb, pf, pl_, pv, tg: (l * N_GROUPS + tg[pj[p]], 0, 0, 0)
    grid_spec = pltpu.PrefetchScalarGridSpec(
        num_scalar_prefetch=6,
        grid=(n_pairs,),
        in_specs=[
            pl.BlockSpec((MOE_TILE, D_MODEL), lambda p, pj, pb, *_: (pb[p], 0)),
            pl.BlockSpec((1, 1, MOE_TILE), lambda p, pj, pb, *_: (pb[p], 0, 0)),
            pl.BlockSpec((MOE_TILE, LANES), lambda p, pj, pb, *_: (pb[p], 0)),
            pl.BlockSpec((1, E_PER_GROUP, D_MODEL, D_EXPERT), wmap),
            pl.BlockSpec((1, E_PER_GROUP, D_MODEL, D_EXPERT), wmap),
            pl.BlockSpec((1, E_PER_GROUP, D_EXPERT, D_MODEL), wmap),
        ],
        out_specs=pl.BlockSpec((MOE_TILE, D_MODEL), lambda p, pj, *_: (pj[p], 0)),
        scratch_shapes=[pltpu.VMEM((MOE_TILE, D_MODEL), F32), pltpu.VMEM((MOE_TILE, LANES), F32)],
    )
    return pl.pallas_call(
        _moe_experts_kernel,
        grid_spec=grid_spec,
        out_shape=jax.ShapeDtypeStruct((n_tiles * MOE_TILE, D_MODEL), BF16),
        compiler_params=_cparams(("arbitrary",)),
        name="moe_experts",
    )(tj, tb, tf, tl, tv, plan.tile_group, h2, plan.dest_row, route, w1, w3, w2)


def _moe_combine_kernel(cj_ref, cb_ref, cf_ref, cl_ref, cv_ref,
                        y_ref, dest_ref, x1_ref, mod_ref, g2_ref, b2_ref, o_ref, acc_scr):
    p = pl.program_id(0)

    def scattered():
        col = lax.broadcasted_iota(jnp.int32, (MOE_TILE, MOE_TILE), 1) + cj_ref[p] * MOE_TILE
        sel = jnp.where(dest_ref[:, 0:1] == col, 1.0, 0.0).astype(BF16)
        return jnp.dot(sel, y_ref[...], preferred_element_type=F32)

    @pl.when(cf_ref[p] == 1)
    def _():
        acc_scr[...] = scattered()

    @pl.when(jnp.logical_and(cv_ref[p] == 1, cf_ref[p] == 0))
    def _():
        acc_scr[...] += scattered()

    @pl.when(cl_ref[p] == 1)
    def _():
        gate2 = mod_ref[0, 5:6, :]
        o_ref[...] = _ln(DN_ALPHA * x1_ref[...] + gate2 * acc_scr[...]) * g2_ref[...] + b2_ref[...]


def _moe_combine(y, plan: MoePlan, x1, mod_l, g2, b2, rows: Rows):
    nt = rows.total
    cj, cb, cf, cl, cv = plan.by_block
    n_pairs = cj.shape[0]
    blk = lambda p, cj, cb, *_: (cb[p], 0)
    const = lambda p, *_: (0, 0)
    grid_spec = pltpu.PrefetchScalarGridSpec(
        num_scalar_prefetch=5,
        grid=(n_pairs,),
        in_specs=[
            pl.BlockSpec((MOE_TILE, D_MODEL), lambda p, cj, *_: (cj[p], 0)),
            pl.BlockSpec((MOE_TILE, LANES), blk),
            pl.BlockSpec((MOE_TILE, D_MODEL), blk),
            pl.BlockSpec((1, 6, D_MODEL), lambda p, cj, cb, *_: (_cond_index(cb[p] * MOE_TILE, rows), 0, 0)),
            pl.BlockSpec((1, D_MODEL), const),
            pl.BlockSpec((1, D_MODEL), const),
        ],
        out_specs=pl.BlockSpec((MOE_TILE, D_MODEL), blk),
        scratch_shapes=[pltpu.VMEM((MOE_TILE, D_MODEL), F32)],
    )
    return pl.pallas_call(
        _moe_combine_kernel,
        grid_spec=grid_spec,
        out_shape=jax.ShapeDtypeStruct((nt, D_MODEL), F32),
        compiler_params=_cparams(("arbitrary",)),
        name="moe_combine",
    )(cj, cb, cf, cl, cv, y, plan.dest_col, x1, mod_l, g2, b2)


class Tiles(NamedTuple):
    token_rows: int
    proj_cols: int
    merge_rows: int


def _tile_config(rows: Rows) -> Tiles:
    token_rows = min(1024, rows.lat_len)
    proj_cols = 1280
    assert N_BIG % proj_cols == 0 and proj_cols % (2 * LANES) == 0
    return Tiles(token_rows, proj_cols, min(256, rows.lat_len))


def _grid_pos_embed(n_tokens):
    rows = n_tokens // GRID_W
    quarter = D_MODEL // 4
    freqs = jnp.exp(-math.log(POS_BASE) * jnp.arange(quarter, dtype=F32) / quarter)
    ar = jnp.arange(rows, dtype=F32).reshape(-1, 1) * freqs
    ac = jnp.arange(GRID_W, dtype=F32).reshape(-1, 1) * freqs
    row_half = jnp.concatenate([jnp.sin(ar), jnp.cos(ar)], axis=-1)
    col_half = jnp.concatenate([jnp.sin(ac), jnp.cos(ac)], axis=-1)
    return jnp.concatenate([jnp.repeat(row_half, GRID_W, axis=0), jnp.tile(col_half, (rows, 1))], axis=-1)


def _pack_w_in(w_in_l):
    sizes = (3 * W_MIX, W_B, W_MIX, 8, 8, W_B, W_MIX, W_MIX, W_MIX, W_MIX, 8, 8, 3 * D_MODEL)
    parts, start = [], 0
    for s in sizes:
        parts.append(w_in_l[:, start:start + s])
        start += s
    qkv_a, x_b, z_a, beta, alpha, g_b, q_c, k_c, v_c, o_c, i_c, f_c, gates = parts
    big = jnp.concatenate([gates, qkv_a, q_c, k_c, v_c, z_a, g_b, o_c], axis=1).astype(BF16)
    pad = jnp.zeros((D_MODEL, LANES - 32), F32)
    small = jnp.concatenate([x_b, beta, alpha, i_c, f_c, pad], axis=1).astype(BF16)
    return big, small


def _lane_row(vals, off):
    rows_ = [jnp.concatenate([jnp.zeros((o,), F32), v.reshape(-1).astype(F32), jnp.zeros((LANES - o - 8,), F32)])
             for v, o in zip(vals, off)]
    return jnp.stack(rows_ + [jnp.zeros((LANES,), F32)] * (8 - len(rows_)))


def kernel(x_prompt, x_sample, state_delta, state_lru, state_mlstm_C, state_mlstm_n, state_mlstm_m, c, c_ctx,
           w_mod, b_mod, w_in, conv_a, delta_a_log, delta_dt_bias, delta_norm, conv_b_w, conv_b_b,
           lru_wa, lru_ba, lru_wx, lru_bx, lru_lambda, mlstm_bi, mlstm_bf, mlstm_norm,
           w_pa, w_pb, w_pc, w_out, ln1_g, ln1_b, ln2_g, ln2_b, w_rg, b_rg, w_re, b_re, w_e1, w_e3, w_e2):
    n_ctx, ctx_len, _ = x_prompt.shape
    n_lat, lat_len, _ = x_sample.shape
    rows = Rows(n_ctx, ctx_len, n_lat, lat_len)
    assert rows.ctx_rows % UNIT == 0 and lat_len % UNIT == 0 and UNIT % ctx_len == 0
    assert rows.ctx_rows % lat_len == 0 and ctx_len % CHUNK == 0 and n_lat <= 7
    assert rows.ctx_rows % MOE_TILE == 0 and lat_len % MOE_TILE == 0

    tiles = _tile_config(rows)
    pos = _grid_pos_embed(lat_len)
    x_parts = (x_prompt.reshape(rows.ctx_rows, D_MODEL), x_sample.reshape(n_lat * lat_len, D_MODEL), pos)

    cond8 = jnp.concatenate([c_ctx[None, :], c, jnp.zeros((8 - 1 - n_lat, D_MODEL), F32)], axis=0)
    mod = _modulation(cond8, w_mod, b_mod).reshape(DEPTH, 8, 6, D_MODEL)

    m_bcast = jnp.broadcast_to(state_mlstm_m[..., None], state_mlstm_m.shape + (HEAD_DIM,))
    w_e1g = w_e1.astype(BF16).reshape(DEPTH * N_GROUPS, E_PER_GROUP, D_MODEL, D_EXPERT)
    w_e3g = w_e3.astype(BF16).reshape(DEPTH * N_GROUPS, E_PER_GROUP, D_MODEL, D_EXPERT)
    w_e2g = w_e2.astype(BF16).reshape(DEPTH * N_GROUPS, E_PER_GROUP, D_EXPERT, D_MODEL)

    finals = []
    for l in range(DEPTH):
        w_big, w_small = _pack_w_in(w_in[l])
        par = _lane_row([delta_a_log[l], delta_dt_bias[l], mlstm_bi[l], mlstm_bf[l]],
                        [SM_ALPHA, SM_ALPHA, SM_I, SM_F])
        proj_big, proj_f32 = _projection(x_parts, mod[l], w_big, w_small, rows, tiles.token_rows, tiles.proj_cols)
        qkv_c = _conv_a(proj_big, conv_a, l, rows, lat_len)
        o_a, sf_a = _delta(qkv_c, proj_f32, par, state_delta, l, rows)
        y_b, hf_b = _lru(proj_f32, proj_big, conv_b_w, conv_b_b, lru_wa, lru_ba, lru_wx, lru_bx, lru_lambda,
                         state_lru, l, rows)
        h_c, cf, nf, mf = _mlstm(proj_big, proj_f32, par, state_mlstm_C, state_mlstm_n, m_bcast, l, rows)
        lane_pack = lambda g, e: jnp.concatenate(
            [g, jnp.zeros(g.shape[:-1] + (64 - N_GROUPS,), F32), e,
             jnp.zeros(g.shape[:-1] + (LANES - 64 - N_EXPERTS,), F32)], axis=-1)
        wr = lane_pack(w_rg[l], w_re[l])
        br = lane_pack(b_rg[l], b_re[l]).reshape(1, LANES)
        x1, h2, gate, meta = _merge(x_parts, mod[l], o_a, proj_big, y_b, h_c,
                                    delta_norm[l].reshape(1, HEAD_DIM), mlstm_norm[l].reshape(1, W_MIX),
                                    w_pa[l].astype(BF16), w_pb[l].astype(BF16), w_pc[l].astype(BF16),
                                    w_out[l].astype(BF16), ln1_g[l].reshape(1, D_MODEL),
                                    ln1_b[l].reshape(1, D_MODEL), wr, br, rows, tiles.merge_rows)
        meta_row = lambda r: meta[:, r, :].reshape(rows.total).astype(jnp.int32)
        plan = _moe_plan(meta_row(0), meta_row(ROUTE_RANK_LANE - ROUTE_GROUP_LANE), rows.total)
        y_moe = _moe_experts(h2, gate, plan, w_e1g, w_e3g, w_e2g, l, rows.total)
        x = _moe_combine(y_moe, plan, x1, mod[l], ln2_g[l].reshape(1, D_MODEL), ln2_b[l].reshape(1, D_MODEL), rows)
        x_parts = (x,)
        n_ctx_blocks = rows.ctx_rows // lat_len
        lru_fin = jnp.swapaxes(hf_b[:n_ctx_blocks], 1, 2).reshape(n_ctx, 2, W_B)
        finals.append((sf_a, lru_fin, cf, nf[:, :, :, 0, :], mf[:, :, :, 0, 0]))

    new_delta, new_lru, new_mc, new_mn, new_mm = (jnp.stack([f[i] for f in finals], axis=1) for i in range(5))
    y_prompt = x[:rows.ctx_rows].reshape(n_ctx, ctx_len, D_MODEL)
    y_sample = x[rows.ctx_rows:].reshape(n_lat, lat_len, D_MODEL)
    return (y_prompt, y_sample, new_delta, new_lru, new_mc, new_mn, new_mm)
```
